```python
import numpy as np
import jax
import jax.numpy as jnp
from jax import lax

D_MODEL = 1024
BATCH = 4
SEQ = 8192
DEPTH = 2
DEC_BATCH = 32
DEC_SEQ = 4
PAST_LEN = 16384
PAGE_SIZE = 128

HEAD_DIM = 64
MIX_WIDTH = D_MODEL
D_RNN = MIX_WIDTH // 2
RG_HEADS = D_RNN // HEAD_DIM
CONV_W = 4
RG_C = 8.0
MOBA_HEADS = (MIX_WIDTH - D_RNN) // HEAD_DIM
MOBA_BLOCK = 256
MOBA_TOPK = 3
NSA_HEADS = MIX_WIDTH // HEAD_DIM
NSA_KV_HEADS = 4
CMP_STRIDE = 16
CMP_LEN = 2 * CMP_STRIDE
CMP_HID = HEAD_DIM
SEL_BLOCK = 64
SEL_TOPK = 16
WINDOW = 512
N_GROUPS = 4
EXPERTS_PER_GROUP = 8
N_EXPERTS = N_GROUPS * EXPERTS_PER_GROUP
EXPERT_TOPK = 2
D_EXPERT = D_MODEL // 4
Q_BLOCK = 32
MOE_ROWS = 128
RMS_EPS = 1e-6
N_AB = (DEPTH + 1) // 2
N_C = DEPTH // 2
MOBA_W = MOBA_HEADS * HEAD_DIM
NSA_QW = NSA_HEADS * HEAD_DIM
NSA_KW = NSA_KV_HEADS * HEAD_DIM
AB_COLS = 2 * D_RNN + 3 * MOBA_W
C_COLS = NSA_QW + 6 * NSA_KW + 3 * NSA_HEADS

kernel_name = "hybrid_rglru_moba_nsa_hmoe_step"

F32 = jnp.float32


def rmsnorm(x, g):
    xf = x.astype(F32)
    y = xf * lax.rsqrt(jnp.mean(xf * xf, axis=-1, keepdims=True) + RMS_EPS)
    return (y * g.astype(F32)).astype(x.dtype)


def alibi_slopes(n):
    return jnp.asarray(2.0 ** (-8.0 * np.arange(1, n + 1) / n), dtype=F32)


def masked_softmax(logits, mask):
    logits = jnp.where(mask, logits, -jnp.inf)
    m = jnp.max(logits, axis=-1, keepdims=True)
    e = jnp.exp(logits - jnp.where(jnp.isfinite(m), m, 0.0))
    return e / jnp.maximum(jnp.sum(e, axis=-1, keepdims=True), 1e-30)


def _to_qblocks(x):
    n, t = x.shape[:2]
    return jnp.swapaxes(x.reshape(n, t // Q_BLOCK, Q_BLOCK, *x.shape[2:]), 0, 1)


def _from_qblocks(x):
    nq, n = x.shape[:2]
    return jnp.swapaxes(x, 0, 1).reshape(n, nq * Q_BLOCK, *x.shape[3:])


def _lin_combine(c1, c2):
    a1, b1 = c1
    a2, b2 = c2
    return a1 * a2, a2 * b1 + b2


def rglru(xr, gr, conv_buf, h0, conv_w, conv_b, w_a, b_a, w_i, b_i, lam):
    n, t, w = xr.shape
    xin = jnp.concatenate([conv_buf.astype(xr.dtype), xr], axis=1)
    xc = (conv_b + sum(xin[:, k:k + t] * conv_w[k] for k in range(CONV_W))).astype(F32)
    xh = xc.reshape(n, t, RG_HEADS, w // RG_HEADS)
    r = jax.nn.sigmoid(jnp.einsum('nthi,hij->nthj', xh, w_a.astype(F32)).reshape(n, t, w) + b_a.astype(F32))
    i = jax.nn.sigmoid(jnp.einsum('nthi,hij->nthj', xh, w_i.astype(F32)).reshape(n, t, w) + b_i.astype(F32))
    log_a = -RG_C * r * jax.nn.softplus(-lam.astype(F32))
    a = jnp.exp(log_a)
    b = jnp.sqrt(-jnp.expm1(2.0 * log_a)) * (i * xc)
    b = b.at[:, 0].add(a[:, 0] * h0.astype(F32))
    _, h = lax.associative_scan(_lin_combine, (a, b), axis=1)
    y = h * jax.nn.gelu(gr.astype(F32))
    return y.astype(xr.dtype), xin[:, -(CONV_W - 1):], h[:, -1].astype(xr.dtype)


def moba_blocks(k, v):
    n, t, h, d = k.shape
    nb = max(-(-t // MOBA_BLOCK), MOBA_TOPK)
    pad = ((0, 0), (0, nb * MOBA_BLOCK - t), (0, 0), (0, 0))
    kb = jnp.pad(k, pad).reshape(n, nb, MOBA_BLOCK, h, d).transpose(0, 3, 1, 2, 4)
    vb = jnp.pad(v, pad).reshape(n, nb, MOBA_BLOCK, h, d).transpose(0, 3, 1, 2, 4)
    kmean = jnp.mean(kb.astype(F32), axis=3)
    return kb, vb, kmean


def moba_query_block(q, q0, kb, vb, kmean, slopes):
    n, tq, h, d = q.shape
    nb = kb.shape[2]
    t = q0 + jnp.arange(tq)
    own = q0 // MOBA_BLOCK
    qf = q.astype(F32)
    gate = jnp.einsum('nqhd,nhbd->nqhb', qf, kmean)
    past_ok = jnp.arange(nb) < own
    _, idx = lax.top_k(jnp.where(past_ok, gate, -jnp.inf), MOBA_TOPK)
    n_i = jnp.arange(n)[:, None, None, None]
    h_i = jnp.arange(h)[None, None, :, None]
    kg = kb[n_i, h_i, idx].astype(F32)
    vg = vb[n_i, h_i, idx].astype(F32)
    qs = qf * d ** -0.5
    pos = idx[..., None] * MOBA_BLOCK + jnp.arange(MOBA_BLOCK)
    dist = (t[None, :, None, None, None] - pos).astype(F32)
    s_past = jnp.einsum('nqhd,nqhkjd->nqhkj', qs, kg) - slopes[None, None, :, None, None] * dist
    s_past = jnp.where((idx < own)[..., None], s_past, -jnp.inf)
    ko = lax.dynamic_index_in_dim(kb, own, axis=2, keepdims=False).astype(F32)
    vo = lax.dynamic_index_in_dim(vb, own, axis=2, keepdims=False).astype(F32)
    pos_o = own * MOBA_BLOCK + jnp.arange(MOBA_BLOCK)
    dist_o = (t[:, None] - pos_o[None, :]).astype(F32)
    s_own = jnp.einsum('nqhd,nhjd->nqhj', qs, ko) - slopes[None, None, :, None] * dist_o[None, :, None, :]
    s_own = jnp.where((dist_o >= 0)[None, :, None, :], s_own, -jnp.inf)
    p = jax.nn.softmax(jnp.concatenate([s_past.reshape(n, tq, h, -1), s_own], axis=-1), axis=-1)
    n_past = MOBA_TOPK * MOBA_BLOCK
    out = (jnp.einsum('nqhkj,nqhkjd->nqhd', p[..., :n_past].reshape(n, tq, h, MOBA_TOPK, MOBA_BLOCK), vg)
           + jnp.einsum('nqhj,nhjd->nqhd', p[..., n_past:], vo))
    return out.astype(q.dtype)


def moba_prompt(q, k, v, slopes):
    kb, vb, kmean = moba_blocks(k, v)
    nq = q.shape[1] // Q_BLOCK
    out = lax.map(lambda a: moba_query_block(a[0], a[1], kb, vb, kmean, slopes),
                  (_to_qblocks(q), jnp.arange(nq, dtype=jnp.int32) * Q_BLOCK))
    return _from_qblocks(out)


def moba_sample(q, k_new, v_new, cache_k, cache_v, page_table, layer, slopes):
    past = page_table.shape[1] * PAGE_SIZE

    def one(a):
        qs, kn, vn, pages = a
        kf = jnp.concatenate([cache_k[layer, pages].reshape(past, *kn.shape[1:]).astype(kn.dtype), kn], 0)
        vf = jnp.concatenate([cache_v[layer, pages].reshape(past, *vn.shape[1:]).astype(vn.dtype), vn], 0)
        kb, vb, kmean = moba_blocks(kf[None], vf[None])
        return moba_query_block(qs[None], past, kb, vb, kmean, slopes)[0]

    return lax.map(one, (q, k_new, v_new, page_table))


def nsa_compress(k, w1, w2, pe):
    n, t, g, d = k.shape
    c = t // CMP_STRIDE
    ch = k[:, :c * CMP_STRIDE].reshape(n, c, CMP_STRIDE, g, d)
    blk = jnp.concatenate([ch[:, :-1], ch[:, 1:]], axis=2) + pe[None, None, :, None, :]
    flat = blk.transpose(0, 1, 3, 2, 4).reshape(n, c - 1, g, CMP_LEN * d)
    return jax.nn.gelu(flat @ w1) @ w2


def nsa_sel_blocks(k):
    n, t, g, d = k.shape
    nb = max(-(-t // SEL_BLOCK), SEL_TOPK)
    kp = jnp.pad(k, ((0, 0), (0, nb * SEL_BLOCK - t), (0, 0), (0, 0)))
    return kp.reshape(n, nb, SEL_BLOCK, g, d).transpose(0, 3, 1, 2, 4)


def nsa_cover(n_cmp, n_sel):
    i = np.arange(n_cmp)[:, None]
    j = np.arange(n_sel)[None, :]
    cov = (CMP_STRIDE * i < SEL_BLOCK * (j + 1)) & (CMP_STRIDE * i + CMP_LEN > SEL_BLOCK * j)
    return jnp.asarray(cov.astype(np.float32))


def nsa_prepare(kc, vc, ks, vs, w_k1, w_k2, pe_k, w_v1, w_v2, pe_v):
    ck = nsa_compress(kc, w_k1, w_k2, pe_k)
    cv = nsa_compress(vc, w_v1, w_v2, pe_v)
    skb = nsa_sel_blocks(ks)
    svb = nsa_sel_blocks(vs)
    return ck, cv, skb, svb, nsa_cover(ck.shape[1], skb.shape[2])


def nsa_query_block(q, q0, ck, cv, sk, sv, cover, wk, wv, wpos, gates, slopes):
    n, tq, h, d = q.shape
    g = ck.shape[2]
    r = h // g
    t = q0 + jnp.arange(tq)
    sl = slopes.reshape(g, r)
    qg = q.astype(F32).reshape(n, tq, g, r, d) * d ** -0.5
    cend = CMP_STRIDE * jnp.arange(ck.shape[1]) + CMP_LEN - 1
    dist_c = t[:, None] - cend[None, :]
    s_c = (jnp.einsum('nqgrd,nigd->nqgri', qg, ck.astype(F32))
           - sl[None, None, :, :, None] * dist_c.astype(F32)[None, :, None, None, :])
    p_c = masked_softmax(s_c, (dist_c >= 0)[None, :, None, None, :])
    o_c = jnp.einsum('nqgri,nigd->nqgrd', p_c, cv.astype(F32))
    imp = jnp.einsum('nqgri,ij->nqgj', p_c, cover)
    jb = jnp.arange(sk.shape[2])[None, :]
    qb = (t // SEL_BLOCK)[:, None]
    vis = (jb <= qb)[None, :, None, :]
    forced = ((jb == 0) | (jb == qb) | (jb == qb - 1))[None, :, None, :]
    score = jnp.where(vis, jnp.where(forced, jnp.inf, imp), -jnp.inf)
    _, idx = lax.top_k(score, SEL_TOPK)
    n_i = jnp.arange(n)[:, None, None, None]
    g_i = jnp.arange(g)[None, None, :, None]
    kg = sk[n_i, g_i, idx].astype(F32)
    vg = sv[n_i, g_i, idx].astype(F32)
    pos = idx[..., None] * SEL_BLOCK + jnp.arange(SEL_BLOCK)
    dist_s = t[None, :, None, None, None] - pos
    ok_s = (idx <= qb[None, :, :, None])[..., None] & (dist_s >= 0)
    s_s = (jnp.einsum('nqgrd,nqgkjd->nqgrkj', qg, kg)
           - sl[None, None, :, :, None, None] * dist_s.astype(F32)[:, :, :, None])
    p_s = masked_softmax(s_s.reshape(n, tq, g, r, -1), ok_s[:, :, :, None].reshape(n, tq, g, 1, -1))
    o_s = jnp.einsum('nqgrkj,nqgkjd->nqgrd', p_s.reshape(s_s.shape), vg)
    dist_w = t[:, None] - wpos[None, :]
    ok_w = (dist_w >= 0) & (dist_w < WINDOW) & (wpos >= 0)[None, :]
    s_w = (jnp.einsum('nqgrd,nsgd->nqgrs', qg, wk.astype(F32))
           - sl[None, None, :, :, None] * dist_w.astype(F32)[None, :, None, None, :])
    p_w = masked_softmax(s_w, ok_w[None, :, None, None, :])
    o_w = jnp.einsum('nqgrs,nsgd->nqgrd', p_w, wv.astype(F32))
    gt = jax.nn.sigmoid(gates.astype(F32)).reshape(n, tq, g, r, 3)
    o = gt[..., 0:1] * o_c + gt[..., 1:2] * o_s + gt[..., 2:3] * o_w
    return o.reshape(n, tq, h, d).astype(q.dtype)


def nsa_prompt(q, kc, vc, ks, vs, kw, vw, gates, cmp_params, slopes):
    ck, cv, skb, svb, cover = nsa_prepare(kc, vc, ks, vs, *cmp_params)
    pad = ((0, 0), (WINDOW, 0), (0, 0), (0, 0))
    kwp = jnp.pad(kw, pad)
    vwp = jnp.pad(vw, pad)
    nq = q.shape[1] // Q_BLOCK

    def one(a):
        qb, gb, q0 = a
        wk = lax.dynamic_slice_in_dim(kwp, q0, WINDOW + Q_BLOCK, axis=1)
        wv = lax.dynamic_slice_in_dim(vwp, q0, WINDOW + Q_BLOCK, axis=1)
        wpos = q0 - WINDOW + jnp.arange(WINDOW + Q_BLOCK)
        return nsa_query_block(qb, q0, ck, cv, skb, svb, cover, wk, wv, wpos, gb, slopes)

    out = lax.map(one, (_to_qblocks(q), _to_qblocks(gates), jnp.arange(nq, dtype=jnp.int32) * Q_BLOCK))
    return _from_qblocks(out)


def nsa_sample(q, kc, vc, ks, vs, kw, vw, gates, cache_ck, cache_cv, cache_sk, cache_sv,
               win_k, win_v, page_table, layer, cmp_params, slopes):
    past = page_table.shape[1] * PAGE_SIZE
    wbuf = win_k.shape[1]

    def with_past(cache, pages, new):
        rows = cache[layer, pages].reshape(past, *new.shape[1:]).astype(new.dtype)
        return jnp.concatenate([rows, new], axis=0)[None]

    def one(a):
        qs, kcn, vcn, ksn, vsn, kwn, vwn, gs, wkb, wvb, pages = a
        ck, cv, skb, svb, cover = nsa_prepare(with_past(cache_ck, pages, kcn), with_past(cache_cv, pages, vcn),
                                              with_past(cache_sk, pages, ksn), with_past(cache_sv, pages, vsn),
                                              *cmp_params)
        wk = jnp.concatenate([wkb.astype(kwn.dtype), kwn], 0)[None]
        wv = jnp.concatenate([wvb.astype(vwn.dtype), vwn], 0)[None]
        wpos = past - wbuf + jnp.arange(wbuf + qs.shape[0])
        return nsa_query_block(qs[None], past, ck, cv, skb, svb, cover, wk, wv, wpos, gs[None], slopes)[0]

    return lax.map(one, (q, kc, vc, ks, vs, kw, vw, gates, win_k, win_v, page_table))


def moe_dispatch(x, experts, weights, w1, w3, w2):
    m, d = x.shape
    n_exp = w1.shape[0]
    a = m * EXPERT_TOPK
    e = experts.reshape(-1)
    tok = jnp.arange(a, dtype=jnp.int32) // EXPERT_TOPK
    wt = weights.reshape(-1)
    order = jnp.argsort(e)
    e_s, tok_s, wt_s = e[order], tok[order], wt[order]
    counts = jnp.bincount(e, length=n_exp)
    padded = (counts + MOE_ROWS - 1) // MOE_ROWS * MOE_ROWS
    start = jnp.cumsum(counts) - counts
    pend = jnp.cumsum(padded)
    dest = (pend - padded)[e_s] + jnp.arange(a) - start[e_s]
    n_blk = -(-(a + n_exp * (MOE_ROWS - 1)) // MOE_ROWS)
    rows = n_blk * MOE_ROWS
    buf_tok = jnp.full((rows,), m, jnp.int32).at[dest].set(tok_s)
    buf_w = jnp.zeros((rows,), F32).at[dest].set(wt_s.astype(F32))
    blk_e = jnp.minimum(jnp.searchsorted(pend, jnp.arange(n_blk) * MOE_ROWS, side='right'), n_exp - 1)
    xp = jnp.concatenate([x, jnp.zeros((1, d), x.dtype)], axis=0)

    def expert_block(arg):
        ids, eid = arg
        xb = xp[ids]
        hb = jax.nn.silu(xb @ w1[eid]) * (xb @ w3[eid])
        return hb @ w2[eid]

    yb = lax.map(expert_block, (buf_tok.reshape(n_blk, MOE_ROWS), blk_e))
    y = jnp.zeros((m + 1, d), F32).at[buf_tok].add(yb.reshape(rows, d).astype(F32) * buf_w[:, None])
    return y[:m].astype(x.dtype)


def hier_moe(h, w_rg, b_rg, w_re, b_re, w1, w3, w2):
    shp = h.shape
    x = h.reshape(-1, shp[-1])
    m = x.shape[0]
    g_logits = (x @ w_rg).astype(F32) + b_rg.astype(F32)
    g_idx = jnp.argmax(g_logits, axis=-1)
    g_w = jnp.take_along_axis(jax.nn.softmax(g_logits, axis=-1), g_idx[:, None], axis=-1)
    e_logits = ((x @ w_re).astype(F32) + b_re.astype(F32)).reshape(m, N_GROUPS, EXPERTS_PER_GROUP)
    e_in = jnp.take_along_axis(e_logits, g_idx[:, None, None], axis=1)[:, 0]
    top_v, top_i = lax.top_k(e_in, EXPERT_TOPK)
    weights = g_w * jax.nn.softmax(top_v, axis=-1)
    experts = g_idx[:, None] * EXPERTS_PER_GROUP + top_i
    return moe_dispatch(x, experts, weights, w1, w3, w2).reshape(shp)


def split_ab(z):
    n, t, _ = z.shape
    xr, gr, q, k, v = jnp.split(z, [D_RNN, 2 * D_RNN, 2 * D_RNN + MOBA_W, 2 * D_RNN + 2 * MOBA_W], axis=-1)
    hs = (n, t, MOBA_HEADS, HEAD_DIM)
    return xr, gr, q.reshape(hs), k.reshape(hs), v.reshape(hs)


def split_c(z):
    n, t, _ = z.shape
    cuts = [NSA_QW + i * NSA_KW for i in range(7)]
    q, kc, vc, ks, vs, kw, vw, gt = jnp.split(z, cuts, axis=-1)
    kv = (n, t, NSA_KV_HEADS, HEAD_DIM)
    return (q.reshape(n, t, NSA_HEADS, HEAD_DIM), kc.reshape(kv), vc.reshape(kv), ks.reshape(kv),
            vs.reshape(kv), kw.reshape(kv), vw.reshape(kv), gt.reshape(n, t, NSA_HEADS, 3))


def setup_inputs(seed: int = 0) -> dict:
    key = jax.random.key(seed)
    ks = iter(jax.random.split(key, 64))

    def nrm(shape, scale):
        return scale * jax.random.normal(next(ks), shape, F32)

    n_pages = PAST_LEN // PAGE_SIZE
    n_used = DEC_BATCH * n_pages
    n_phys = n_used + max(n_used // 4, 1)
    w_buf = min(WINDOW, PAST_LEN)
    page_table = jax.random.permutation(next(ks), n_phys)[:n_used].reshape(DEC_BATCH, n_pages).astype(jnp.int32)
    a0 = jax.random.uniform(next(ks), (N_AB, D_RNN), F32, 0.9, 0.999)
    return {
        "x_prompt": nrm((BATCH, SEQ, D_MODEL), 1.0),
        "x_sample": nrm((DEC_BATCH, DEC_SEQ, D_MODEL), 1.0),
        "cache_moba_k": nrm((N_AB, n_phys, PAGE_SIZE, MOBA_HEADS, HEAD_DIM), 1.0),
        "cache_moba_v": nrm((N_AB, n_phys, PAGE_SIZE, MOBA_HEADS, HEAD_DIM), 1.0),
        "cache_cmp_k": nrm((N_C, n_phys, PAGE_SIZE, NSA_KV_HEADS, HEAD_DIM), 1.0),
        "cache_cmp_v": nrm((N_C, n_phys, PAGE_SIZE, NSA_KV_HEADS, HEAD_DIM), 1.0),
        "cache_sel_k": nrm((N_C, n_phys, PAGE_SIZE, NSA_KV_HEADS, HEAD_DIM), 1.0),
        "cache_sel_v": nrm((N_C, n_phys, PAGE_SIZE, NSA_KV_HEADS, HEAD_DIM), 1.0),
        "state_win_k": nrm((N_C, DEC_BATCH, w_buf, NSA_KV_HEADS, HEAD_DIM), 1.0),
        "state_win_v": nrm((N_C, DEC_BATCH, w_buf, NSA_KV_HEADS, HEAD_DIM), 1.0),
        "state_conv": nrm((N_AB, DEC_BATCH, CONV_W - 1, D_RNN), 1.0),
        "state_rglru": nrm((N_AB, DEC_BATCH, D_RNN), 0.5),
        "page_table": page_table,
        "norm_mix": 1.0 + nrm((DEPTH, D_MODEL), 0.1),
        "norm_ffn": 1.0 + nrm((DEPTH, D_MODEL), 0.1),
        "norm_final": 1.0 + nrm((D_MODEL,), 0.1),
        "w_in_ab": nrm((N_AB, D_MODEL, AB_COLS), D_MODEL ** -0.5),
        "w_out_ab": nrm((N_AB, D_RNN + MOBA_W, D_MODEL), (D_RNN + MOBA_W) ** -0.5),
        "conv_w": nrm((N_AB, CONV_W, D_RNN), 0.5),
        "conv_b": nrm((N_AB, D_RNN), 0.01),
        "w_rg_a": nrm((N_AB, RG_HEADS, D_RNN // RG_HEADS, D_RNN // RG_HEADS), (D_RNN // RG_HEADS) ** -0.5),
        "b_rg_a": nrm((N_AB, D_RNN), 0.01),
        "w_rg_i": nrm((N_AB, RG_HEADS, D_RNN // RG_HEADS, D_RNN // RG_HEADS), (D_RNN // RG_HEADS) ** -0.5),
        "b_rg_i": nrm((N_AB, D_RNN), 0.01),
        "rg_lambda": jnp.log(a0) - jnp.log1p(-a0),
        "w_in_c": nrm((N_C, D_MODEL, C_COLS), D_MODEL ** -0.5),
        "w_out_c": nrm((N_C, NSA_QW, D_MODEL), NSA_QW ** -0.5),
        "w_cmp_k1": nrm((N_C, CMP_LEN * HEAD_DIM, CMP_HID), (CMP_LEN * HEAD_DIM) ** -0.5),
        "w_cmp_k2": nrm((N_C, CMP_HID, HEAD_DIM), CMP_HID ** -0.5),
        "pe_cmp_k": nrm((N_C, CMP_LEN, HEAD_DIM), 0.1),
        "w_cmp_v1": nrm((N_C, CMP_LEN * HEAD_DIM, CMP_HID), (CMP_LEN * HEAD_DIM) ** -0.5),
        "w_cmp_v2": nrm((N_C, CMP_HID, HEAD_DIM), CMP_HID ** -0.5),
        "pe_cmp_v": nrm((N_C, CMP_LEN, HEAD_DIM), 0.1),
        "w_route_group": nrm((DEPTH, D_MODEL, N_GROUPS), D_MODEL ** -0.5),
        "b_route_group": nrm((DEPTH, N_GROUPS), 0.01),
        "w_route_expert": nrm((DEPTH, D_MODEL, N_EXPERTS), D_MODEL ** -0.5),
        "b_route_expert": nrm((DEPTH, N_EXPERTS), 0.01),
        "w_exp_gate": nrm((DEPTH, N_EXPERTS, D_MODEL, D_EXPERT), D_MODEL ** -0.5),
        "w_exp_up": nrm((DEPTH, N_EXPERTS, D_MODEL, D_EXPERT), D_MODEL ** -0.5),
        "w_exp_down": nrm((DEPTH, N_EXPERTS, D_EXPERT, D_MODEL), D_EXPERT ** -0.5),
    }


def reference(x_prompt, x_sample, cache_moba_k, cache_moba_v, cache_cmp_k, cache_cmp_v,
              cache_sel_k, cache_sel_v, state_win_k, state_win_v, state_conv, state_rglru,
              page_table, norm_mix, norm_ffn, norm_final, w_in_ab, w_out_ab, conv_w, conv_b,
              w_rg_a, b_rg_a, w_rg_i, b_rg_i, rg_lambda, w_in_c, w_out_c, w_cmp_k1, w_cmp_k2,
              pe_cmp_k, w_cmp_v1, w_cmp_v2, pe_cmp_v, w_route_group, b_route_group,
              w_route_expert, b_route_expert, w_exp_gate, w_exp_up, w_exp_down):
    slopes_b = alibi_slopes(MOBA_HEADS)
    slopes_c = alibi_slopes(NSA_HEADS)
    bp, t_p, _ = x_prompt.shape
    bs, t_s, _ = x_sample.shape
    w_keep = min(WINDOW, t_p)
    w_buf = state_win_k.shape[2]
    yp, ys = x_prompt, x_sample
    l_mk_p, l_mk_s, l_mv_p, l_mv_s, l_cb_p, l_cb_s, l_h_p, l_h_s = [], [], [], [], [], [], [], []
    l_ck_p, l_ck_s, l_cv_p, l_cv_s, l_sk_p, l_sk_s, l_sv_p, l_sv_s = [], [], [], [], [], [], [], []
    l_wk_p, l_wk_s, l_wv_p, l_wv_s = [], [], [], []
    for layer in range(DEPTH):
        l = layer // 2
        if layer % 2 == 0:
            rg = (conv_w[l], conv_b[l], w_rg_a[l], b_rg_a[l], w_rg_i[l], b_rg_i[l], rg_lambda[l])
            xr, gr, q, k, v = split_ab(rmsnorm(yp, norm_mix[layer]) @ w_in_ab[l])
            r, cbuf, hfin = rglru(xr, gr, jnp.zeros((bp, CONV_W - 1, D_RNN), yp.dtype),
                                  jnp.zeros((bp, D_RNN), yp.dtype), *rg)
            att = moba_prompt(q, k, v, slopes_b)
            yp = yp + jnp.concatenate([r, att.reshape(bp, t_p, MOBA_W)], axis=-1) @ w_out_ab[l]
            l_mk_p.append(k); l_mv_p.append(v); l_cb_p.append(cbuf); l_h_p.append(hfin)
            xr, gr, q, k, v = split_ab(rmsnorm(ys, norm_mix[layer]) @ w_in_ab[l])
            r, cbuf, hfin = rglru(xr, gr, state_conv[l], state_rglru[l], *rg)
            att = moba_sample(q, k, v, cache_moba_k, cache_moba_v, page_table, l, slopes_b)
            ys = ys + jnp.concatenate([r, att.reshape(bs, t_s, MOBA_W)], axis=-1) @ w_out_ab[l]
            l_mk_s.append(k); l_mv_s.append(v); l_cb_s.append(cbuf); l_h_s.append(hfin)
        else:
            cmp_params = (w_cmp_k1[l], w_cmp_k2[l], pe_cmp_k[l], w_cmp_v1[l], w_cmp_v2[l], pe_cmp_v[l])
            q, kc, vc, ks_, vs_, kw, vw, gt = split_c(rmsnorm(yp, norm_mix[layer]) @ w_in_c[l])
            att = nsa_prompt(q, kc, vc, ks_, vs_, kw, vw, gt, cmp_params, slopes_c)
            yp = yp + att.reshape(bp, t_p, NSA_QW) @ w_out_c[l]
            l_ck_p.append(kc); l_cv_p.append(vc); l_sk_p.append(ks_); l_sv_p.append(vs_)
            l_wk_p.append(kw[:, t_p - w_keep:]); l_wv_p.append(vw[:, t_p - w_keep:])
            q, kc, vc, ks_, vs_, kw, vw, gt = split_c(rmsnorm(ys, norm_mix[layer]) @ w_in_c[l])
            att = nsa_sample(q, kc, vc, ks_, vs_, kw, vw, gt, cache_cmp_k, cache_cmp_v, cache_sel_k,
                             cache_sel_v, state_win_k[l], state_win_v[l], page_table, l, cmp_params, slopes_c)
            ys = ys + att.reshape(bs, t_s, NSA_QW) @ w_out_c[l]
            l_ck_s.append(kc); l_cv_s.append(vc); l_sk_s.append(ks_); l_sv_s.append(vs_)
            l_wk_s.append(jnp.concatenate([state_win_k[l].astype(kw.dtype), kw], axis=1)[:, -w_buf:])
            l_wv_s.append(jnp.concatenate([state_win_v[l].astype(vw.dtype), vw], axis=1)[:, -w_buf:])
        moe_w = (w_route_group[layer], b_route_group[layer], w_route_expert[layer], b_route_expert[layer],
                 w_exp_gate[layer], w_exp_up[layer], w_exp_down[layer])
        yp = yp + hier_moe(rmsnorm(yp, norm_ffn[layer]), *moe_w)
        ys = ys + hier_moe(rmsnorm(ys, norm_ffn[layer]), *moe_w)
    y_prompt = rmsnorm(yp, norm_final)
    y_sample = rmsnorm(ys, norm_final)
    moba_k_p, moba_k_s = jnp.stack(l_mk_p), jnp.stack(l_mk_s)
    moba_v_p, moba_v_s = jnp.stack(l_mv_p), jnp.stack(l_mv_s)
    conv_p, conv_s = jnp.stack(l_cb_p), jnp.stack(l_cb_s)
    rglru_p, rglru_s = jnp.stack(l_h_p), jnp.stack(l_h_s)
    cmp_k_p, cmp_k_s = jnp.stack(l_ck_p), jnp.stack(l_ck_s)
    cmp_v_p, cmp_v_s = jnp.stack(l_cv_p), jnp.stack(l_cv_s)
    sel_k_p, sel_k_s = jnp.stack(l_sk_p), jnp.stack(l_sk_s)
    sel_v_p, sel_v_s = jnp.stack(l_sv_p), jnp.stack(l_sv_s)
    win_k_p, win_k_s = jnp.stack(l_wk_p), jnp.stack(l_wk_s)
    win_v_p, win_v_s = jnp.stack(l_wv_p), jnp.stack(l_wv_s)
    return (y_prompt, y_sample, moba_k_p, moba_k_s, moba_v_p, moba_v_s, conv_p, conv_s, rglru_p, rglru_s,
            cmp_k_p, cmp_k_s, cmp_v_p, cmp_v_s, sel_k_p, sel_k_s, sel_v_p, sel_v_s,
            win_k_p, win_k_s, win_v_p, win_v_s)
```

```python
import functools

import numpy as np
import jax
import jax.numpy as jnp
from jax import lax
from jax.experimental import pallas as pl
from jax.experimental.pallas import tpu as pltpu

F32 = jnp.float32
BF16 = jnp.bfloat16
HIGHEST = lax.Precision.HIGHEST

HEAD_DIM = 64
LANES = 128
SUBLANES = 8
CONV_W = 4
RG_C = 8.0
MOBA_BLOCK = 256
MOBA_TOPK = 3
CMP_STRIDE = 16
CMP_LEN = 32
SEL_BLOCK = 64
SEL_TOPK = 16
WINDOW = 512
N_GROUPS = 4
EXPERTS_PER_GROUP = 8
N_EXPERTS = N_GROUPS * EXPERTS_PER_GROUP
PAGE_SIZE = 128
RMS_EPS = 1e-6
NEG_BIG = -1e30
VMEM_LIMIT = 56 * 1024 * 1024
MOE_BLOCK_ROWS = 256


def _cparams(sem):
    return pltpu.CompilerParams(dimension_semantics=sem, vmem_limit_bytes=VMEM_LIMIT)


def _alibi_slopes(n):
    return np.asarray(2.0 ** (-8.0 * np.arange(1, n + 1) / n), dtype=np.float32)


def _row_tile(m, pref=512):
    return pref if m % pref == 0 else m


def _dot_nt(a, b, precision=None):
    return lax.dot_general(a, b, (((1,), (1,)), ((), ())), precision=precision,
                           preferred_element_type=F32)


def _norm_proj_body(out_map, x_ref, g_ref, *refs):
    n_w = max(out_map) + 1
    w_refs, o_refs = refs[:n_w], refs[n_w:]
    x = x_ref[...]
    y = x * lax.rsqrt(jnp.mean(x * x, axis=-1, keepdims=True) + RMS_EPS) * g_ref[...]
    yb = y.astype(BF16)
    prods = [jnp.dot(yb, w_ref[...], preferred_element_type=F32) for w_ref in w_refs]
    for wi, o_ref in zip(out_map, o_refs):
        o_ref[...] = prods[wi].astype(o_ref.dtype)


def norm_proj(x, g, ws, outs):
    m, d = x.shape
    tm = _row_tile(m)
    out_map = tuple(wi for wi, _ in outs)
    return pl.pallas_call(
        functools.partial(_norm_proj_body, out_map),
        grid=(m // tm,),
        in_specs=[pl.BlockSpec((tm, d), lambda i: (i, 0)), pl.BlockSpec((1, d), lambda i: (0, 0))]
        + [pl.BlockSpec(w.shape, lambda i: (0, 0)) for w in ws],
        out_specs=[pl.BlockSpec((tm, ws[wi].shape[1]), lambda i: (i, 0)) for wi, _ in outs],
        out_shape=[jax.ShapeDtypeStruct((m, ws[wi].shape[1]), dt) for wi, dt in outs],
        compiler_params=_cparams(("parallel",)),
        name="norm_proj",
    )(x, g.reshape(1, d), *ws)


def _proj_res_body(n_in, *refs):
    x_refs, w_refs = refs[:n_in], refs[n_in:2 * n_in]
    res_ref, o_ref = refs[2 * n_in], refs[2 * n_in + 1]
    acc = jnp.dot(x_refs[0][...].astype(BF16), w_refs[0][...], preferred_element_type=F32)
    for x_ref, w_ref in zip(x_refs[1:], w_refs[1:]):
        acc = acc + jnp.dot(x_ref[...].astype(BF16), w_ref[...], preferred_element_type=F32)
    o_ref[...] = res_ref[...] + acc


def proj_residual(xs, ws, res):
    m, d = res.shape
    tm = _row_tile(m)
    return pl.pallas_call(
        functools.partial(_proj_res_body, len(xs)),
        grid=(m // tm,),
        in_specs=[pl.BlockSpec((tm, x.shape[1]), lambda i: (i, 0)) for x in xs]
        + [pl.BlockSpec(w.shape, lambda i: (0, 0)) for w in ws]
        + [pl.BlockSpec((tm, d), lambda i: (i, 0))],
        out_specs=pl.BlockSpec((tm, d), lambda i: (i, 0)),
        out_shape=jax.ShapeDtypeStruct((m, d), F32),
        compiler_params=_cparams(("parallel",)),
        name="proj_residual",
    )(*xs, *ws, res)


def _rmsnorm_body(x_ref, g_ref, o_ref):
    x = x_ref[...]
    o_ref[...] = x * lax.rsqrt(jnp.mean(x * x, axis=-1, keepdims=True) + RMS_EPS) * g_ref[...]


def rmsnorm(x, g):
    m, d = x.shape
    tm = _row_tile(m)
    return pl.pallas_call(
        _rmsnorm_body,
        grid=(m // tm,),
        in_specs=[pl.BlockSpec((tm, d), lambda i: (i, 0)), pl.BlockSpec((1, d), lambda i: (0, 0))],
        out_specs=pl.BlockSpec((tm, d), lambda i: (i, 0)),
        out_shape=jax.ShapeDtypeStruct((m, d), F32),
        compiler_params=_cparams(("parallel",)),
        name="rmsnorm",
    )(x, g.reshape(1, d))


_XIN_PAD = 8


def _rglru_body(xr_ref, gr_ref, cb_ref, h0_ref, cw_ref, cbias_ref, wa_ref, ba_ref, wi_ref, bi_ref,
                lam_ref, y_ref, tail_ref, hl_ref, xin_s, a_s, b_s, h_s):
    tt = xr_ref.shape[1]
    keep = CONV_W - 1

    @pl.when(pl.program_id(1) == 0)
    def _():
        xin_s[_XIN_PAD - keep:_XIN_PAD, :] = cb_ref[0]
        h_s[...] = h0_ref[0]

    xin_s[_XIN_PAD:_XIN_PAD + tt, :] = xr_ref[0]
    xc = cbias_ref[...]
    for k in range(CONV_W):
        lo = _XIN_PAD - keep + k
        xc = xc + xin_s[lo:lo + tt, :] * cw_ref[k:k + 1, :]
    new_tail = xin_s[_XIN_PAD + tt - keep:_XIN_PAD + tt, :]
    xin_s[_XIN_PAD - keep:_XIN_PAD, :] = new_tail
    tail_ref[0] = new_tail

    xcb = xc.astype(BF16)
    r = jax.nn.sigmoid(jnp.dot(xcb, wa_ref[...], preferred_element_type=F32) + ba_ref[...])
    gi = jax.nn.sigmoid(jnp.dot(xcb, wi_ref[...], preferred_element_type=F32) + bi_ref[...])
    nl = -lam_ref[...]
    softplus = jnp.maximum(nl, 0.0) + jnp.log1p(jnp.exp(-jnp.abs(nl)))
    log_a = (-RG_C) * r * softplus
    a = jnp.exp(log_a)
    a_s[...] = a
    b_s[...] = jnp.sqrt(1.0 - a * a) * (gi * xc)

    def step(t, h):
        h = a_s[pl.ds(t, 1), :] * h + b_s[pl.ds(t, 1), :]
        b_s[pl.ds(t, 1), :] = h
        return h

    h = lax.fori_loop(0, tt, step, h_s[...], unroll=min(tt, 8))
    h_s[...] = h
    hl_ref[0] = h
    y_ref[0] = b_s[...] * jax.nn.gelu(gr_ref[0])


def rglru(xr, gr, conv_buf, h0, conv_w, conv_b, wa_bd, b_a, wi_bd, b_i, lam):
    n, t, w = xr.shape
    tt = min(t, 512)
    assert t % tt == 0 and tt >= CONV_W - 1
    row = lambda v: v.reshape(1, w)
    full = lambda shp: pl.BlockSpec(shp, lambda b, s: (0,) * len(shp))
    y, tail, hl = pl.pallas_call(
        _rglru_body,
        grid=(n, t // tt),
        in_specs=[pl.BlockSpec((1, tt, w), lambda b, s: (b, s, 0)),
                  pl.BlockSpec((1, tt, w), lambda b, s: (b, s, 0)),
                  pl.BlockSpec((1, CONV_W - 1, w), lambda b, s: (b, 0, 0)),
                  pl.BlockSpec((1, 1, w), lambda b, s: (b, 0, 0)),
                  full((CONV_W, w)), full((1, w)), full((w, w)), full((1, w)), full((w, w)),
                  full((1, w)), full((1, w))],
        out_specs=[pl.BlockSpec((1, tt, w), lambda b, s: (b, s, 0)),
                   pl.BlockSpec((1, CONV_W - 1, w), lambda b, s: (b, 0, 0)),
                   pl.BlockSpec((1, 1, w), lambda b, s: (b, 0, 0))],
        out_shape=[jax.ShapeDtypeStruct((n, t, w), F32),
                   jax.ShapeDtypeStruct((n, CONV_W - 1, w), F32),
                   jax.ShapeDtypeStruct((n, 1, w), F32)],
        scratch_shapes=[pltpu.VMEM((_XIN_PAD + tt, w), F32), pltpu.VMEM((tt, w), F32),
                        pltpu.VMEM((tt, w), F32), pltpu.VMEM((1, w), F32)],
        compiler_params=_cparams(("parallel", "arbitrary")),
        name="rglru",
    )(xr, gr, conv_buf, h0.reshape(n, 1, w), conv_w, row(conv_b), wa_bd, row(b_a), wi_bd, row(b_i), row(lam))
    return y, tail, hl.reshape(n, w)


def _block_diag(w):
    h, d, _ = w.shape
    eye = jnp.eye(h, dtype=w.dtype)
    return (eye[:, None, :, None] * w[:, :, None, :]).reshape(h * d, h * d)


def _block_mean_body(blk, k_ref, o_ref):
    x = k_ref[0]
    nb = x.shape[0] // blk
    o_ref[0] = jnp.mean(x.reshape(nb, blk, x.shape[1]), axis=1)


def block_mean(k, blk):
    n, t, w = k.shape
    nb = t // blk
    g = SUBLANES if nb % SUBLANES == 0 else nb
    return pl.pallas_call(
        functools.partial(_block_mean_body, blk),
        grid=(n, nb // g),
        in_specs=[pl.BlockSpec((1, g * blk, w), lambda b, s: (b, s, 0))],
        out_specs=pl.BlockSpec((1, g, w), lambda b, s: (b, s, 0)),
        out_shape=jax.ShapeDtypeStruct((n, nb, w), F32),
        compiler_params=_cparams(("parallel", "parallel")),
        name="block_mean",
    )(k)


def _first_argmax(g, idx, sentinel):
    m = jnp.max(g, axis=-1, keepdims=True)
    return jnp.min(jnp.where(g == m, idx, sentinel), axis=-1, keepdims=True)


def _split_hi_lo(x):
    hi = x.astype(BF16)
    lo = (x - hi.astype(F32)).astype(BF16)
    return hi, lo


def _softmax_step(carry, s, v, c=None):
    m, l, acc = carry
    smax = jnp.max(s, axis=-1, keepdims=True)
    if c is None:
        m_new = jnp.maximum(m, smax)
        p = jnp.exp(s - m_new)
    else:
        m_new = jnp.maximum(m, smax + c)
        p = jnp.exp(s - (m_new - c))
    alpha = jnp.exp(m - m_new)
    l = alpha * l + jnp.sum(p, axis=-1, keepdims=True)
    acc = alpha * acc + jnp.dot(p.astype(BF16), v, preferred_element_type=F32)
    return m_new, l, acc


def _moba_prompt_body(nbk, slopes_ref, q_ref, km_ref, kb_ref, vb_ref, o_ref):
    p_idx = pl.program_id(1)
    i = pl.program_id(2)
    tq = q_ref.shape[1]
    q = q_ref[0]
    lane = lax.broadcasted_iota(jnp.int32, (tq, LANES), 1)
    lane_f = lane.astype(F32)
    km = km_ref[0]
    halves = []
    for hh in range(2):
        in_head = (lane >= hh * HEAD_DIM) & (lane < (hh + 1) * HEAD_DIM)
        qh = jnp.where(in_head, q, 0.0)
        gate = _dot_nt(qh, km, HIGHEST)
        g = jnp.where(lane < i, gate, -jnp.inf)
        sel = lane == i
        for _ in range(MOBA_TOPK):
            pick = lane == _first_argmax(g, lane, LANES)
            sel = sel | (pick & (lane < i))
            g = jnp.where(pick, -jnp.inf, g)
        slope = slopes_ref[2 * p_idx + hh]
        pb = jnp.where(sel, 0.0, NEG_BIG) + slope * (lane_f * MOBA_BLOCK)
        pb_hi, pb_lo = _split_hi_lo(pb)
        pb_lo = pltpu.roll(pb_lo.astype(F32), nbk, 1)
        s_hi, s_lo = _split_hi_lo(jnp.full((tq, LANES), slope, F32))
        aug = jnp.where(lane < nbk, pb_hi.astype(F32),
                        jnp.where(lane < 2 * nbk, pb_lo,
                                  jnp.where(lane == 2 * nbk, s_hi.astype(F32),
                                            jnp.where(lane == 2 * nbk + 1, s_lo.astype(F32), 0.0))))
        halves.append(jnp.concatenate([(qh * HEAD_DIM ** -0.5).astype(BF16), aug.astype(BF16)], axis=1))
    qa = jnp.concatenate(halves, axis=0)

    klane = lax.broadcasted_iota(jnp.int32, (MOBA_BLOCK, LANES), 1)
    krow = lax.broadcasted_iota(jnp.int32, (MOBA_BLOCK, LANES), 0).astype(F32)

    def tile(j, carry, causal):
        off = pl.multiple_of(j * MOBA_BLOCK, MOBA_BLOCK)
        kt = kb_ref[0, pl.ds(off, MOBA_BLOCK), :]
        vt = vb_ref[0, pl.ds(off, MOBA_BLOCK), :]
        kaug = jnp.where((klane == j) | (klane == nbk + j), 1.0,
                         jnp.where((klane == 2 * nbk) | (klane == 2 * nbk + 1), krow, 0.0)).astype(BF16)
        s = _dot_nt(qa, jnp.concatenate([kt, kaug], axis=1))
        if causal:
            r = lax.broadcasted_iota(jnp.int32, s.shape, 0)
            c = lax.broadcasted_iota(jnp.int32, s.shape, 1)
            s = jnp.where(c <= jnp.where(r >= tq, r - tq, r), s, NEG_BIG)
        return _softmax_step(carry, s, vt)

    init = (jnp.full((2 * tq, 1), -jnp.inf, F32), jnp.zeros((2 * tq, 1), F32),
            jnp.zeros((2 * tq, LANES), F32))
    carry = lax.fori_loop(0, i, lambda j, c: tile(j, c, False), init)
    _, l, acc = tile(i, carry, True)
    out = acc / l
    o_ref[0] = jnp.where(lane < HEAD_DIM, out[:tq], out[tq:])


def moba_prompt(q, kb, vb, kmean, slopes):
    n, t, w = q.shape
    nbk = t // MOBA_BLOCK
    assert t % MOBA_BLOCK == 0 and nbk >= MOBA_TOPK and 2 * nbk + 2 <= LANES
    km = jnp.pad(kmean, ((0, 0), (0, LANES - nbk), (0, 0)))
    grid_spec = pltpu.PrefetchScalarGridSpec(
        num_scalar_prefetch=0,
        grid=(n, w // LANES, nbk),
        in_specs=[pl.BlockSpec(memory_space=pltpu.SMEM),
                  pl.BlockSpec((1, MOBA_BLOCK, LANES), lambda b, p, i: (b, i, p)),
                  pl.BlockSpec((1, LANES, LANES), lambda b, p, i: (b, 0, p)),
                  pl.BlockSpec((1, t, LANES), lambda b, p, i: (b, 0, p)),
                  pl.BlockSpec((1, t, LANES), lambda b, p, i: (b, 0, p))],
        out_specs=pl.BlockSpec((1, MOBA_BLOCK, LANES), lambda b, p, i: (b, i, p)),
    )
    return pl.pallas_call(
        functools.partial(_moba_prompt_body, nbk),
        grid_spec=grid_spec,
        out_shape=jax.ShapeDtypeStruct((n, t, w), F32),
        compiler_params=_cparams(("parallel", "parallel", "arbitrary")),
        name="moba_prompt",
    )(jnp.asarray(slopes), q, km, kb, vb)


_ROUTE_E, _ROUTE_W, _ROUTE_RANK = 0, 2, 4


def _router_body(x_ref, g_ref, wr_ref, br_ref, xn_ref, route_ref, cnt_ref, carry_s):
    tm = x_ref.shape[0]

    @pl.when(pl.program_id(0) == 0)
    def _():
        carry_s[...] = jnp.zeros_like(carry_s)

    x = x_ref[...]
    xn = x * lax.rsqrt(jnp.mean(x * x, axis=-1, keepdims=True) + RMS_EPS) * g_ref[...]
    xn_ref[...] = xn
    logits = jnp.dot(xn, wr_ref[...], precision=HIGHEST, preferred_element_type=F32) + br_ref[...]
    lane = lax.broadcasted_iota(jnp.int32, (tm, LANES), 1)
    is_g = lane < N_GROUPS
    gl = jnp.where(is_g, logits, -jnp.inf)
    gmax = jnp.max(gl, axis=-1, keepdims=True)
    g_idx = jnp.min(jnp.where(gl == gmax, lane, LANES), axis=-1, keepdims=True)
    g_w = 1.0 / jnp.sum(jnp.where(is_g, jnp.exp(gl - gmax), 0.0), axis=-1, keepdims=True)
    e_lo = N_GROUPS + g_idx * EXPERTS_PER_GROUP
    el = jnp.where((lane >= e_lo) & (lane < e_lo + EXPERTS_PER_GROUP), logits, -jnp.inf)
    v1 = jnp.max(el, axis=-1, keepdims=True)
    i1 = jnp.min(jnp.where(el == v1, lane, LANES), axis=-1, keepdims=True)
    el2 = jnp.where(lane == i1, -jnp.inf, el)
    v2 = jnp.max(el2, axis=-1, keepdims=True)
    i2 = jnp.min(jnp.where(el2 == v2, lane, LANES), axis=-1, keepdims=True)
    e21 = jnp.exp(v2 - v1)
    p1 = 1.0 / (1.0 + e21)
    w1 = g_w * p1
    w2 = g_w * (e21 * p1)
    hot1 = lane == i1
    hot2 = lane == i2
    cnt = jnp.where(hot1 | hot2, 1.0, 0.0)
    r = lax.broadcasted_iota(jnp.int32, (tm, tm), 0)
    c = lax.broadcasted_iota(jnp.int32, (tm, tm), 1)
    tri = jnp.where(c < r, 1.0, 0.0).astype(BF16)
    before = jnp.dot(tri, cnt.astype(BF16), preferred_element_type=F32) + carry_s[...]
    r1 = jnp.sum(jnp.where(hot1, before, 0.0), axis=-1, keepdims=True)
    r2 = jnp.sum(jnp.where(hot2, before, 0.0), axis=-1, keepdims=True)
    carry_s[...] = carry_s[...] + jnp.sum(cnt, axis=0, keepdims=True)
    cnt_ref[...] = carry_s[...]
    route = jnp.zeros((tm, LANES), F32)
    for pos, val in ((_ROUTE_E, (i1 - N_GROUPS).astype(F32)), (_ROUTE_E + 1, (i2 - N_GROUPS).astype(F32)),
                     (_ROUTE_W, w1), (_ROUTE_W + 1, w2), (_ROUTE_RANK, r1), (_ROUTE_RANK + 1, r2)):
        route = jnp.where(lane == pos, val, route)
    route_ref[...] = route


def moe_router(y, g, wr, br):
    m, d = y.shape
    tm = _row_tile(m, 256)
    return pl.pallas_call(
        _router_body,
        grid=(m // tm,),
        in_specs=[pl.BlockSpec((tm, d), lambda i: (i, 0)), pl.BlockSpec((1, d), lambda i: (0, 0)),
                  pl.BlockSpec((d, LANES), lambda i: (0, 0)), pl.BlockSpec((1, LANES), lambda i: (0, 0))],
        out_specs=[pl.BlockSpec((tm, d), lambda i: (i, 0)), pl.BlockSpec((tm, LANES), lambda i: (i, 0)),
                   pl.BlockSpec((1, LANES), lambda i: (0, 0))],
        out_shape=[jax.ShapeDtypeStruct((m, d), F32), jax.ShapeDtypeStruct((m, LANES), F32),
                   jax.ShapeDtypeStruct((1, LANES), F32)],
        scratch_shapes=[pltpu.VMEM((1, LANES), F32)],
        compiler_params=_cparams(("arbitrary",)),
        name="moe_router",
    )(y, g.reshape(1, d), wr, br)


def _row_copy(src, s_row, dst, d_row, sem):
    return pltpu.make_async_copy(src.at[pl.ds(s_row, 1), :], dst.at[pl.ds(d_row, 1), :], sem)


def _scatter_body(dest_ref, xn_ref, xs_in_ref, xs_ref, sem):
    del xs_in_ref
    tm = xn_ref.shape[0]
    base = pl.program_id(0) * tm

    def issue(r, _):
        for s in range(2):
            _row_copy(xn_ref, r, xs_ref, dest_ref[2 * (base + r) + s], sem).start()
        return 0

    def drain(r, _):
        for s in range(2):
            _row_copy(xn_ref, r, xs_ref, dest_ref[2 * (base + r) + s], sem).wait()
        return 0

    lax.fori_loop(0, tm, issue, 0)
    lax.fori_loop(0, tm, drain, 0)


def moe_scatter(dest, xn, rows):
    m, d = xn.shape
    tm = _row_tile(m, 256)
    grid_spec = pltpu.PrefetchScalarGridSpec(
        num_scalar_prefetch=1,
        grid=(m // tm,),
        in_specs=[pl.BlockSpec((tm, d), lambda i, dest: (i, 0)), pl.BlockSpec(memory_space=pl.ANY)],
        out_specs=pl.BlockSpec(memory_space=pl.ANY),
        scratch_shapes=[pltpu.SemaphoreType.DMA(())],
    )
    return pl.pallas_call(
        _scatter_body,
        grid_spec=grid_spec,
        out_shape=jax.ShapeDtypeStruct((rows, d), F32),
        input_output_aliases={2: 0},
        compiler_params=_cparams(("arbitrary",)),
        name="moe_scatter",
    )(dest, xn, jnp.zeros((rows, d), F32))


def _expert_body(blk_e_ref, nused_ref, xs_ref, w1_ref, w3_ref, w2_ref, ys_ref):
    del blk_e_ref
    used = pl.program_id(0) < nused_ref[0]

    @pl.when(used)
    def _():
        x = xs_ref[...].astype(BF16)
        h = jax.nn.silu(jnp.dot(x, w1_ref[0], preferred_element_type=F32))
        h = h * jnp.dot(x, w3_ref[0], preferred_element_type=F32)
        ys_ref[...] = jnp.dot(h.astype(BF16), w2_ref[0], preferred_element_type=F32)

    @pl.when(jnp.logical_not(used))
    def _():
        ys_ref[...] = jnp.zeros_like(ys_ref)


def moe_experts(blk_e, nused, xs, w1, w3, w2):
    rows, d = xs.shape
    te = MOE_BLOCK_ROWS
    de = w1.shape[2]
    grid_spec = pltpu.PrefetchScalarGridSpec(
        num_scalar_prefetch=2,
        grid=(rows // te,),
        in_specs=[pl.BlockSpec((te, d), lambda i, be, nu: (i, 0)),
                  pl.BlockSpec((1, d, de), lambda i, be, nu: (be[i], 0, 0)),
                  pl.BlockSpec((1, d, de), lambda i, be, nu: (be[i], 0, 0)),
                  pl.BlockSpec((1, de, d), lambda i, be, nu: (be[i], 0, 0))],
        out_specs=pl.BlockSpec((te, d), lambda i, be, nu: (i, 0)),
    )
    return pl.pallas_call(
        _expert_body,
        grid_spec=grid_spec,
        out_shape=jax.ShapeDtypeStruct((rows, d), F32),
        compiler_params=_cparams(("arbitrary",)),
        name="moe_experts",
    )(blk_e, nused, xs, w1, w3, w2)


def _combine_body(dest_ref, route_ref, res_ref, ys_ref, o_ref, buf, sem):
    tm = res_ref.shape[0]
    base = pl.program_id(0) * tm

    def issue(r, _):
        for s in range(2):
            _row_copy(ys_ref, dest_ref[2 * (base + r) + s], buf.at[s], r, sem).start()
        return 0

    def drain(r, _):
        for s in range(2):
            _row_copy(ys_ref, dest_ref[2 * (base + r) + s], buf.at[s], r, sem).wait()
        return 0

    lax.fori_loop(0, tm, issue, 0)
    lax.fori_loop(0, tm, drain, 0)
    route = route_ref[...]
    w1 = route[:, _ROUTE_W:_ROUTE_W + 1]
    w2 = route[:, _ROUTE_W + 1:_ROUTE_W + 2]
    o_ref[...] = res_ref[...] + (w1 * buf[0] + w2 * buf[1])


def moe_combine(dest, route, res, ys):
    m, d = res.shape
    tm = _row_tile(m, 256)
    grid_spec = pltpu.PrefetchScalarGridSpec(
        num_scalar_prefetch=1,
        grid=(m // tm,),
        in_specs=[pl.BlockSpec((tm, LANES), lambda i, dest: (i, 0)),
                  pl.BlockSpec((tm, d), lambda i, dest: (i, 0)),
                  pl.BlockSpec(memory_space=pl.ANY)],
        out_specs=pl.BlockSpec((tm, d), lambda i, dest: (i, 0)),
        scratch_shapes=[pltpu.VMEM((2, tm, d), F32), pltpu.SemaphoreType.DMA(())],
    )
    return pl.pallas_call(
        _combine_body,
        grid_spec=grid_spec,
        out_shape=jax.ShapeDtypeStruct((m, d), F32),
        compiler_params=_cparams(("arbitrary",)),
        name="moe_combine",
    )(dest, route, res, ys)


def hier_moe_residual(y, g, wr, br, w1, w3, w2):
    m, d = y.shape
    te = MOE_BLOCK_ROWS
    xn, route, cnt = moe_router(y, g, wr, br)
    counts = cnt[0, N_GROUPS:N_GROUPS + N_EXPERTS].astype(jnp.int32)
    padded = (counts + te - 1) // te * te
    pend = jnp.cumsum(padded)
    pstart = pend - padded
    e = route[:, _ROUTE_E:_ROUTE_E + 2].astype(jnp.int32)
    rank = route[:, _ROUTE_RANK:_ROUTE_RANK + 2].astype(jnp.int32)
    dest = (pstart[e] + rank).reshape(-1)
    n_blk = -(-(2 * m + N_EXPERTS * (te - 1)) // te)
    blk_e = jnp.minimum(jnp.searchsorted(pend, jnp.arange(n_blk, dtype=jnp.int32) * te, side="right"),
                        N_EXPERTS - 1).astype(jnp.int32)
    nused = (pend[-1:] // te).astype(jnp.int32)
    xs = moe_scatter(dest, xn, n_blk * te)
    ys = moe_experts(blk_e, nused, xs, w1, w3, w2)
    return moe_combine(dest, route, y, ys)


def _router_weights(w_rg, b_rg, w_re, b_re):
    d = w_rg.shape[0]
    pad = LANES - N_GROUPS - N_EXPERTS
    wr = jnp.concatenate([w_rg, w_re, jnp.zeros((d, pad), F32)], axis=1)
    br = jnp.concatenate([b_rg, b_re, jnp.zeros((pad,), F32)]).reshape(1, LANES)
    return wr, br


_POS_LANE = HEAD_DIM


def _compress_tail(ab, bias, w2_ref, add_pos):
    c = ab.shape[0] - SUBLANES
    gw = ab.shape[1] // 2
    h = ab[0:c, 0:gw] + ab[1:c + 1, gw:2 * gw] + bias
    out = jnp.dot(jax.nn.gelu(h).astype(BF16), w2_ref[...], preferred_element_type=F32)
    if add_pos:
        lane = lax.broadcasted_iota(jnp.int32, out.shape, 1) % LANES
        cend = lax.broadcasted_iota(jnp.int32, out.shape, 0) * CMP_STRIDE + (CMP_LEN - 1)
        hi = ((cend // 256) * 256).astype(F32)
        lo = (cend % 256).astype(F32)
        out = jnp.where((lane == _POS_LANE) | (lane == _POS_LANE + 1), hi,
                        jnp.where((lane == _POS_LANE + 2) | (lane == _POS_LANE + 3), lo, out))
    return out


def _compress_body(add_pos, n_k, *refs):
    k_refs = refs[:n_k]
    wab_ref, pe_ref, w2_ref, o_ref, ab_s = refs[n_k:]
    c = k_refs[0].shape[1] // CMP_STRIDE
    gw = n_k * LANES
    acc = jnp.zeros((c + SUBLANES, 2 * gw), F32)
    for j in range(CMP_STRIDE):
        xk = jnp.concatenate([r[0, pl.ds(j, c, stride=CMP_STRIDE), :] for r in k_refs], axis=1)
        xj = jnp.concatenate([xk, pe_ref[j:j + 1, :], pe_ref[CMP_STRIDE + j:CMP_STRIDE + j + 1, :],
                              jnp.zeros((SUBLANES - 2, gw), F32)], axis=0).astype(BF16)
        acc = acc + jnp.dot(xj, wab_ref[j], preferred_element_type=F32)
    bias = acc[c:c + 1, 0:gw] + acc[c + 1:c + 2, gw:2 * gw]
    ab_s[0:c, :] = acc[0:c]
    ab_s[c:c + SUBLANES, :] = jnp.zeros((SUBLANES, 2 * gw), F32)
    o_ref[0] = _compress_tail(ab_s, bias, w2_ref, add_pos).astype(o_ref.dtype)


def _compress_weights(w1, w2, pe, groups):
    d = w2.shape[0]
    w1j = w1.reshape(CMP_LEN, d, d)
    eye = jnp.eye(groups, dtype=F32)
    bd = (eye[None, :, None, :, None] * w1j[:, None, :, None, :]).reshape(CMP_LEN, groups * d, groups * d)
    wab = jnp.concatenate([bd[:CMP_STRIDE], bd[CMP_STRIDE:]], axis=2).astype(BF16)
    pe_t = jnp.tile(pe, (1, groups))
    w2p = jnp.concatenate([w2, jnp.zeros((d, LANES - d), F32)], axis=1)
    w2bd = (eye[:, None, :, None] * w2p[None, :, None, :]).reshape(groups * d, groups * LANES).astype(BF16)
    return wab, pe_t, w2bd


def nsa_compress(k, wab, pe_t, w2bd, add_pos):
    n, t, gw = k.shape
    c = t // CMP_STRIDE
    n_k = gw // LANES
    full = lambda shp: pl.BlockSpec(shp, lambda b: (0,) * len(shp))
    return pl.pallas_call(
        functools.partial(_compress_body, add_pos, n_k),
        grid=(n,),
        in_specs=[pl.BlockSpec((1, t, LANES), functools.partial(lambda u, b: (b, 0, u), u)) for u in range(n_k)]
        + [full(wab.shape), full(pe_t.shape), full(w2bd.shape)],
        out_specs=pl.BlockSpec((1, c, w2bd.shape[1]), lambda b: (b, 0, 0)),
        out_shape=jax.ShapeDtypeStruct((n, c, w2bd.shape[1]), BF16),
        scratch_shapes=[pltpu.VMEM((c + SUBLANES, 2 * gw), F32)],
        compiler_params=_cparams(("parallel",)),
        name="nsa_compress",
    )(*([k] * n_k), wab, pe_t, w2bd)


def _stack_queries(q, slopes, lane):
    rows = []
    for r, slope in enumerate(slopes):
        q2 = q[:, (r // 2) * LANES:(r // 2 + 1) * LANES]
        if r % 2:
            q2 = pltpu.roll(q2, HEAD_DIM, 1)
        s_hi, s_lo = _split_hi_lo(jnp.full(q2.shape, slope, F32))
        odd = ((lane - _POS_LANE) % 2) == 1
        feat = jnp.where(odd, s_lo.astype(F32), s_hi.astype(F32))
        rows.append(jnp.where(lane < HEAD_DIM, q2 * HEAD_DIM ** -0.5,
                              jnp.where(lane < _POS_LANE + 4, feat, 0.0)).astype(BF16))
    return jnp.concatenate(rows, axis=0)


def _select_blocks(imp, qb, lane):
    vis = lane <= qb
    forced = (lane == 0) | (lane == qb) | (lane == qb - 1)
    score = jnp.where(vis, jnp.where(forced, jnp.inf, imp), -jnp.inf)
    sel = jnp.zeros(imp.shape, jnp.bool_)
    for _ in range(SEL_TOPK):
        pick = lane == _first_argmax(score, lane, score.shape[-1])
        sel = sel | pick
        score = jnp.where(pick, -jnp.inf, score)
    return sel & vis


def _nsa_prompt_body(slopes_ref, q_ref, gt_ref, ck_ref, cv_ref, cover_ref, ks_ref, vs_ref, kw_ref, vw_ref,
                     o_ref):
    g = pl.program_id(1)
    i = pl.program_id(2)
    tq = q_ref.shape[1]
    r_heads = q_ref.shape[2] // HEAD_DIM
    rows = r_heads * tq
    t0 = i * tq
    lane = lax.broadcasted_iota(jnp.int32, (tq, LANES), 1)
    slopes = [slopes_ref[r_heads * g + r] for r in range(r_heads)]
    qa = _stack_queries(q_ref[0], slopes, lane)
    slope_col = jnp.concatenate([jnp.full((tq, 1), s, F32) for s in slopes], axis=0)
    row_t = lax.broadcasted_iota(jnp.int32, (rows, 1), 0) % tq

    ck = ck_ref[0]
    s_c = _dot_nt(qa, ck)
    cend = lax.broadcasted_iota(jnp.int32, s_c.shape, 1) * CMP_STRIDE + (CMP_LEN - 1)
    ok_c = cend <= t0 + row_t
    s_c = jnp.where(ok_c, s_c, NEG_BIG)
    e_c = jnp.where(ok_c, jnp.exp(s_c - jnp.max(s_c, axis=-1, keepdims=True)), 0.0)
    p_c = e_c / jnp.maximum(jnp.sum(e_c, axis=-1, keepdims=True), 1e-30)
    o_c = jnp.dot(p_c.astype(BF16), cv_ref[0], preferred_element_type=F32)

    psum = p_c[0:tq]
    for r in range(1, r_heads):
        psum = psum + p_c[r * tq:(r + 1) * tq]
    imp = jnp.dot(psum, cover_ref[...], precision=HIGHEST, preferred_element_type=F32)
    qb = (t0 + lax.broadcasted_iota(jnp.int32, (tq, LANES), 0)) // SEL_BLOCK
    pen = jnp.where(_select_blocks(imp, qb, lane), 0.0, NEG_BIG).astype(BF16)
    qs = jnp.concatenate([qa, jnp.concatenate([pen] * r_heads, axis=0)], axis=1)

    tk = tq
    klane = lax.broadcasted_iota(jnp.int32, (tk, LANES), 1)
    krow = lax.broadcasted_iota(jnp.int32, (tk, LANES), 0)
    krow_f = krow.astype(F32)
    is_pos = (klane == _POS_LANE) | (klane == _POS_LANE + 1)
    blocks_per_tile = tk // SEL_BLOCK
    col = lax.broadcasted_iota(jnp.int32, (rows, tk), 1)

    def keys_with_pos(k_ref, j):
        off = pl.multiple_of(j * tk, tk)
        return jnp.where(is_pos, krow_f, k_ref[0, pl.ds(off, tk), :].astype(F32)).astype(BF16)

    def values(v_ref, j):
        return v_ref[0, pl.ds(pl.multiple_of(j * tk, tk), tk), :]

    def tile_offset(j):
        return slope_col * ((j - i) * tk).astype(F32)

    def sel_tile(j, carry, causal):
        onehot = jnp.where(klane == j * blocks_per_tile + krow // SEL_BLOCK, 1.0, 0.0).astype(BF16)
        s = _dot_nt(qs, jnp.concatenate([keys_with_pos(ks_ref, j), onehot], axis=1))
        if causal:
            s = jnp.where(col <= row_t, s, NEG_BIG)
        return _softmax_step(carry, s, values(vs_ref, j), tile_offset(j))

    def init():
        return (jnp.full((rows, 1), -jnp.inf, F32), jnp.zeros((rows, 1), F32), jnp.zeros((rows, LANES), F32))

    carry = lax.fori_loop(0, i, lambda j, c: sel_tile(j, c, False), init())
    _, l_s, acc_s = sel_tile(i, carry, True)
    o_s = acc_s / l_s

    carry = init()
    n_win = WINDOW // tk + 1
    for w in range(n_win):
        j_true = i - (n_win - 1) + w
        j = jnp.maximum(j_true, 0)
        s = _dot_nt(qa, keys_with_pos(kw_ref, j))
        dist = (t0 - j * tk) + row_t - col
        ok = (dist >= 0) & (dist < WINDOW) & (j_true >= 0)
        carry = _softmax_step(carry, jnp.where(ok, s, NEG_BIG), values(vw_ref, j), tile_offset(j))
    _, l_w, acc_w = carry
    o_w = acc_w / l_w

    gates = jax.nn.sigmoid(gt_ref[0])
    outs = []
    for r in range(r_heads):
        sl = slice(r * tq, (r + 1) * tq)
        gr = [gates[:, 3 * r + b:3 * r + b + 1] for b in range(3)]
        outs.append(gr[0] * o_c[sl] + gr[1] * o_s[sl] + gr[2] * o_w[sl])
    pairs = [jnp.where(lane < HEAD_DIM, outs[2 * u], pltpu.roll(outs[2 * u + 1], HEAD_DIM, 1))
             for u in range(r_heads // 2)]
    o_ref[0] = jnp.concatenate(pairs, axis=1)


def nsa_prompt(q, gt, ck, cv, cover, ks, vs, kw, vw, slopes):
    n, t, w = q.shape
    groups = ks.shape[2] // LANES
    qw = w // groups
    tq = 256
    c = ck.shape[1]
    assert t % tq == 0 and -(-t // SEL_BLOCK) <= LANES and t // SEL_BLOCK >= SEL_TOPK
    kv_spec = lambda rows: pl.BlockSpec((1, rows, LANES), lambda b, g, i: (b, 0, g))
    grid_spec = pltpu.PrefetchScalarGridSpec(
        num_scalar_prefetch=0,
        grid=(n, groups, t // tq),
        in_specs=[pl.BlockSpec(memory_space=pltpu.SMEM),
                  pl.BlockSpec((1, tq, qw), lambda b, g, i: (b, i, g)),
                  pl.BlockSpec((1, tq, LANES), lambda b, g, i: (b, i, g)),
                  kv_spec(c), kv_spec(c),
                  pl.BlockSpec(cover.shape, lambda b, g, i: (0, 0)),
                  kv_spec(t), kv_spec(t), kv_spec(t), kv_spec(t)],
        out_specs=pl.BlockSpec((1, tq, qw), lambda b, g, i: (b, i, g)),
    )
    return pl.pallas_call(
        _nsa_prompt_body,
        grid_spec=grid_spec,
        out_shape=jax.ShapeDtypeStruct((n, t, w), F32),
        compiler_params=_cparams(("parallel", "parallel", "arbitrary")),
        name="nsa_prompt",
    )(jnp.asarray(slopes), q, gt, ck, cv, cover, ks, vs, kw, vw)


def _nsa_cover(n_cmp_rows, n_sel):
    i = np.arange(n_cmp_rows)[:, None]
    j = np.arange(-(-n_sel // LANES) * LANES)[None, :]
    cov = (CMP_STRIDE * i < SEL_BLOCK * (j + 1)) & (CMP_STRIDE * i + CMP_LEN > SEL_BLOCK * j) & (j < n_sel)
    return jnp.asarray(cov.astype(np.float32))


def _pad_group_cols(w, width):
    d, gw = w.shape
    g = gw // width
    return jnp.pad(w.reshape(d, g, width), ((0, 0), (0, 0), (0, LANES - width))).reshape(d, g * LANES)


def _moba_decode_body(nblk, past, pt_ref, q_ref, slope_ref, kn_ref, vn_ref, k0_ref, k1_ref, v0_ref, v1_ref,
                      o_ref, m_s, l_s, acc_s, km_s):
    del pt_ref
    j = pl.program_id(1)
    nq = q_ref.shape[1] // (k0_ref.shape[3])
    heads = k0_ref.shape[3]
    page_rows = PAGE_SIZE * heads
    rows = q_ref.shape[1]
    q = q_ref[0]
    qb = (q * HEAD_DIM ** -0.5).astype(BF16)
    slope = slope_ref[...]
    lane = lax.broadcasted_iota(jnp.int32, (rows, LANES), 1)

    @pl.when(j == 0)
    def _():
        m_s[...] = jnp.full(m_s.shape, NEG_BIG, F32)
        l_s[...] = jnp.zeros(l_s.shape, F32)

    r_i = lax.broadcasted_iota(jnp.int32, (rows, page_rows), 0)
    c_i = lax.broadcasted_iota(jnp.int32, (rows, page_rows), 1)
    cmat = jnp.where((c_i % heads) == (r_i // nq), slope * (c_i // heads).astype(F32), NEG_BIG)
    logits, offs = [], []
    for pg, k_ref in enumerate((k0_ref, k1_ref)):
        k2 = k_ref[0, 0].reshape(page_rows, HEAD_DIM).astype(BF16)
        logits.append(_dot_nt(qb, k2) + cmat)
        offs.append(slope * (j * MOBA_BLOCK + pg * PAGE_SIZE - past).astype(F32))
    m_j = jnp.maximum(jnp.max(logits[0], axis=-1, keepdims=True) + offs[0],
                      jnp.max(logits[1], axis=-1, keepdims=True) + offs[1])
    l_j = jnp.zeros((rows, 1), F32)
    acc_j = jnp.zeros((rows, HEAD_DIM), F32)
    for s, off, v_ref in zip(logits, offs, (v0_ref, v1_ref)):
        p = jnp.exp(s - (m_j - off))
        l_j = l_j + jnp.sum(p, axis=-1, keepdims=True)
        v2 = v_ref[0, 0].reshape(page_rows, HEAD_DIM).astype(BF16)
        acc_j = acc_j + jnp.dot(p.astype(BF16), v2, preferred_element_type=F32)
    m_s[...] = jnp.where(lane == j, m_j, m_s[...])
    l_s[...] = jnp.where(lane == j, l_j, l_s[...])
    acc_s[j] = acc_j
    km_s[j] = (jnp.sum(k0_ref[0, 0], axis=0) + jnp.sum(k1_ref[0, 0], axis=0)) * (1.0 / MOBA_BLOCK)

    @pl.when(j == nblk - 1)
    def _():
        km = km_s[...].reshape(nblk * heads, HEAD_DIM)
        g_all = _dot_nt(q, km, HIGHEST)
        gr = lax.broadcasted_iota(jnp.int32, g_all.shape, 0)
        gc = lax.broadcasted_iota(jnp.int32, g_all.shape, 1)
        g_own = jnp.where((gc % heads) == (gr // nq), g_all, 0.0)
        fold = jnp.where(lax.broadcasted_iota(jnp.int32, (nblk * heads, LANES), 0) // heads
                         == lax.broadcasted_iota(jnp.int32, (nblk * heads, LANES), 1), 1.0, 0.0)
        gate = jnp.dot(g_own, fold, precision=HIGHEST, preferred_element_type=F32)
        g = jnp.where(lane < nblk, gate, -jnp.inf)
        sel = jnp.zeros(g.shape, jnp.bool_)
        for _ in range(MOBA_TOPK):
            pick = lane == _first_argmax(g, lane, LANES)
            sel = sel | pick
            g = jnp.where(pick, -jnp.inf, g)
        sel = sel & (lane < nblk)
        n_new = kn_ref.shape[1]
        orow = lax.broadcasted_iota(jnp.int32, (rows, n_new), 0)
        ocol = lax.broadcasted_iota(jnp.int32, (rows, n_new), 1)
        ok_o = ((ocol % heads) == (orow // nq)) & ((ocol // heads) <= (orow % nq))
        s_o = jnp.where(ok_o, _dot_nt(qb, kn_ref[0].astype(BF16)) + slope * (ocol // heads).astype(F32), NEG_BIG)
        m_o = jnp.max(s_o, axis=-1, keepdims=True)
        p_o = jnp.exp(s_o - m_o)
        l_o = jnp.sum(p_o, axis=-1, keepdims=True)
        acc_o = jnp.dot(p_o.astype(BF16), vn_ref[0].astype(BF16), preferred_element_type=F32)
        m_all = jnp.maximum(jnp.max(jnp.where(sel, m_s[...], -jnp.inf), axis=-1, keepdims=True), m_o)
        w = jnp.where(sel, jnp.exp(m_s[...] - m_all), 0.0)
        w_o = jnp.exp(m_o - m_all)
        den = jnp.sum(w * l_s[...], axis=-1, keepdims=True) + w_o * l_o
        num = w_o * acc_o
        for jj in range(nblk):
            num = num + w[:, jj:jj + 1] * acc_s[jj]
        o_ref[0] = num / den


def moba_decode(q_rows, k_new, v_new, cache_k, cache_v, page_table, layer, slopes):
    b, rows, _ = q_rows.shape
    heads = cache_k.shape[3]
    nq = rows // heads
    n_pages = page_table.shape[1]
    past = n_pages * PAGE_SIZE
    pages_per_blk = MOBA_BLOCK // PAGE_SIZE
    nblk = past // MOBA_BLOCK
    assert past % MOBA_BLOCK == 0 and MOBA_TOPK <= nblk <= LANES and pages_per_blk == 2
    slope_rows = jnp.asarray(np.repeat(slopes, nq).reshape(rows, 1))
    page = lambda pg: pl.BlockSpec((1, 1, PAGE_SIZE, heads, HEAD_DIM),
                                   lambda s, j, pt: (layer, pt[s, pages_per_blk * j + pg], 0, 0, 0))
    grid_spec = pltpu.PrefetchScalarGridSpec(
        num_scalar_prefetch=1,
        grid=(b, nblk),
        in_specs=[pl.BlockSpec((1, rows, HEAD_DIM), lambda s, j, pt: (s, 0, 0)),
                  pl.BlockSpec((rows, 1), lambda s, j, pt: (0, 0)),
                  pl.BlockSpec((1, nq * heads, HEAD_DIM), lambda s, j, pt: (s, 0, 0)),
                  pl.BlockSpec((1, nq * heads, HEAD_DIM), lambda s, j, pt: (s, 0, 0)),
                  page(0), page(1), page(0), page(1)],
        out_specs=pl.BlockSpec((1, rows, HEAD_DIM), lambda s, j, pt: (s, 0, 0)),
        scratch_shapes=[pltpu.VMEM((rows, LANES), F32), pltpu.VMEM((rows, LANES), F32),
                        pltpu.VMEM((nblk, rows, HEAD_DIM), F32), pltpu.VMEM((nblk, heads, HEAD_DIM), F32)],
    )
    return pl.pallas_call(
        functools.partial(_moba_decode_body, nblk, past),
        grid_spec=grid_spec,
        out_shape=jax.ShapeDtypeStruct((b, rows, HEAD_DIM), F32),
        compiler_params=_cparams(("parallel", "arbitrary")),
        name="moba_decode",
    )(page_table, q_rows, slope_rows, k_new, v_new, cache_k, cache_k, cache_v, cache_v)


_CHUNKS_PER_PAGE = PAGE_SIZE // CMP_STRIDE


def _page_ab_body(k_ref, wj_ref, o_ref):
    pages, groups = k_ref.shape[1], k_ref.shape[3]
    for g in range(groups):
        acc = jnp.zeros((pages * _CHUNKS_PER_PAGE, LANES), F32)
        for j in range(CMP_STRIDE):
            x = k_ref[0, :, pl.ds(j, _CHUNKS_PER_PAGE, stride=CMP_STRIDE), g, :]
            x = x.reshape(pages * _CHUNKS_PER_PAGE, HEAD_DIM).astype(BF16)
            acc = acc + jnp.dot(x, wj_ref[j], preferred_element_type=F32)
        o_ref[:, g * LANES:(g + 1) * LANES] = acc


def nsa_page_ab(cache, layer, wj):
    _, n_phys, _, groups, _ = cache.shape
    pages = next(p for p in (32, 16, 8, 4, 2, 1) if n_phys % p == 0)
    return pl.pallas_call(
        _page_ab_body,
        grid=(n_phys // pages,),
        in_specs=[pl.BlockSpec((1, pages, PAGE_SIZE, groups, HEAD_DIM), lambda i: (layer, i, 0, 0, 0)),
                  pl.BlockSpec(wj.shape, lambda i: (0, 0, 0))],
        out_specs=pl.BlockSpec((pages * _CHUNKS_PER_PAGE, groups * LANES), lambda i: (i, 0)),
        out_shape=jax.ShapeDtypeStruct((n_phys * _CHUNKS_PER_PAGE, groups * LANES), F32),
        compiler_params=_cparams(("parallel",)),
        name="nsa_page_ab",
    )(cache, wj)


def _page_ab_copy(ab_hbm, pt_ref, s, p, buf, sem):
    return pltpu.make_async_copy(ab_hbm.at[pt_ref[s, p]], buf.at[pl.ds(p * _CHUNKS_PER_PAGE, _CHUNKS_PER_PAGE), :], sem)


def _decode_queries(q_ref, slope_ref):
    q = q_ref[0]
    lane = lax.broadcasted_iota(jnp.int32, q.shape, 1)
    s_hi, s_lo = _split_hi_lo(jnp.broadcast_to(slope_ref[...], q.shape))
    feat = jnp.where(((lane - _POS_LANE) % 2) == 1, s_lo.astype(F32), s_hi.astype(F32))
    return jnp.where(lane < HEAD_DIM, q * HEAD_DIM ** -0.5,
                     jnp.where(lane < _POS_LANE + 4, feat, 0.0)).astype(BF16)


def _nsa_decode_cmp_body(n_pages, past, pt_ref, q_ref, slope_ref, abk_ref, abv_ref, wjk_ref, wjv_ref, pek_ref,
                         pev_ref, w2k_ref, w2v_ref, cover_ref, oc_ref, pen_ref, bufk, bufv, sem):
    s = pl.program_id(0)
    c = n_pages * _CHUNKS_PER_PAGE
    rows = q_ref.shape[1]
    groups = abk_ref.shape[2] // LANES
    rpg = rows // groups
    nq = pen_ref.shape[2] // 2

    def issue(p, _):
        _page_ab_copy(abk_ref, pt_ref, s, p, bufk, sem.at[0]).start()
        _page_ab_copy(abv_ref, pt_ref, s, p, bufv, sem.at[1]).start()
        return 0

    def drain(p, _):
        _page_ab_copy(abk_ref, pt_ref, s, p, bufk, sem.at[0]).wait()
        _page_ab_copy(abv_ref, pt_ref, s, p, bufv, sem.at[1]).wait()
        return 0

    lax.fori_loop(0, n_pages, issue, 0)
    pad = jnp.zeros((SUBLANES, bufk.shape[1]), F32)
    bufk[c:c + SUBLANES, :] = pad
    bufv[c:c + SUBLANES, :] = pad
    lax.fori_loop(0, n_pages, drain, 0)

    def tokens(buf, wj_ref, pe_ref, w2_ref, g, add_pos):
        acc = jnp.zeros((SUBLANES, LANES), F32)
        for j in range(CMP_STRIDE):
            xj = jnp.concatenate([pe_ref[j:j + 1, :], pe_ref[CMP_STRIDE + j:CMP_STRIDE + j + 1, :],
                                  jnp.zeros((SUBLANES - 2, HEAD_DIM), F32)], axis=0).astype(BF16)
            acc = acc + jnp.dot(xj, wj_ref[j], preferred_element_type=F32)
        bias = acc[0:1] + pltpu.roll(acc, HEAD_DIM, 1)[1:2]
        slab = slice(g * LANES, (g + 1) * LANES)
        h = buf[0:c, slab] + pltpu.roll(buf[1:c + 1, slab], HEAD_DIM, 1) + bias
        out = jnp.dot(jax.nn.gelu(h).astype(BF16), w2_ref[...], preferred_element_type=F32)
        if add_pos:
            lane = lax.broadcasted_iota(jnp.int32, out.shape, 1)
            cend = lax.broadcasted_iota(jnp.int32, out.shape, 0) * CMP_STRIDE + (CMP_LEN - 1)
            hi = ((cend // 256) * 256).astype(F32)
            lo = (cend % 256).astype(F32)
            out = jnp.where((lane == _POS_LANE) | (lane == _POS_LANE + 1), hi,
                            jnp.where((lane == _POS_LANE + 2) | (lane == _POS_LANE + 3), lo, out))
        return out.astype(BF16)

    qa = _decode_queries(q_ref, slope_ref)
    for g in range(groups):
        ck = tokens(bufk, wjk_ref, pek_ref, w2k_ref, g, True)
        cv = tokens(bufv, wjv_ref, pev_ref, w2v_ref, g, False)
        s_c = _dot_nt(qa[g * rpg:(g + 1) * rpg], ck)
        cend = lax.broadcasted_iota(jnp.int32, s_c.shape, 1) * CMP_STRIDE + (CMP_LEN - 1)
        t_q = past + lax.broadcasted_iota(jnp.int32, s_c.shape, 0) % nq
        ok = cend <= t_q
        s_c = jnp.where(ok, s_c, NEG_BIG)
        e_c = jnp.where(ok, jnp.exp(s_c - jnp.max(s_c, axis=-1, keepdims=True)), 0.0)
        p_c = e_c / jnp.maximum(jnp.sum(e_c, axis=-1, keepdims=True), 1e-30)
        oc_ref[0, g * rpg:(g + 1) * rpg, :] = jnp.dot(p_c.astype(BF16), cv, preferred_element_type=F32)
        half = p_c[0:2 * nq]
        for u in range(1, rpg // (2 * nq)):
            half = half + p_c[2 * nq * u:2 * nq * (u + 1)]
        psum = half + pltpu.roll(half, nq, 0)
        imp = jnp.dot(psum, cover_ref[...], precision=HIGHEST, preferred_element_type=F32)
        lane = lax.broadcasted_iota(jnp.int32, imp.shape, 1)
        qb = (past + lax.broadcasted_iota(jnp.int32, imp.shape, 0) % nq) // SEL_BLOCK
        pen_ref[0, g] = jnp.where(_select_blocks(imp, qb, lane), 0.0, NEG_BIG)


def nsa_decode_cmp(q_rows, slope_rows, ab_k, ab_v, page_table, wjk, wjv, pek, pev, w2k, w2v, cover, nq):
    b, rows, _ = q_rows.shape
    n_pages = page_table.shape[1]
    past = n_pages * PAGE_SIZE
    groups = ab_k.shape[1] // LANES
    c = n_pages * _CHUNKS_PER_PAGE
    assert past % CMP_STRIDE == 0 and nq < CMP_STRIDE and 2 * nq == SUBLANES
    ab_k3 = ab_k.reshape(-1, _CHUNKS_PER_PAGE, groups * LANES)
    ab_v3 = ab_v.reshape(-1, _CHUNKS_PER_PAGE, groups * LANES)
    full = lambda a: pl.BlockSpec(a.shape, lambda s, pt: (0,) * a.ndim)
    grid_spec = pltpu.PrefetchScalarGridSpec(
        num_scalar_prefetch=1,
        grid=(b,),
        in_specs=[pl.BlockSpec((1, rows, LANES), lambda s, pt: (s, 0, 0)), full(slope_rows),
                  pl.BlockSpec(memory_space=pl.ANY), pl.BlockSpec(memory_space=pl.ANY),
                  full(wjk), full(wjv), full(pek), full(pev), full(w2k), full(w2v), full(cover)],
        out_specs=[pl.BlockSpec((1, rows, LANES), lambda s, pt: (s, 0, 0)),
                   pl.BlockSpec((1, groups, 2 * nq, cover.shape[1]), lambda s, pt: (s, 0, 0, 0))],
        scratch_shapes=[pltpu.VMEM((c + SUBLANES, groups * LANES), F32),
                        pltpu.VMEM((c + SUBLANES, groups * LANES), F32),
                        pltpu.SemaphoreType.DMA((2,))],
    )
    return pl.pallas_call(
        functools.partial(_nsa_decode_cmp_body, n_pages, past),
        grid_spec=grid_spec,
        out_shape=[jax.ShapeDtypeStruct((b, rows, LANES), F32),
                   jax.ShapeDtypeStruct((b, groups, 2 * nq, cover.shape[1]), F32)],
        compiler_params=_cparams(("arbitrary",)),
        name="nsa_decode_cmp",
    )(page_table, q_rows, slope_rows, ab_k3, ab_v3, wjk, wjv, pek, pev, w2k, w2v, cover)


_SEL_PAGES_PER_STEP = 4


def _nsa_decode_sel_body(n_steps, past, wbuf, pt_ref, q_ref, slope_ref, pen_ref, oc_ref, gt_ref, ksn_ref, vsn_ref,
                         kwn_ref, vwn_ref, wk_ref, wv_ref, *refs):
    del pt_ref
    pp = _SEL_PAGES_PER_STEP
    k_refs, v_refs = refs[:pp], refs[pp:2 * pp]
    o_ref, m_s, l_s, acc_s = refs[2 * pp:]
    step = pl.program_id(1)
    rows = q_ref.shape[1]
    groups = pen_ref.shape[1]
    rpg = rows // groups
    nq = pen_ref.shape[2] // 2
    n_sel = pen_ref.shape[3]
    qa = _decode_queries(q_ref, slope_ref)[:, 0:HEAD_DIM]
    slope = slope_ref[...]
    blocks_per_page = PAGE_SIZE // SEL_BLOCK

    @pl.when(step == 0)
    def _():
        m_s[...] = jnp.full(m_s.shape, -jnp.inf, F32)
        l_s[...] = jnp.zeros(l_s.shape, F32)
        acc_s[...] = jnp.zeros(acc_s.shape, F32)

    e_row = lax.broadcasted_iota(jnp.int32, (n_sel, PAGE_SIZE), 0)
    e_col = lax.broadcasted_iota(jnp.int32, (n_sel, PAGE_SIZE), 1)
    col_f = lax.broadcasted_iota(jnp.int32, (rpg, PAGE_SIZE), 1).astype(F32)
    for g in range(groups):
        sl = slice(g * rpg, (g + 1) * rpg)
        pen_g = pen_ref[0, g]
        pen_rows = jnp.concatenate([pen_g] * (rpg // (2 * nq)), axis=0).astype(BF16)
        carry = (m_s[sl], l_s[sl], acc_s[sl])
        for u in range(pp):
            page = step * pp + u
            expand = jnp.where(e_row == page * blocks_per_page + e_col // SEL_BLOCK, 1.0, 0.0).astype(BF16)
            kg = k_refs[u][0, 0, :, g, :].astype(BF16)
            vg = v_refs[u][0, 0, :, g, :].astype(BF16)
            s = (_dot_nt(qa[sl], kg) + jnp.dot(pen_rows, expand, preferred_element_type=F32)
                 + slope[sl] * (col_f + (page * PAGE_SIZE - past).astype(F32)))
            carry = _softmax_step(carry, s, vg)
        m_s[sl], l_s[sl], acc_s[sl] = carry

    @pl.when(step == n_steps - 1)
    def _():
        gates = jax.nn.sigmoid(gt_ref[0])
        n_new = ksn_ref.shape[2]
        wr = lax.broadcasted_iota(jnp.int32, (rpg, wbuf), 0) % nq
        wc = lax.broadcasted_iota(jnp.int32, (rpg, wbuf), 1)
        dist_w = wr + wbuf - wc
        ok_w = (dist_w >= 0) & (dist_w < WINDOW) & (past - wbuf + wc >= 0)
        nr = lax.broadcasted_iota(jnp.int32, (rpg, n_new), 0) % nq
        nc = lax.broadcasted_iota(jnp.int32, (rpg, n_new), 1)
        ok_n = (nc <= nr) & (nc < nq)
        bias_n = nc.astype(F32)
        for g in range(groups):
            sl = slice(g * rpg, (g + 1) * rpg)
            pen_g = pen_ref[0, g]
            pen_new = jnp.concatenate([pen_g[:, past // SEL_BLOCK:past // SEL_BLOCK + 1]] * (rpg // (2 * nq)), axis=0)
            s_n = jnp.where(ok_n, _dot_nt(qa[sl], ksn_ref[0, g].astype(BF16)) + slope[sl] * bias_n + pen_new,
                            NEG_BIG)
            _, l_sel, acc_sel = _softmax_step((m_s[sl], l_s[sl], acc_s[sl]), s_n, vsn_ref[0, g].astype(BF16))
            o_sel = acc_sel / l_sel
            init = (jnp.full((rpg, 1), -jnp.inf, F32), jnp.zeros((rpg, 1), F32), jnp.zeros((rpg, HEAD_DIM), F32))
            s_w = _dot_nt(qa[sl], wk_ref[0, 0, :, g, :].astype(BF16)) + slope[sl] * (wc - wbuf).astype(F32)
            carry = _softmax_step(init, jnp.where(ok_w, s_w, NEG_BIG), wv_ref[0, 0, :, g, :].astype(BF16))
            s_wn = jnp.where(ok_n, _dot_nt(qa[sl], kwn_ref[0, g].astype(BF16)) + slope[sl] * bias_n, NEG_BIG)
            _, l_win, acc_win = _softmax_step(carry, s_wn, vwn_ref[0, g].astype(BF16))
            o_win = acc_win / l_win
            o_ref[0, sl, :] = (gates[sl, 0:1] * oc_ref[0, sl, 0:HEAD_DIM] + gates[sl, 1:2] * o_sel
                               + gates[sl, 2:3] * o_win)


def nsa_decode_sel(q_rows, slope_rows, pen, o_c, gt_rows, ks_new, vs_new, kw_new, vw_new, win_k, win_v,
                   cache_k, cache_v, page_table, layer):
    b, rows, _ = q_rows.shape
    groups = cache_k.shape[3]
    n_pages = page_table.shape[1]
    past = n_pages * PAGE_SIZE
    pp = _SEL_PAGES_PER_STEP
    wbuf = win_k.shape[2]
    assert n_pages % pp == 0 and past % SEL_BLOCK == 0
    n_steps = n_pages // pp
    per_seq = lambda a: pl.BlockSpec((1,) + a.shape[1:], lambda s, j, pt: (s,) + (0,) * (a.ndim - 1))
    page = lambda u: pl.BlockSpec((1, 1, PAGE_SIZE, groups, HEAD_DIM),
                                  lambda s, j, pt: (layer, pt[s, pp * j + u], 0, 0, 0))
    win = pl.BlockSpec((1, 1, wbuf, groups, HEAD_DIM), lambda s, j, pt: (layer, s, 0, 0, 0))
    grid_spec = pltpu.PrefetchScalarGridSpec(
        num_scalar_prefetch=1,
        grid=(b, n_steps),
        in_specs=[per_seq(q_rows), pl.BlockSpec(slope_rows.shape, lambda s, j, pt: (0, 0)),
                  per_seq(pen), per_seq(o_c), per_seq(gt_rows),
                  per_seq(ks_new), per_seq(vs_new), per_seq(kw_new), per_seq(vw_new), win, win]
        + [page(u) for u in range(pp)] * 2,
        out_specs=pl.BlockSpec((1, rows, HEAD_DIM), lambda s, j, pt: (s, 0, 0)),
        scratch_shapes=[pltpu.VMEM((rows, 1), F32), pltpu.VMEM((rows, 1), F32),
                        pltpu.VMEM((rows, HEAD_DIM), F32)],
    )
    return pl.pallas_call(
        functools.partial(_nsa_decode_sel_body, n_steps, past, wbuf),
        grid_spec=grid_spec,
        out_shape=jax.ShapeDtypeStruct((b, rows, HEAD_DIM), F32),
        compiler_params=_cparams(("parallel", "arbitrary")),
        name="nsa_decode_sel",
    )(page_table, q_rows, slope_rows, pen, o_c, gt_rows, ks_new, vs_new, kw_new, vw_new, win_k, win_v,
      *([cache_k] * pp), *([cache_v] * pp))


def _heads_to_rows(x, n, t, heads):
    return x.reshape(n, t, heads, HEAD_DIM).transpose(0, 2, 1, 3).reshape(n, heads * t, HEAD_DIM)


def _rows_to_heads(x, n, t, heads):
    return x.reshape(n, heads, t, HEAD_DIM).transpose(0, 2, 1, 3).reshape(n * t, heads * HEAD_DIM)


def _new_kv_rows(x, n, t, groups):
    x = x.reshape(n, t, groups, HEAD_DIM).transpose(0, 2, 1, 3)
    return jnp.pad(x, ((0, 0), (0, 0), (0, SUBLANES - t), (0, 0)))


def _layer_ab(layer, l, yp, ys, dims, p):
    bp, tp, bs, ts = dims
    d_rnn = p["conv_w"].shape[2]
    heads = (p["w_in_ab"].shape[2] - 2 * d_rnn) // (3 * HEAD_DIM)
    hw = heads * HEAD_DIM
    w_in = p["w_in_ab"][l].astype(BF16)
    cuts = [0, d_rnn, 2 * d_rnn, 2 * d_rnn + hw, 2 * d_rnn + 2 * hw, 2 * d_rnn + 3 * hw]
    ws = [w_in[:, a:b] for a, b in zip(cuts[:-1], cuts[1:])]
    w_out = p["w_out_ab"][l].astype(BF16)
    w_out = [w_out[:d_rnn], w_out[d_rnn:]]
    rg = (p["conv_w"][l], p["conv_b"][l], _block_diag(p["w_rg_a"][l]).astype(BF16), p["b_rg_a"][l],
          _block_diag(p["w_rg_i"][l]).astype(BF16), p["b_rg_i"][l], p["rg_lambda"][l])
    slopes = _alibi_slopes(heads)
    g = p["norm_mix"][layer]
    f32x5 = [(i, F32) for i in range(5)]

    xr, gr, q, k, v, kb, vb = norm_proj(yp, g, ws, f32x5 + [(3, BF16), (4, BF16)])
    seq = lambda a: a.reshape(bp, tp, a.shape[-1])
    r_p, conv_p, h_p = rglru(seq(xr), seq(gr), jnp.zeros((bp, CONV_W - 1, d_rnn), F32),
                             jnp.zeros((bp, d_rnn), F32), *rg)
    att = moba_prompt(seq(q), seq(kb), seq(vb), block_mean(seq(k), MOBA_BLOCK), slopes)
    yp = proj_residual([r_p.reshape(bp * tp, d_rnn), att.reshape(bp * tp, hw)], w_out, yp)
    out_p = (k.reshape(bp, tp, heads, HEAD_DIM), v.reshape(bp, tp, heads, HEAD_DIM), conv_p, h_p)

    xr, gr, q, k, v = norm_proj(ys, g, ws, f32x5)
    seq = lambda a: a.reshape(bs, ts, a.shape[-1])
    r_s, conv_s, h_s = rglru(seq(xr), seq(gr), p["state_conv"][l], p["state_rglru"][l], *rg)
    att = moba_decode(_heads_to_rows(q, bs, ts, heads), k.reshape(bs, ts * heads, HEAD_DIM),
                      v.reshape(bs, ts * heads, HEAD_DIM), p["cache_moba_k"], p["cache_moba_v"],
                      p["page_table"], l, slopes)
    ys = proj_residual([r_s.reshape(bs * ts, d_rnn), _rows_to_heads(att, bs, ts, heads)], w_out, ys)
    out_s = (k.reshape(bs, ts, heads, HEAD_DIM), v.reshape(bs, ts, heads, HEAD_DIM), conv_s, h_s)
    return yp, ys, out_p, out_s


def _layer_c(layer, l, yp, ys, dims, p):
    bp, tp, bs, ts = dims
    groups = p["cache_cmp_k"].shape[3]
    gw = groups * HEAD_DIM
    w_in = p["w_in_c"][l]
    heads = (w_in.shape[1] - 6 * gw) // (HEAD_DIM + 3)
    qw = heads * HEAD_DIM
    cuts = [0, qw] + [qw + (i + 1) * gw for i in range(6)]
    w_f32 = [w_in[:, a:b] for a, b in zip(cuts[:-1], cuts[1:])]
    w_gt = w_in[:, cuts[-1]:]
    slopes = _alibi_slopes(heads)
    g = p["norm_mix"][layer]
    w_out = [p["w_out_c"][l].astype(BF16)]
    cmp_k = (p["w_cmp_k1"][l], p["w_cmp_k2"][l], p["pe_cmp_k"][l])
    cmp_v = (p["w_cmp_v1"][l], p["w_cmp_v2"][l], p["pe_cmp_v"][l])
    kv4 = lambda a, n, t: a.reshape(n, t, groups, HEAD_DIM)

    ws = ([w.astype(BF16) for w in w_f32] + [_pad_group_cols(w_gt, 3 * heads // groups).astype(BF16)]
          + [_pad_group_cols(w, HEAD_DIM).astype(BF16) for w in w_f32[3:7]])
    outs = [(i, F32) for i in range(8)] + [(i, BF16) for i in range(8, 12)]
    q, kc, vc, ks, vs, kw, vw, gt, ksp, vsp, kwp, vwp = norm_proj(yp, g, ws, outs)
    seq = lambda a: a.reshape(bp, tp, a.shape[-1])
    ck = nsa_compress(seq(kc), *_compress_weights(*cmp_k, groups), True)
    cv = nsa_compress(seq(vc), *_compress_weights(*cmp_v, groups), False)
    cover = _nsa_cover(tp // CMP_STRIDE, max(-(-tp // SEL_BLOCK), SEL_TOPK))
    att = nsa_prompt(seq(q), seq(gt), ck, cv, cover, seq(ksp), seq(vsp), seq(kwp), seq(vwp), slopes)
    yp = proj_residual([att.reshape(bp * tp, qw)], w_out, yp)
    w_keep = min(WINDOW, tp)
    out_p = tuple(kv4(a, bp, tp) for a in (kc, vc, ks, vs)) + tuple(kv4(a, bp, tp)[:, tp - w_keep:] for a in (kw, vw))

    ws = [w.astype(BF16) for w in w_f32] + [jnp.pad(w_gt, ((0, 0), (0, LANES - w_gt.shape[1]))).astype(BF16)]
    q, kc, vc, ks, vs, kw, vw, gt = norm_proj(ys, g, ws, [(i, F32) for i in range(8)])
    q_rows = jnp.pad(_heads_to_rows(q, bs, ts, heads), ((0, 0), (0, 0), (0, LANES - HEAD_DIM)))
    slope_rows = jnp.asarray(np.repeat(slopes, ts).reshape(heads * ts, 1))
    gt_rows = gt[:, :3 * heads].reshape(bs, ts, heads, 3).transpose(0, 2, 1, 3).reshape(bs, heads * ts, 3)
    n_pages = p["page_table"].shape[1]
    past = n_pages * PAGE_SIZE
    cover = _nsa_cover(n_pages * _CHUNKS_PER_PAGE, max(-(-(past + ts) // SEL_BLOCK), SEL_TOPK))
    dec_w = []
    for w1, w2, pe in (cmp_k, cmp_v):
        w1j = w1.reshape(CMP_LEN, HEAD_DIM, HEAD_DIM)
        dec_w.append((jnp.concatenate([w1j[:CMP_STRIDE], w1j[CMP_STRIDE:]], axis=2).astype(BF16), pe,
                      jnp.pad(w2, ((0, LANES - HEAD_DIM), (0, LANES - HEAD_DIM))).astype(BF16)))
    (wjk, pek, w2k), (wjv, pev, w2v) = dec_w
    ab_k = nsa_page_ab(p["cache_cmp_k"], l, wjk)
    ab_v = nsa_page_ab(p["cache_cmp_v"], l, wjv)
    o_c, pen = nsa_decode_cmp(q_rows, slope_rows, ab_k, ab_v, p["page_table"], wjk, wjv, pek, pev, w2k, w2v,
                              cover, ts)
    att = nsa_decode_sel(q_rows, slope_rows, pen, o_c, gt_rows, _new_kv_rows(ks, bs, ts, groups),
                         _new_kv_rows(vs, bs, ts, groups), _new_kv_rows(kw, bs, ts, groups),
                         _new_kv_rows(vw, bs, ts, groups), p["state_win_k"], p["state_win_v"],
                         p["cache_sel_k"], p["cache_sel_v"], p["page_table"], l)
    ys = proj_residual([_rows_to_heads(att, bs, ts, heads)], w_out, ys)
    w_buf = p["state_win_k"].shape[2]
    win = lambda state, new: jnp.concatenate([state[l], kv4(new, bs, ts)], axis=1)[:, -w_buf:]
    out_s = tuple(kv4(a, bs, ts) for a in (kc, vc, ks, vs)) + (win(p["state_win_k"], kw), win(p["state_win_v"], vw))
    return yp, ys, out_p, out_s


def kernel(x_prompt, x_sample, cache_moba_k, cache_moba_v, cache_cmp_k, cache_cmp_v, cache_sel_k, cache_sel_v,
           state_win_k, state_win_v, state_conv, state_rglru, page_table, norm_mix, norm_ffn, norm_final,
           w_in_ab, w_out_ab, conv_w, conv_b, w_rg_a, b_rg_a, w_rg_i, b_rg_i, rg_lambda, w_in_c, w_out_c,
           w_cmp_k1, w_cmp_k2, pe_cmp_k, w_cmp_v1, w_cmp_v2, pe_cmp_v, w_route_group, b_route_group,
           w_route_expert, b_route_expert, w_exp_gate, w_exp_up, w_exp_down):
    p = dict(cache_moba_k=cache_moba_k, cache_moba_v=cache_moba_v, cache_cmp_k=cache_cmp_k,
             cache_cmp_v=cache_cmp_v, cache_sel_k=cache_sel_k, cache_sel_v=cache_sel_v,
             state_win_k=state_win_k, state_win_v=state_win_v, state_conv=state_conv, state_rglru=state_rglru,
             page_table=page_table, norm_mix=norm_mix, w_in_ab=w_in_ab, w_out_ab=w_out_ab, conv_w=conv_w,
             conv_b=conv_b, w_rg_a=w_rg_a, b_rg_a=b_rg_a, w_rg_i=w_rg_i, b_rg_i=b_rg_i, rg_lambda=rg_lambda,
             w_in_c=w_in_c, w_out_c=w_out_c, w_cmp_k1=w_cmp_k1, w_cmp_k2=w_cmp_k2, pe_cmp_k=pe_cmp_k,
             w_cmp_v1=w_cmp_v1, w_cmp_v2=w_cmp_v2, pe_cmp_v=pe_cmp_v)
    bp, tp, d = x_prompt.shape
    bs, ts, _ = x_sample.shape
    dims = (bp, tp, bs, ts)
    yp = x_prompt.reshape(bp * tp, d)
    ys = x_sample.reshape(bs * ts, d)
    ab_p, ab_s, c_p, c_s = [], [], [], []
    for layer in range(norm_mix.shape[0]):
        l = layer // 2
        if layer % 2 == 0:
            yp, ys, o_p, o_s = _layer_ab(layer, l, yp, ys, dims, p)
            ab_p.append(o_p)
            ab_s.append(o_s)
        else:
            yp, ys, o_p, o_s = _layer_c(layer, l, yp, ys, dims, p)
            c_p.append(o_p)
            c_s.append(o_s)
        wr, br = _router_weights(w_route_group[layer], b_route_group[layer], w_route_expert[layer],
                                 b_route_expert[layer])
        moe_w = (w_exp_gate[layer].astype(BF16), w_exp_up[layer].astype(BF16), w_exp_down[layer].astype(BF16))
        yp = hier_moe_residual(yp, norm_ffn[layer], wr, br, *moe_w)
        ys = hier_moe_residual(ys, norm_ffn[layer], wr, br, *moe_w)
    y_prompt = rmsnorm(yp, norm_final).reshape(bp, tp, d)
    y_sample = rmsnorm(ys, norm_final).reshape(bs, ts, d)
    stack = lambda outs, i: jnp.stack([o[i] for o in outs])
    res = [y_prompt, y_sample]
    for i in range(4):
        res += [stack(ab_p, i), stack(ab_s, i)]
    for i in range(6):
        res += [stack(c_p, i), stack(c_s, i)]
    return tuple(res)
```

```python
import functools

import numpy as np
import jax
import jax.numpy as jnp
from jax import lax
from jax.experimental import pallas as pl
from jax.experimental.pallas import tpu as pltpu

F32 = jnp.float32
BF16 = jnp.bfloat16
HIGHEST = lax.Precision.HIGHEST

HEAD_DIM = 64
LANES = 128
SUBLANES = 8
CONV_W = 4
RG_C = 8.0
MOBA_BLOCK = 256
MOBA_TOPK = 3
CMP_STRIDE = 16
CMP_LEN = 32
SEL_BLOCK = 64
SEL_TOPK = 16
WINDOW = 512
N_GROUPS = 4
EXPERTS_PER_GROUP = 8
N_EXPERTS = N_GROUPS * EXPERTS_PER_GROUP
PAGE_SIZE = 128
RMS_EPS = 1e-6
NEG_BIG = -1e30
VMEM_LIMIT = 56 * 1024 * 1024
MOE_BLOCK_ROWS = 256


def _cparams(sem):
    return pltpu.CompilerParams(dimension_semantics=sem, vmem_limit_bytes=VMEM_LIMIT)


def _alibi_slopes(n):
    return np.asarray(2.0 ** (-8.0 * np.arange(1, n + 1) / n), dtype=np.float32)


def _row_tile(m, pref=512):
    return pref if m % pref == 0 else m


def _dot_nt(a, b, precision=None):
    return lax.dot_general(a, b, (((1,), (1,)), ((), ())), precision=precision,
                           preferred_element_type=F32)


def _norm_proj_body(out_map, x_ref, g_ref, *refs):
    n_w = max(out_map) + 1
    w_refs, o_refs = refs[:n_w], refs[n_w:]
    x = x_ref[...]
    y = x * lax.rsqrt(jnp.mean(x * x, axis=-1, keepdims=True) + RMS_EPS) * g_ref[...]
    yb = y.astype(BF16)
    prods = [jnp.dot(yb, w_ref[...], preferred_element_type=F32) for w_ref in w_refs]
    for wi, o_ref in zip(out_map, o_refs):
        o_ref[...] = prods[wi].astype(o_ref.dtype)


def norm_proj(x, g, ws, outs):
    m, d = x.shape
    tm = _row_tile(m)
    out_map = tuple(wi for wi, _ in outs)
    return pl.pallas_call(
        functools.partial(_norm_proj_body, out_map),
        grid=(m // tm,),
        in_specs=[pl.BlockSpec((tm, d), lambda i: (i, 0)), pl.BlockSpec((1, d), lambda i: (0, 0))]
        + [pl.BlockSpec(w.shape, lambda i: (0, 0)) for w in ws],
        out_specs=[pl.BlockSpec((tm, ws[wi].shape[1]), lambda i: (i, 0)) for wi, _ in outs],
        out_shape=[jax.ShapeDtypeStruct((m, ws[wi].shape[1]), dt) for wi, dt in outs],
        compiler_params=_cparams(("parallel",)),
        name="norm_proj",
    )(x, g.reshape(1, d), *ws)


def _proj_res_body(n_in, *refs):
    x_refs, w_refs = refs[:n_in], refs[n_in:2 * n_in]
    res_ref, o_ref = refs[2 * n_in], refs[2 * n_in + 1]
    acc = jnp.dot(x_refs[0][...].astype(BF16), w_refs[0][...], preferred_element_type=F32)
    for x_ref, w_ref in zip(x_refs[1:], w_refs[1:]):
        acc = acc + jnp.dot(x_ref[...].astype(BF16), w_ref[...], preferred_element_type=F32)
    o_ref[...] = res_ref[...] + acc


def proj_residual(xs, ws, res):
    m, d = res.shape
    tm = _row_tile(m)
    return pl.pallas_call(
        functools.partial(_proj_res_body, len(xs)),
        grid=(m // tm,),
        in_specs=[pl.BlockSpec((tm, x.shape[1]), lambda i: (i, 0)) for x in xs]
        + [pl.BlockSpec(w.shape, lambda i: (0, 0)) for w in ws]
        + [pl.BlockSpec((tm, d), lambda i: (i, 0))],
        out_specs=pl.BlockSpec((tm, d), lambda i: (i, 0)),
        out_shape=jax.ShapeDtypeStruct((m, d), F32),
        compiler_params=_cparams(("parallel",)),
        name="proj_residual",
    )(*xs, *ws, res)


def _rmsnorm_body(x_ref, g_ref, o_ref):
    x = x_ref[...]
    o_ref[...] = x * lax.rsqrt(jnp.mean(x * x, axis=-1, keepdims=True) + RMS_EPS) * g_ref[...]


def rmsnorm(x, g):
    m, d = x.shape
    tm = _row_tile(m)
    return pl.pallas_call(
        _rmsnorm_body,
        grid=(m // tm,),
        in_specs=[pl.BlockSpec((tm, d), lambda i: (i, 0)), pl.BlockSpec((1, d), lambda i: (0, 0))],
        out_specs=pl.BlockSpec((tm, d), lambda i: (i, 0)),
        out_shape=jax.ShapeDtypeStruct((m, d), F32),
        compiler_params=_cparams(("parallel",)),
        name="rmsnorm",
    )(x, g.reshape(1, d))


_XIN_PAD = 8


def _rglru_body(xr_ref, gr_ref, cb_ref, h0_ref, cw_ref, cbias_ref, wa_ref, ba_ref, wi_ref, bi_ref,
                lam_ref, y_ref, tail_ref, hl_ref, xin_s, a_s, b_s, h_s):
    tt = xr_ref.shape[1]
    keep = CONV_W - 1

    @pl.when(pl.program_id(1) == 0)
    def _():
        xin_s[_XIN_PAD - keep:_XIN_PAD, :] = cb_ref[0]
        h_s[...] = h0_ref[0]

    xin_s[_XIN_PAD:_XIN_PAD + tt, :] = xr_ref[0]
    xc = cbias_ref[...]
    for k in range(CONV_W):
        lo = _XIN_PAD - keep + k
        xc = xc + xin_s[lo:lo + tt, :] * cw_ref[k:k + 1, :]
    new_tail = xin_s[_XIN_PAD + tt - keep:_XIN_PAD + tt, :]
    xin_s[_XIN_PAD - keep:_XIN_PAD, :] = new_tail
    tail_ref[0] = new_tail

    xcb = xc.astype(BF16)
    r = jax.nn.sigmoid(jnp.dot(xcb, wa_ref[...], preferred_element_type=F32) + ba_ref[...])
    gi = jax.nn.sigmoid(jnp.dot(xcb, wi_ref[...], preferred_element_type=F32) + bi_ref[...])
    nl = -lam_ref[...]
    softplus = jnp.maximum(nl, 0.0) + jnp.log1p(jnp.exp(-jnp.abs(nl)))
    log_a = (-RG_C) * r * softplus
    a = jnp.exp(log_a)
    a_s[...] = a
    b_s[...] = jnp.sqrt(1.0 - a * a) * (gi * xc)

    def step(t, h):
        h = a_s[pl.ds(t, 1), :] * h + b_s[pl.ds(t, 1), :]
        b_s[pl.ds(t, 1), :] = h
        return h

    h = lax.fori_loop(0, tt, step, h_s[...], unroll=min(tt, 8))
    h_s[...] = h
    hl_ref[0] = h
    y_ref[0] = b_s[...] * jax.nn.gelu(gr_ref[0])


def rglru(xr, gr, conv_buf, h0, conv_w, conv_b, wa_bd, b_a, wi_bd, b_i, lam):
    n, t, w = xr.shape
    tt = min(t, 512)
    assert t % tt == 0 and tt >= CONV_W - 1
    row = lambda v: v.reshape(1, w)
    full = lambda shp: pl.BlockSpec(shp, lambda b, s: (0,) * len(shp))
    y, tail, hl = pl.pallas_call(
        _rglru_body,
        grid=(n, t // tt),
        in_specs=[pl.BlockSpec((1, tt, w), lambda b, s: (b, s, 0)),
                  pl.BlockSpec((1, tt, w), lambda b, s: (b, s, 0)),
                  pl.BlockSpec((1, CONV_W - 1, w), lambda b, s: (b, 0, 0)),
                  pl.BlockSpec((1, 1, w), lambda b, s: (b, 0, 0)),
                  full((CONV_W, w)), full((1, w)), full((w, w)), full((1, w)), full((w, w)),
                  full((1, w)), full((1, w))],
        out_specs=[pl.BlockSpec((1, tt, w), lambda b, s: (b, s, 0)),
                   pl.BlockSpec((1, CONV_W - 1, w), lambda b, s: (b, 0, 0)),
                   pl.BlockSpec((1, 1, w), lambda b, s: (b, 0, 0))],
        out_shape=[jax.ShapeDtypeStruct((n, t, w), F32),
                   jax.ShapeDtypeStruct((n, CONV_W - 1, w), F32),
                   jax.ShapeDtypeStruct((n, 1, w), F32)],
        scratch_shapes=[pltpu.VMEM((_XIN_PAD + tt, w), F32), pltpu.VMEM((tt, w), F32),
                        pltpu.VMEM((tt, w), F32), pltpu.VMEM((1, w), F32)],
        compiler_params=_cparams(("parallel", "arbitrary")),
        name="rglru",
    )(xr, gr, conv_buf, h0.reshape(n, 1, w), conv_w, row(conv_b), wa_bd, row(b_a), wi_bd, row(b_i), row(lam))
    return y, tail, hl.reshape(n, w)


def _block_diag(w):
    h, d, _ = w.shape
    eye = jnp.eye(h, dtype=w.dtype)
    return (eye[:, None, :, None] * w[:, :, None, :]).reshape(h * d, h * d)


def _block_mean_body(blk, k_ref, o_ref):
    x = k_ref[0]
    nb = x.shape[0] // blk
    o_ref[0] = jnp.mean(x.reshape(nb, blk, x.shape[1]), axis=1)


def block_mean(k, blk):
    n, t, w = k.shape
    nb = t // blk
    g = SUBLANES if nb % SUBLANES == 0 else nb
    return pl.pallas_call(
        functools.partial(_block_mean_body, blk),
        grid=(n, nb // g),
        in_specs=[pl.BlockSpec((1, g * blk, w), lambda b, s: (b, s, 0))],
        out_specs=pl.BlockSpec((1, g, w), lambda b, s: (b, s, 0)),
        out_shape=jax.ShapeDtypeStruct((n, nb, w), F32),
        compiler_params=_cparams(("parallel", "parallel")),
        name="block_mean",
    )(k)


def _first_argmax(g, idx_f):
    m = jnp.max(g, axis=-1, keepdims=True)
    return jnp.min(jnp.where(g == m, idx_f, jnp.inf), axis=-1, keepdims=True)


def _split_hi_lo(x):
    hi = x.astype(BF16)
    lo = (x - hi.astype(F32)).astype(BF16)
    return hi, lo


_CHUNK_ROWS = 128


def _row_chunks(x):
    return [x[r:r + _CHUNK_ROWS] for r in range(0, x.shape[0], _CHUNK_ROWS)]


def _flash_init(rows):
    return [(jnp.full((_CHUNK_ROWS, 1), -jnp.inf, F32), jnp.zeros((_CHUNK_ROWS, LANES), F32))
            for _ in range(rows // _CHUNK_ROWS)]


def _causal_chunk_masks(tq, tk):
    r = lax.broadcasted_iota(jnp.int32, (_CHUNK_ROWS, tk), 0)
    c = lax.broadcasted_iota(jnp.int32, (_CHUNK_ROWS, tk), 1)
    return [c <= r + off for off in range(0, tq, _CHUNK_ROWS)]


def _flash_update(s_all, v_ones, carry, masks=None):
    stats, ps = [], []
    for ci, (m, _) in enumerate(carry):
        s = s_all[ci * _CHUNK_ROWS:(ci + 1) * _CHUNK_ROWS]
        if masks is not None:
            s = jnp.where(masks[ci % len(masks)], s, NEG_BIG)
        m_new = jnp.maximum(m, jnp.max(s, axis=-1, keepdims=True))
        ps.append(jnp.exp(s - m_new).astype(BF16))
        stats.append((m_new, jnp.exp(m - m_new)))
    pv_all = jnp.dot(jnp.concatenate(ps, axis=0), v_ones, preferred_element_type=F32)
    return [(m_new, alpha * acc + pv_all[ci * _CHUNK_ROWS:(ci + 1) * _CHUNK_ROWS])
            for ci, ((m_new, alpha), (_, acc)) in enumerate(zip(stats, carry))]


def _moba_prompt_body(nbk, slopes_ref, q_ref, km_ref, kb_ref, vb_ref, o_ref):
    p_idx = pl.program_id(1)
    i = pl.program_id(2)
    tq = q_ref.shape[1]
    q = q_ref[0]
    lane = lax.broadcasted_iota(jnp.int32, (tq, LANES), 1)
    lane_f = lane.astype(F32)
    km = km_ref[0]
    halves = []
    for hh in range(2):
        in_head = (lane >= hh * HEAD_DIM) & (lane < (hh + 1) * HEAD_DIM)
        qh = jnp.where(in_head, q, 0.0)
        gate = _dot_nt(qh, km, HIGHEST)
        g = jnp.where(lane < i, gate, -jnp.inf)
        sel = lane == i
        for _ in range(MOBA_TOPK):
            pick = lane_f == _first_argmax(g, lane_f)
            sel = sel | (pick & (lane < i))
            g = jnp.where(pick, -jnp.inf, g)
        slope = slopes_ref[2 * p_idx + hh]
        pb = jnp.where(sel, 0.0, NEG_BIG) + slope * (lane_f * MOBA_BLOCK)
        pb_hi, pb_lo = _split_hi_lo(pb)
        pb_lo = pltpu.roll(pb_lo.astype(F32), nbk, 1)
        s_hi, s_lo = _split_hi_lo(jnp.full((tq, LANES), slope, F32))
        aug = jnp.where(lane < nbk, pb_hi.astype(F32),
                        jnp.where(lane < 2 * nbk, pb_lo,
                                  jnp.where(lane == 2 * nbk, s_hi.astype(F32),
                                            jnp.where(lane == 2 * nbk + 1, s_lo.astype(F32), 0.0))))
        halves.append(jnp.concatenate([(qh * HEAD_DIM ** -0.5).astype(BF16), aug.astype(BF16)], axis=1))
    qa = jnp.concatenate(halves, axis=0)

    klane = lax.broadcasted_iota(jnp.int32, (MOBA_BLOCK, LANES), 1)
    krow = lax.broadcasted_iota(jnp.int32, (MOBA_BLOCK, LANES), 0).astype(F32)
    ones_lane = (HEAD_DIM, 0)
    causal_masks = _causal_chunk_masks(tq, MOBA_BLOCK)

    def scores(j):
        kt = kb_ref[0, pl.ds(pl.multiple_of(j * MOBA_BLOCK, MOBA_BLOCK), MOBA_BLOCK), :]
        kaug = jnp.where((klane == j) | (klane == nbk + j), 1.0,
                         jnp.where((klane == 2 * nbk) | (klane == 2 * nbk + 1), krow, 0.0)).astype(BF16)
        return _dot_nt(qa, jnp.concatenate([kt, kaug], axis=1))

    def update(j, s, carry, causal):
        vt = vb_ref[0, pl.ds(pl.multiple_of(j * MOBA_BLOCK, MOBA_BLOCK), MOBA_BLOCK), :]
        new = []
        for hh in range(2):
            in_head = (klane >= hh * HEAD_DIM) & (klane < (hh + 1) * HEAD_DIM)
            v_h = jnp.where(in_head, vt, jnp.where(klane == ones_lane[hh], 1.0, 0.0).astype(BF16))
            new.append(_flash_update(s[hh * tq:(hh + 1) * tq], v_h, carry[hh], causal_masks if causal else None))
        return new

    def body(j, c):
        s, carry = c
        return scores(j + 1), update(j, s, carry, False)

    s, carry = lax.fori_loop(0, i, body, (scores(0), [_flash_init(tq), _flash_init(tq)]))
    carry = update(i, s, carry, True)
    outs = [jnp.concatenate([acc / acc[:, ones_lane[hh]:ones_lane[hh] + 1] for _, acc in carry[hh]], axis=0)
            for hh in range(2)]
    o_ref[0] = jnp.where(lane < HEAD_DIM, outs[0], outs[1])


def moba_prompt(q, kb, vb, kmean, slopes):
    n, t, w = q.shape
    nbk = t // MOBA_BLOCK
    assert t % MOBA_BLOCK == 0 and nbk >= MOBA_TOPK and 2 * nbk + 2 <= LANES
    km = jnp.pad(kmean, ((0, 0), (0, LANES - nbk), (0, 0)))
    grid_spec = pltpu.PrefetchScalarGridSpec(
        num_scalar_prefetch=0,
        grid=(n, w // LANES, nbk),
        in_specs=[pl.BlockSpec(memory_space=pltpu.SMEM),
                  pl.BlockSpec((1, MOBA_BLOCK, LANES), lambda b, p, i: (b, i, p)),
                  pl.BlockSpec((1, LANES, LANES), lambda b, p, i: (b, 0, p)),
                  pl.BlockSpec((1, t, LANES), lambda b, p, i: (b, 0, p)),
                  pl.BlockSpec((1, t, LANES), lambda b, p, i: (b, 0, p))],
        out_specs=pl.BlockSpec((1, MOBA_BLOCK, LANES), lambda b, p, i: (b, i, p)),
    )
    return pl.pallas_call(
        functools.partial(_moba_prompt_body, nbk),
        grid_spec=grid_spec,
        out_shape=jax.ShapeDtypeStruct((n, t, w), F32),
        compiler_params=_cparams(("parallel", "parallel", "arbitrary")),
        name="moba_prompt",
    )(jnp.asarray(slopes), q, km, kb, vb)


_ROUTE_E, _ROUTE_W, _ROUTE_RANK = 0, 2, 4


def _router_body(x_ref, g_ref, wr_ref, br_ref, xn_ref, route_ref, cnt_ref, carry_s):
    tm = x_ref.shape[0]

    @pl.when(pl.program_id(0) == 0)
    def _():
        carry_s[...] = jnp.zeros_like(carry_s)

    x = x_ref[...]
    xn = x * lax.rsqrt(jnp.mean(x * x, axis=-1, keepdims=True) + RMS_EPS) * g_ref[...]
    xn_ref[...] = xn
    logits = jnp.dot(xn, wr_ref[...], precision=HIGHEST, preferred_element_type=F32) + br_ref[...]
    lane = lax.broadcasted_iota(jnp.int32, (tm, LANES), 1)
    is_g = lane < N_GROUPS
    gl = jnp.where(is_g, logits, -jnp.inf)
    gmax = jnp.max(gl, axis=-1, keepdims=True)
    g_idx = jnp.min(jnp.where(gl == gmax, lane, LANES), axis=-1, keepdims=True)
    g_w = 1.0 / jnp.sum(jnp.where(is_g, jnp.exp(gl - gmax), 0.0), axis=-1, keepdims=True)
    e_lo = N_GROUPS + g_idx * EXPERTS_PER_GROUP
    el = jnp.where((lane >= e_lo) & (lane < e_lo + EXPERTS_PER_GROUP), logits, -jnp.inf)
    v1 = jnp.max(el, axis=-1, keepdims=True)
    i1 = jnp.min(jnp.where(el == v1, lane, LANES), axis=-1, keepdims=True)
    el2 = jnp.where(lane == i1, -jnp.inf, el)
    v2 = jnp.max(el2, axis=-1, keepdims=True)
    i2 = jnp.min(jnp.where(el2 == v2, lane, LANES), axis=-1, keepdims=True)
    e21 = jnp.exp(v2 - v1)
    p1 = 1.0 / (1.0 + e21)
    w1 = g_w * p1
    w2 = g_w * (e21 * p1)
    hot1 = lane == i1
    hot2 = lane == i2
    cnt = jnp.where(hot1 | hot2, 1.0, 0.0)
    r = lax.broadcasted_iota(jnp.int32, (tm, tm), 0)
    c = lax.broadcasted_iota(jnp.int32, (tm, tm), 1)
    tri = jnp.where(c < r, 1.0, 0.0).astype(BF16)
    before = jnp.dot(tri, cnt.astype(BF16), preferred_element_type=F32) + carry_s[...]
    r1 = jnp.sum(jnp.where(hot1, before, 0.0), axis=-1, keepdims=True)
    r2 = jnp.sum(jnp.where(hot2, before, 0.0), axis=-1, keepdims=True)
    carry_s[...] = carry_s[...] + jnp.sum(cnt, axis=0, keepdims=True)
    cnt_ref[...] = carry_s[...]
    route = jnp.zeros((tm, LANES), F32)
    for pos, val in ((_ROUTE_E, (i1 - N_GROUPS).astype(F32)), (_ROUTE_E + 1, (i2 - N_GROUPS).astype(F32)),
                     (_ROUTE_W, w1), (_ROUTE_W + 1, w2), (_ROUTE_RANK, r1), (_ROUTE_RANK + 1, r2)):
        route = jnp.where(lane == pos, val, route)
    route_ref[...] = route


def moe_router(y, g, wr, br):
    m, d = y.shape
    tm = _row_tile(m, 256)
    return pl.pallas_call(
        _router_body,
        grid=(m // tm,),
        in_specs=[pl.BlockSpec((tm, d), lambda i: (i, 0)), pl.BlockSpec((1, d), lambda i: (0, 0)),
                  pl.BlockSpec((d, LANES), lambda i: (0, 0)), pl.BlockSpec((1, LANES), lambda i: (0, 0))],
        out_specs=[pl.BlockSpec((tm, d), lambda i: (i, 0)), pl.BlockSpec((tm, LANES), lambda i: (i, 0)),
                   pl.BlockSpec((1, LANES), lambda i: (0, 0))],
        out_shape=[jax.ShapeDtypeStruct((m, d), F32), jax.ShapeDtypeStruct((m, LANES), F32),
                   jax.ShapeDtypeStruct((1, LANES), F32)],
        scratch_shapes=[pltpu.VMEM((1, LANES), F32)],
        compiler_params=_cparams(("arbitrary",)),
        name="moe_router",
    )(y, g.reshape(1, d), wr, br)


def _row_copy(src, s_row, dst, d_row, sem):
    return pltpu.make_async_copy(src.at[pl.ds(s_row, 1), :], dst.at[pl.ds(d_row, 1), :], sem)


def _scatter_body(dest_ref, xn_ref, xs_in_ref, xs_ref, sem):
    del xs_in_ref
    tm = xn_ref.shape[0]
    base = pl.program_id(0) * tm

    def issue(r, _):
        for s in range(2):
            _row_copy(xn_ref, r, xs_ref, dest_ref[2 * (base + r) + s], sem).start()
        return 0

    def drain(r, _):
        for s in range(2):
            _row_copy(xn_ref, r, xs_ref, dest_ref[2 * (base + r) + s], sem).wait()
        return 0

    lax.fori_loop(0, tm, issue, 0)
    lax.fori_loop(0, tm, drain, 0)


def moe_scatter(dest, xn, rows):
    m, d = xn.shape
    tm = _row_tile(m, 256)
    grid_spec = pltpu.PrefetchScalarGridSpec(
        num_scalar_prefetch=1,
        grid=(m // tm,),
        in_specs=[pl.BlockSpec((tm, d), lambda i, dest: (i, 0)), pl.BlockSpec(memory_space=pl.ANY)],
        out_specs=pl.BlockSpec(memory_space=pl.ANY),
        scratch_shapes=[pltpu.SemaphoreType.DMA(())],
    )
    return pl.pallas_call(
        _scatter_body,
        grid_spec=grid_spec,
        out_shape=jax.ShapeDtypeStruct((rows, d), F32),
        input_output_aliases={2: 0},
        compiler_params=_cparams(("arbitrary",)),
        name="moe_scatter",
    )(dest, xn, jnp.zeros((rows, d), F32))


def _expert_body(blk_e_ref, nused_ref, xs_ref, w1_ref, w3_ref, w2_ref, ys_ref):
    del blk_e_ref
    used = pl.program_id(0) < nused_ref[0]

    @pl.when(used)
    def _():
        x = xs_ref[...].astype(BF16)
        h = jax.nn.silu(jnp.dot(x, w1_ref[0], preferred_element_type=F32))
        h = h * jnp.dot(x, w3_ref[0], preferred_element_type=F32)
        ys_ref[...] = jnp.dot(h.astype(BF16), w2_ref[0], preferred_element_type=F32)

    @pl.when(jnp.logical_not(used))
    def _():
        ys_ref[...] = jnp.zeros_like(ys_ref)


def moe_experts(blk_e, nused, xs, w1, w3, w2):
    rows, d = xs.shape
    te = MOE_BLOCK_ROWS
    de = w1.shape[2]
    grid_spec = pltpu.PrefetchScalarGridSpec(
        num_scalar_prefetch=2,
        grid=(rows // te,),
        in_specs=[pl.BlockSpec((te, d), lambda i, be, nu: (i, 0)),
                  pl.BlockSpec((1, d, de), lambda i, be, nu: (be[i], 0, 0)),
                  pl.BlockSpec((1, d, de), lambda i, be, nu: (be[i], 0, 0)),
                  pl.BlockSpec((1, de, d), lambda i, be, nu: (be[i], 0, 0))],
        out_specs=pl.BlockSpec((te, d), lambda i, be, nu: (i, 0)),
    )
    return pl.pallas_call(
        _expert_body,
        grid_spec=grid_spec,
        out_shape=jax.ShapeDtypeStruct((rows, d), F32),
        compiler_params=_cparams(("arbitrary",)),
        name="moe_experts",
    )(blk_e, nused, xs, w1, w3, w2)


def _combine_body(dest_ref, route_ref, res_ref, ys_ref, o_ref, buf, sem):
    tm = res_ref.shape[0]
    base = pl.program_id(0) * tm

    def issue(r, _):
        for s in range(2):
            _row_copy(ys_ref, dest_ref[2 * (base + r) + s], buf.at[s], r, sem).start()
        return 0

    def drain(r, _):
        for s in range(2):
            _row_copy(ys_ref, dest_ref[2 * (base + r) + s], buf.at[s], r, sem).wait()
        return 0

    lax.fori_loop(0, tm, issue, 0)
    lax.fori_loop(0, tm, drain, 0)
    route = route_ref[...]
    w1 = route[:, _ROUTE_W:_ROUTE_W + 1]
    w2 = route[:, _ROUTE_W + 1:_ROUTE_W + 2]
    o_ref[...] = res_ref[...] + (w1 * buf[0] + w2 * buf[1])


def moe_combine(dest, route, res, ys):
    m, d = res.shape
    tm = _row_tile(m, 256)
    grid_spec = pltpu.PrefetchScalarGridSpec(
        num_scalar_prefetch=1,
        grid=(m // tm,),
        in_specs=[pl.BlockSpec((tm, LANES), lambda i, dest: (i, 0)),
                  pl.BlockSpec((tm, d), lambda i, dest: (i, 0)),
                  pl.BlockSpec(memory_space=pl.ANY)],
        out_specs=pl.BlockSpec((tm, d), lambda i, dest: (i, 0)),
        scratch_shapes=[pltpu.VMEM((2, tm, d), F32), pltpu.SemaphoreType.DMA(())],
    )
    return pl.pallas_call(
        _combine_body,
        grid_spec=grid_spec,
        out_shape=jax.ShapeDtypeStruct((m, d), F32),
        compiler_params=_cparams(("arbitrary",)),
        name="moe_combine",
    )(dest, route, res, ys)


def hier_moe_residual(y, g, wr, br, w1, w3, w2):
    m, d = y.shape
    te = MOE_BLOCK_ROWS
    xn, route, cnt = moe_router(y, g, wr, br)
    counts = cnt[0, N_GROUPS:N_GROUPS + N_EXPERTS].astype(jnp.int32)
    padded = (counts + te - 1) // te * te
    pend = jnp.cumsum(padded)
    pstart = pend - padded
    e = route[:, _ROUTE_E:_ROUTE_E + 2].astype(jnp.int32)
    rank = route[:, _ROUTE_RANK:_ROUTE_RANK + 2].astype(jnp.int32)
    dest = (pstart[e] + rank).reshape(-1)
    n_blk = -(-(2 * m + N_EXPERTS * (te - 1)) // te)
    blk_e = jnp.minimum(jnp.searchsorted(pend, jnp.arange(n_blk, dtype=jnp.int32) * te, side="right"),
                        N_EXPERTS - 1).astype(jnp.int32)
    nused = (pend[-1:] // te).astype(jnp.int32)
    xs = moe_scatter(dest, xn, n_blk * te)
    ys = moe_experts(blk_e, nused, xs, w1, w3, w2)
    return moe_combine(dest, route, y, ys)


def _router_weights(w_rg, b_rg, w_re, b_re):
    d = w_rg.shape[0]
    pad = LANES - N_GROUPS - N_EXPERTS
    wr = jnp.concatenate([w_rg, w_re, jnp.zeros((d, pad), F32)], axis=1)
    br = jnp.concatenate([b_rg, b_re, jnp.zeros((pad,), F32)]).reshape(1, LANES)
    return wr, br


_POS_LANE = HEAD_DIM


def _compress_tail(ab, bias, w2_ref, add_pos):
    c = ab.shape[0] - SUBLANES
    gw = ab.shape[1] // 2
    h = ab[0:c, 0:gw] + ab[1:c + 1, gw:2 * gw] + bias
    out = jnp.dot(jax.nn.gelu(h).astype(BF16), w2_ref[...], preferred_element_type=F32)
    if add_pos:
        lane = lax.broadcasted_iota(jnp.int32, out.shape, 1) % LANES
        cend = lax.broadcasted_iota(jnp.int32, out.shape, 0) * CMP_STRIDE + (CMP_LEN - 1)
        hi = ((cend // 256) * 256).astype(F32)
        lo = (cend % 256).astype(F32)
        out = jnp.where((lane == _POS_LANE) | (lane == _POS_LANE + 1), hi,
                        jnp.where((lane == _POS_LANE + 2) | (lane == _POS_LANE + 3), lo, out))
    return out


def _compress_body(add_pos, n_k, *refs):
    k_refs = refs[:n_k]
    wab_ref, pe_ref, w2_ref, o_ref, ab_s = refs[n_k:]
    c = k_refs[0].shape[1] // CMP_STRIDE
    gw = n_k * LANES
    acc = jnp.zeros((c + SUBLANES, 2 * gw), F32)
    for j in range(CMP_STRIDE):
        xk = jnp.concatenate([r[0, pl.ds(j, c, stride=CMP_STRIDE), :] for r in k_refs], axis=1)
        xj = jnp.concatenate([xk, pe_ref[j:j + 1, :], pe_ref[CMP_STRIDE + j:CMP_STRIDE + j + 1, :],
                              jnp.zeros((SUBLANES - 2, gw), F32)], axis=0).astype(BF16)
        acc = acc + jnp.dot(xj, wab_ref[j], preferred_element_type=F32)
    bias = acc[c:c + 1, 0:gw] + acc[c + 1:c + 2, gw:2 * gw]
    ab_s[0:c, :] = acc[0:c]
    ab_s[c:c + SUBLANES, :] = jnp.zeros((SUBLANES, 2 * gw), F32)
    o_ref[0] = _compress_tail(ab_s, bias, w2_ref, add_pos).astype(o_ref.dtype)


def _compress_weights(w1, w2, pe, groups):
    d = w2.shape[0]
    w1j = w1.reshape(CMP_LEN, d, d)
    eye = jnp.eye(groups, dtype=F32)
    bd = (eye[None, :, None, :, None] * w1j[:, None, :, None, :]).reshape(CMP_LEN, groups * d, groups * d)
    wab = jnp.concatenate([bd[:CMP_STRIDE], bd[CMP_STRIDE:]], axis=2).astype(BF16)
    pe_t = jnp.tile(pe, (1, groups))
    w2p = jnp.concatenate([w2, jnp.zeros((d, LANES - d), F32)], axis=1)
    w2bd = (eye[:, None, :, None] * w2p[None, :, None, :]).reshape(groups * d, groups * LANES).astype(BF16)
    return wab, pe_t, w2bd


def nsa_compress(k, wab, pe_t, w2bd, add_pos):
    n, t, gw = k.shape
    c = t // CMP_STRIDE
    n_k = gw // LANES
    full = lambda shp: pl.BlockSpec(shp, lambda b: (0,) * len(shp))
    return pl.pallas_call(
        functools.partial(_compress_body, add_pos, n_k),
        grid=(n,),
        in_specs=[pl.BlockSpec((1, t, LANES), functools.partial(lambda u, b: (b, 0, u), u)) for u in range(n_k)]
        + [full(wab.shape), full(pe_t.shape), full(w2bd.shape)],
        out_specs=pl.BlockSpec((1, c, w2bd.shape[1]), lambda b: (b, 0, 0)),
        out_shape=jax.ShapeDtypeStruct((n, c, w2bd.shape[1]), BF16),
        scratch_shapes=[pltpu.VMEM((c + SUBLANES, 2 * gw), F32)],
        compiler_params=_cparams(("parallel",)),
        name="nsa_compress",
    )(*([k] * n_k), wab, pe_t, w2bd)


def _stack_queries(q, slopes, lane):
    rows = []
    for r, slope in enumerate(slopes):
        q2 = q[:, (r // 2) * LANES:(r // 2 + 1) * LANES]
        if r % 2:
            q2 = pltpu.roll(q2, HEAD_DIM, 1)
        s_hi, s_lo = _split_hi_lo(jnp.full(q2.shape, slope, F32))
        odd = ((lane - _POS_LANE) % 2) == 1
        feat = jnp.where(odd, s_lo.astype(F32), s_hi.astype(F32))
        rows.append(jnp.where(lane < HEAD_DIM, q2 * HEAD_DIM ** -0.5,
                              jnp.where(lane < _POS_LANE + 4, feat, 0.0)).astype(BF16))
    return jnp.concatenate(rows, axis=0)


def _select_blocks(imp, qb, lane):
    vis = lane <= qb
    forced = (lane == 0) | (lane == qb) | (lane == qb - 1)
    score = jnp.where(vis & jnp.logical_not(forced), imp, -jnp.inf)
    lane_f = lane.astype(F32)
    sel = forced
    for _ in range(SEL_TOPK - 3):
        pick = lane_f == _first_argmax(score, lane_f)
        sel = sel | pick
        score = jnp.where(pick, -jnp.inf, score)
    return sel & vis


def _nsa_prompt_body(slopes_ref, q_ref, gt_ref, ck_ref, cv_ref, cover_ref, ks_ref, vs_ref, kw_ref, vw_ref,
                     o_ref):
    g = pl.program_id(1)
    i = pl.program_id(2)
    tq = q_ref.shape[1]
    r_heads = q_ref.shape[2] // HEAD_DIM
    rows = r_heads * tq
    t0 = i * tq
    lane = lax.broadcasted_iota(jnp.int32, (tq, LANES), 1)
    slopes = [slopes_ref[r_heads * g + r] for r in range(r_heads)]
    cpt = tq // _CHUNK_ROWS
    qa_chunks = _row_chunks(_stack_queries(q_ref[0], slopes, lane))

    ck = ck_ref[0]
    cv = cv_ref[0]
    cend = lax.broadcasted_iota(jnp.int32, (_CHUNK_ROWS, ck.shape[0]), 1) * CMP_STRIDE + (CMP_LEN - 1)
    crow = lax.broadcasted_iota(jnp.int32, (_CHUNK_ROWS, ck.shape[0]), 0)
    o_c, psum = [], [None] * cpt
    for ci, qc in enumerate(qa_chunks):
        ok_c = cend <= t0 + (ci % cpt) * _CHUNK_ROWS + crow
        s_c = jnp.where(ok_c, _dot_nt(qc, ck), NEG_BIG)
        e_c = jnp.where(ok_c, jnp.exp(s_c - jnp.max(s_c, axis=-1, keepdims=True)), 0.0)
        p_c = e_c / jnp.maximum(jnp.sum(e_c, axis=-1, keepdims=True), 1e-30)
        o_c.append(jnp.dot(p_c.astype(BF16), cv, preferred_element_type=F32))
        psum[ci % cpt] = p_c if psum[ci % cpt] is None else psum[ci % cpt] + p_c

    imp = jnp.dot(jnp.concatenate(psum, axis=0), cover_ref[...], precision=HIGHEST,
                  preferred_element_type=F32)
    qb = (t0 + lax.broadcasted_iota(jnp.int32, (tq, LANES), 0)) // SEL_BLOCK
    pen_chunks = _row_chunks(jnp.where(_select_blocks(imp, qb, lane), 0.0, NEG_BIG).astype(BF16))
    qa = jnp.concatenate(qa_chunks, axis=0)
    qs = jnp.concatenate([qa, jnp.concatenate([pen_chunks[ci % cpt] for ci in range(len(qa_chunks))], axis=0)],
                         axis=1)

    tk = tq
    klane = lax.broadcasted_iota(jnp.int32, (tk, LANES), 1)
    krow = lax.broadcasted_iota(jnp.int32, (tk, LANES), 0)
    krow_b = krow.astype(F32).astype(BF16)
    is_pos = (klane == _POS_LANE) | (klane == _POS_LANE + 1)
    is_off = (klane == _POS_LANE + 2) | (klane == _POS_LANE + 3)
    blocks_per_tile = tk // SEL_BLOCK
    one_b = jnp.ones((tk, LANES), BF16)
    causal_masks = _causal_chunk_masks(tq, tk)

    def keys_with_pos(k_ref, j):
        kt = k_ref[0, pl.ds(pl.multiple_of(j * tk, tk), tk), :]
        off_b = jnp.full((tk, LANES), ((j - i) * tk).astype(F32), F32).astype(BF16)
        return jnp.where(is_pos, krow_b, jnp.where(is_off, off_b, kt))

    def values_with_ones(v_ref, j):
        return jnp.where(klane == _POS_LANE, one_b, v_ref[0, pl.ds(pl.multiple_of(j * tk, tk), tk), :])

    def sel_scores(j):
        onehot = jnp.where(klane == j * blocks_per_tile + krow // SEL_BLOCK, 1.0, 0.0).astype(BF16)
        return _dot_nt(qs, jnp.concatenate([keys_with_pos(ks_ref, j), onehot], axis=1))

    def sel_body(j, c):
        s, carry = c
        return sel_scores(j + 1), _flash_update(s, values_with_ones(vs_ref, j), carry)

    def finish(carry):
        return [acc / acc[:, _POS_LANE:_POS_LANE + 1] for _, acc in carry]

    s, carry = lax.fori_loop(0, i, sel_body, (sel_scores(0), _flash_init(rows)))
    o_s = finish(_flash_update(s, values_with_ones(vs_ref, i), carry, causal_masks))

    carry = _flash_init(rows)
    n_win = WINDOW // tk + 1
    wr = lax.broadcasted_iota(jnp.int32, (_CHUNK_ROWS, tk), 0)
    wc = lax.broadcasted_iota(jnp.int32, (_CHUNK_ROWS, tk), 1)
    for w in range(n_win):
        j_true = i - (n_win - 1) + w
        j = jnp.maximum(j_true, 0)
        masks = []
        for u in range(cpt):
            dist = (t0 - j * tk) + u * _CHUNK_ROWS + wr - wc
            masks.append((dist >= 0) & (dist < WINDOW) & (j_true >= 0))
        carry = _flash_update(_dot_nt(qa, keys_with_pos(kw_ref, j)), values_with_ones(vw_ref, j), carry, masks)
    o_w = finish(carry)

    gates = _row_chunks(jax.nn.sigmoid(gt_ref[0]))
    outs = []
    for r in range(r_heads):
        parts = []
        for u in range(cpt):
            ci = r * cpt + u
            gr = [gates[u][:, 3 * r + b:3 * r + b + 1] for b in range(3)]
            parts.append(gr[0] * o_c[ci] + gr[1] * o_s[ci] + gr[2] * o_w[ci])
        outs.append(jnp.concatenate(parts, axis=0))
    lane_o = lax.broadcasted_iota(jnp.int32, (tq, LANES), 1)
    pairs = [jnp.where(lane_o < HEAD_DIM, outs[2 * u], pltpu.roll(outs[2 * u + 1], HEAD_DIM, 1))
             for u in range(r_heads // 2)]
    o_ref[0] = jnp.concatenate(pairs, axis=1)


def nsa_prompt(q, gt, ck, cv, cover, ks, vs, kw, vw, slopes):
    n, t, w = q.shape
    groups = ks.shape[2] // LANES
    qw = w // groups
    tq = 256
    c = ck.shape[1]
    assert t % tq == 0 and -(-t // SEL_BLOCK) <= LANES and t // SEL_BLOCK >= SEL_TOPK
    kv_spec = lambda rows: pl.BlockSpec((1, rows, LANES), lambda b, g, i: (b, 0, g))
    grid_spec = pltpu.PrefetchScalarGridSpec(
        num_scalar_prefetch=0,
        grid=(n, groups, t // tq),
        in_specs=[pl.BlockSpec(memory_space=pltpu.SMEM),
                  pl.BlockSpec((1, tq, qw), lambda b, g, i: (b, i, g)),
                  pl.BlockSpec((1, tq, LANES), lambda b, g, i: (b, i, g)),
                  kv_spec(c), kv_spec(c),
                  pl.BlockSpec(cover.shape, lambda b, g, i: (0, 0)),
                  kv_spec(t), kv_spec(t), kv_spec(t), kv_spec(t)],
        out_specs=pl.BlockSpec((1, tq, qw), lambda b, g, i: (b, i, g)),
    )
    return pl.pallas_call(
        _nsa_prompt_body,
        grid_spec=grid_spec,
        out_shape=jax.ShapeDtypeStruct((n, t, w), F32),
        compiler_params=_cparams(("parallel", "parallel", "arbitrary")),
        name="nsa_prompt",
    )(jnp.asarray(slopes), q, gt, ck, cv, cover, ks, vs, kw, vw)


def _nsa_cover(n_cmp_rows, n_sel):
    i = np.arange(n_cmp_rows)[:, None]
    j = np.arange(-(-n_sel // LANES) * LANES)[None, :]
    cov = (CMP_STRIDE * i < SEL_BLOCK * (j + 1)) & (CMP_STRIDE * i + CMP_LEN > SEL_BLOCK * j) & (j < n_sel)
    return jnp.asarray(cov.astype(np.float32))


def _pad_group_cols(w, width):
    d, gw = w.shape
    g = gw // width
    return jnp.pad(w.reshape(d, g, width), ((0, 0), (0, 0), (0, LANES - width))).reshape(d, g * LANES)


def _own_head_block(rows, width, nq):
    r = lax.broadcasted_iota(jnp.int32, (rows, width), 0)
    c = lax.broadcasted_iota(jnp.int32, (rows, width), 1)
    return (c // HEAD_DIM) == (r // nq)


def _fold_heads(x, own):
    width = x.shape[1]
    fold = jnp.where(lax.broadcasted_iota(jnp.int32, (width, HEAD_DIM), 0) % HEAD_DIM
                     == lax.broadcasted_iota(jnp.int32, (width, HEAD_DIM), 1), 1.0, 0.0)
    return jnp.dot(jnp.where(own, x, 0.0), fold, precision=HIGHEST, preferred_element_type=F32)


def _moba_decode_body(nblk, past, pt_ref, q_ref, slope_ref, kn_ref, vn_ref, k0_ref, k1_ref, v0_ref, v1_ref,
                      o_ref, m_s, l_s, g_s, acc_s):
    del pt_ref
    j = pl.program_id(1)
    rows, width = q_ref.shape[1], q_ref.shape[2]
    nq = rows // (width // HEAD_DIM)
    own = _own_head_block(rows, width, nq)
    q_hi, q_lo = _split_hi_lo(jnp.where(own, q_ref[0] * HEAD_DIM ** -0.5, 0.0))
    slope = slope_ref[...]
    lane = lax.broadcasted_iota(jnp.int32, (rows, LANES), 1)
    lane_f = lane.astype(F32)

    @pl.when(j == 0)
    def _():
        m_s[...] = jnp.full(m_s.shape, NEG_BIG, F32)
        l_s[...] = jnp.zeros(l_s.shape, F32)
        g_s[...] = jnp.zeros(g_s.shape, F32)

    lhs = jnp.concatenate([q_hi, q_lo], axis=0)
    logits = []
    gate = jnp.zeros((rows, 1), F32)
    for pg, k_ref in enumerate((k0_ref, k1_ref)):
        kt = k_ref[0, 0].reshape(width, PAGE_SIZE).astype(BF16)
        s2 = jnp.dot(lhs, kt, preferred_element_type=F32)
        gate = gate + jnp.sum(s2[:rows] + s2[rows:], axis=-1, keepdims=True)
        page_start = (j * MOBA_BLOCK + pg * PAGE_SIZE - past).astype(F32)
        logits.append(s2[:rows] + slope * (lane_f + page_start))
    m_j = jnp.maximum(jnp.max(logits[0], axis=-1, keepdims=True), jnp.max(logits[1], axis=-1, keepdims=True))
    l_j = jnp.zeros((rows, 1), F32)
    acc_j = jnp.zeros((rows, width), F32)
    for s, v_ref in zip(logits, (v0_ref, v1_ref)):
        p = jnp.exp(s - m_j)
        l_j = l_j + jnp.sum(p, axis=-1, keepdims=True)
        acc_j = acc_j + _dot_nt(p.astype(BF16), v_ref[0, 0].reshape(width, PAGE_SIZE).astype(BF16))
    m_s[...] = jnp.where(lane == j, m_j, m_s[...])
    l_s[...] = jnp.where(lane == j, l_j, l_s[...])
    g_s[...] = jnp.where(lane == j, gate, g_s[...])
    acc_s[j] = acc_j

    @pl.when(j == nblk - 1)
    def _():
        g = jnp.where(lane < nblk, g_s[...], -jnp.inf)
        sel = jnp.zeros(g.shape, jnp.bool_)
        for _ in range(MOBA_TOPK):
            pick = lane_f == _first_argmax(g, lane_f)
            sel = sel | pick
            g = jnp.where(pick, -jnp.inf, g)
        sel = sel & (lane < nblk)
        ok_o = (lane < nq) & (lane <= lax.broadcasted_iota(jnp.int32, (rows, LANES), 0) % nq)
        s_o = jnp.dot(q_hi, kn_ref[0].astype(BF16), preferred_element_type=F32) + slope * lane_f
        s_o = jnp.where(ok_o, s_o, NEG_BIG)
        m_o = jnp.max(s_o, axis=-1, keepdims=True)
        p_o = jnp.exp(s_o - m_o)
        l_o = jnp.sum(p_o, axis=-1, keepdims=True)
        acc_o = _dot_nt(p_o.astype(BF16), vn_ref[0].astype(BF16))
        m_all = jnp.maximum(jnp.max(jnp.where(sel, m_s[...], -jnp.inf), axis=-1, keepdims=True), m_o)
        w = jnp.where(sel, jnp.exp(m_s[...] - m_all), 0.0)
        w_o = jnp.exp(m_o - m_all)
        den = jnp.sum(w * l_s[...], axis=-1, keepdims=True) + w_o * l_o
        num = w_o * acc_o
        for jj in range(nblk):
            num = num + w[:, jj:jj + 1] * acc_s[jj]
        o_ref[0] = _fold_heads(num, own) / den


def moba_decode(q_tiled, kn_t, vn_t, cache_k, cache_v, page_table, layer, slopes):
    b, rows, width = q_tiled.shape
    heads = cache_k.shape[2]
    nq = rows // heads
    n_pages = page_table.shape[1]
    past = n_pages * PAGE_SIZE
    pages_per_blk = MOBA_BLOCK // PAGE_SIZE
    nblk = past // MOBA_BLOCK
    assert past % MOBA_BLOCK == 0 and MOBA_TOPK <= nblk <= LANES and pages_per_blk == 2
    slope_rows = jnp.asarray(np.repeat(slopes, nq).reshape(rows, 1))
    page = lambda pg: pl.BlockSpec((1, 1, heads, HEAD_DIM, PAGE_SIZE),
                                   lambda s, j, pt: (layer, pt[s, pages_per_blk * j + pg], 0, 0, 0))
    per_seq = lambda a: pl.BlockSpec((1,) + a.shape[1:], lambda s, j, pt: (s, 0, 0))
    grid_spec = pltpu.PrefetchScalarGridSpec(
        num_scalar_prefetch=1,
        grid=(b, nblk),
        in_specs=[per_seq(q_tiled), pl.BlockSpec((rows, 1), lambda s, j, pt: (0, 0)), per_seq(kn_t), per_seq(vn_t),
                  page(0), page(1), page(0), page(1)],
        out_specs=pl.BlockSpec((1, rows, HEAD_DIM), lambda s, j, pt: (s, 0, 0)),
        scratch_shapes=[pltpu.VMEM((rows, LANES), F32), pltpu.VMEM((rows, LANES), F32),
                        pltpu.VMEM((rows, LANES), F32), pltpu.VMEM((nblk, rows, width), F32)],
    )
    return pl.pallas_call(
        functools.partial(_moba_decode_body, nblk, past),
        grid_spec=grid_spec,
        out_shape=jax.ShapeDtypeStruct((b, rows, HEAD_DIM), F32),
        compiler_params=_cparams(("parallel", "arbitrary")),
        name="moba_decode",
    )(page_table, q_tiled, slope_rows, kn_t, vn_t, cache_k, cache_k, cache_v, cache_v)


_CHUNKS_PER_PAGE = PAGE_SIZE // CMP_STRIDE


def _page_ab_body(k_ref, wj_ref, o_ref, xs):
    pages, groups = k_ref.shape[1], k_ref.shape[2]
    for g in range(groups):
        xs[...] = jnp.swapaxes(k_ref[0, :, g], 1, 2)
        acc = jnp.zeros((pages * _CHUNKS_PER_PAGE, LANES), F32)
        for j in range(CMP_STRIDE):
            x = xs[:, pl.ds(j, _CHUNKS_PER_PAGE, stride=CMP_STRIDE), :]
            x = x.reshape(pages * _CHUNKS_PER_PAGE, HEAD_DIM).astype(BF16)
            acc = acc + jnp.dot(x, wj_ref[j], preferred_element_type=F32)
        o_ref[:, g * LANES:(g + 1) * LANES] = acc


def nsa_page_ab(cache, layer, wj):
    _, n_phys, groups, _, _ = cache.shape
    pages = next(p for p in (32, 16, 8, 4, 2, 1) if n_phys % p == 0)
    return pl.pallas_call(
        _page_ab_body,
        grid=(n_phys // pages,),
        in_specs=[pl.BlockSpec((1, pages, groups, HEAD_DIM, PAGE_SIZE), lambda i: (layer, i, 0, 0, 0)),
                  pl.BlockSpec(wj.shape, lambda i: (0, 0, 0))],
        out_specs=pl.BlockSpec((pages * _CHUNKS_PER_PAGE, groups * LANES), lambda i: (i, 0)),
        out_shape=jax.ShapeDtypeStruct((n_phys * _CHUNKS_PER_PAGE, groups * LANES), F32),
        scratch_shapes=[pltpu.VMEM((pages, PAGE_SIZE, HEAD_DIM), F32)],
        compiler_params=_cparams(("parallel",)),
        name="nsa_page_ab",
    )(cache, wj)


def _page_ab_copy(ab_hbm, pt_ref, s, p, buf, sem):
    return pltpu.make_async_copy(ab_hbm.at[pt_ref[s, p]], buf.at[pl.ds(p * _CHUNKS_PER_PAGE, _CHUNKS_PER_PAGE), :], sem)


def _decode_queries(q_ref, slope_ref):
    q = q_ref[0]
    lane = lax.broadcasted_iota(jnp.int32, q.shape, 1)
    s_hi, s_lo = _split_hi_lo(jnp.broadcast_to(slope_ref[...], q.shape))
    feat = jnp.where(((lane - _POS_LANE) % 2) == 1, s_lo.astype(F32), s_hi.astype(F32))
    return jnp.where(lane < HEAD_DIM, q * HEAD_DIM ** -0.5,
                     jnp.where(lane < _POS_LANE + 4, feat, 0.0)).astype(BF16)


def _nsa_decode_cmp_body(n_pages, past, pt_ref, q_ref, slope_ref, abk_ref, abv_ref, wjk_ref, wjv_ref, pek_ref,
                         pev_ref, w2k_ref, w2v_ref, cover_ref, oc_ref, pen_ref, bufk, bufv, sem):
    s = pl.program_id(0)
    c = n_pages * _CHUNKS_PER_PAGE
    rows = q_ref.shape[1]
    groups = abk_ref.shape[2] // LANES
    rpg = rows // groups
    nq = pen_ref.shape[2] // 2

    def issue(p, _):
        _page_ab_copy(abk_ref, pt_ref, s, p, bufk, sem.at[0]).start()
        _page_ab_copy(abv_ref, pt_ref, s, p, bufv, sem.at[1]).start()
        return 0

    def drain(p, _):
        _page_ab_copy(abk_ref, pt_ref, s, p, bufk, sem.at[0]).wait()
        _page_ab_copy(abv_ref, pt_ref, s, p, bufv, sem.at[1]).wait()
        return 0

    lax.fori_loop(0, n_pages, issue, 0)
    pad = jnp.zeros((SUBLANES, bufk.shape[1]), F32)
    bufk[c:c + SUBLANES, :] = pad
    bufv[c:c + SUBLANES, :] = pad
    lax.fori_loop(0, n_pages, drain, 0)

    def tokens(buf, wj_ref, pe_ref, w2_ref, g, add_pos):
        acc = jnp.zeros((SUBLANES, LANES), F32)
        for j in range(CMP_STRIDE):
            xj = jnp.concatenate([pe_ref[j:j + 1, :], pe_ref[CMP_STRIDE + j:CMP_STRIDE + j + 1, :],
                                  jnp.zeros((SUBLANES - 2, HEAD_DIM), F32)], axis=0).astype(BF16)
            acc = acc + jnp.dot(xj, wj_ref[j], preferred_element_type=F32)
        bias = acc[0:1] + pltpu.roll(acc, HEAD_DIM, 1)[1:2]
        slab = slice(g * LANES, (g + 1) * LANES)
        h = buf[0:c, slab] + pltpu.roll(buf[1:c + 1, slab], HEAD_DIM, 1) + bias
        out = jnp.dot(jax.nn.gelu(h).astype(BF16), w2_ref[...], preferred_element_type=F32)
        if add_pos:
            lane = lax.broadcasted_iota(jnp.int32, out.shape, 1)
            cend = lax.broadcasted_iota(jnp.int32, out.shape, 0) * CMP_STRIDE + (CMP_LEN - 1)
            hi = ((cend // 256) * 256).astype(F32)
            lo = (cend % 256).astype(F32)
            out = jnp.where((lane == _POS_LANE) | (lane == _POS_LANE + 1), hi,
                            jnp.where((lane == _POS_LANE + 2) | (lane == _POS_LANE + 3), lo, out))
        return out.astype(BF16)

    qa = _decode_queries(q_ref, slope_ref)
    for g in range(groups):
        ck = tokens(bufk, wjk_ref, pek_ref, w2k_ref, g, True)
        cv = tokens(bufv, wjv_ref, pev_ref, w2v_ref, g, False)
        s_c = _dot_nt(qa[g * rpg:(g + 1) * rpg], ck)
        cend = lax.broadcasted_iota(jnp.int32, s_c.shape, 1) * CMP_STRIDE + (CMP_LEN - 1)
        t_q = past + lax.broadcasted_iota(jnp.int32, s_c.shape, 0) % nq
        ok = cend <= t_q
        s_c = jnp.where(ok, s_c, NEG_BIG)
        e_c = jnp.where(ok, jnp.exp(s_c - jnp.max(s_c, axis=-1, keepdims=True)), 0.0)
        p_c = e_c / jnp.maximum(jnp.sum(e_c, axis=-1, keepdims=True), 1e-30)
        oc_ref[0, g * rpg:(g + 1) * rpg, :] = jnp.dot(p_c.astype(BF16), cv, preferred_element_type=F32)
        half = p_c[0:2 * nq]
        for u in range(1, rpg // (2 * nq)):
            half = half + p_c[2 * nq * u:2 * nq * (u + 1)]
        psum = half + pltpu.roll(half, nq, 0)
        imp = jnp.dot(psum, cover_ref[...], precision=HIGHEST, preferred_element_type=F32)
        lane = lax.broadcasted_iota(jnp.int32, imp.shape, 1)
        qb = (past + lax.broadcasted_iota(jnp.int32, imp.shape, 0) % nq) // SEL_BLOCK
        pen_ref[0, g] = jnp.where(_select_blocks(imp, qb, lane), 0.0, NEG_BIG)


def nsa_decode_cmp(q_rows, slope_rows, ab_k, ab_v, page_table, wjk, wjv, pek, pev, w2k, w2v, cover, nq):
    b, rows, _ = q_rows.shape
    n_pages = page_table.shape[1]
    past = n_pages * PAGE_SIZE
    groups = ab_k.shape[1] // LANES
    c = n_pages * _CHUNKS_PER_PAGE
    assert past % CMP_STRIDE == 0 and nq < CMP_STRIDE and 2 * nq == SUBLANES
    ab_k3 = ab_k.reshape(-1, _CHUNKS_PER_PAGE, groups * LANES)
    ab_v3 = ab_v.reshape(-1, _CHUNKS_PER_PAGE, groups * LANES)
    full = lambda a: pl.BlockSpec(a.shape, lambda s, pt: (0,) * a.ndim)
    grid_spec = pltpu.PrefetchScalarGridSpec(
        num_scalar_prefetch=1,
        grid=(b,),
        in_specs=[pl.BlockSpec((1, rows, LANES), lambda s, pt: (s, 0, 0)), full(slope_rows),
                  pl.BlockSpec(memory_space=pl.ANY), pl.BlockSpec(memory_space=pl.ANY),
                  full(wjk), full(wjv), full(pek), full(pev), full(w2k), full(w2v), full(cover)],
        out_specs=[pl.BlockSpec((1, rows, LANES), lambda s, pt: (s, 0, 0)),
                   pl.BlockSpec((1, groups, 2 * nq, cover.shape[1]), lambda s, pt: (s, 0, 0, 0))],
        scratch_shapes=[pltpu.VMEM((c + SUBLANES, groups * LANES), F32),
                        pltpu.VMEM((c + SUBLANES, groups * LANES), F32),
                        pltpu.SemaphoreType.DMA((2,))],
    )
    return pl.pallas_call(
        functools.partial(_nsa_decode_cmp_body, n_pages, past),
        grid_spec=grid_spec,
        out_shape=[jax.ShapeDtypeStruct((b, rows, LANES), F32),
                   jax.ShapeDtypeStruct((b, groups, 2 * nq, cover.shape[1]), F32)],
        compiler_params=_cparams(("arbitrary",)),
        name="nsa_decode_cmp",
    )(page_table, q_rows, slope_rows, ab_k3, ab_v3, wjk, wjv, pek, pev, w2k, w2v, cover)


_SEL_PAGES_PER_STEP = 4


def _nsa_decode_sel_body(n_steps, past, wbuf, pt_ref, q_ref, slope_ref, pen_ref, oc_ref, gt_ref, ksn_ref, vsn_ref,
                         kwn_ref, vwn_ref, wk_ref, wv_ref, *refs):
    del pt_ref
    pp = _SEL_PAGES_PER_STEP
    k_refs, v_refs = refs[:pp], refs[pp:2 * pp]
    o_ref, m_s, l_s, acc_s = refs[2 * pp:]
    step = pl.program_id(1)
    rows, width = q_ref.shape[1], q_ref.shape[2]
    groups = pen_ref.shape[1]
    rpg = rows // groups
    nq = pen_ref.shape[2] // 2
    n_sel = pen_ref.shape[3]
    own = _own_head_block(rows, width, rpg)
    qb = jnp.where(own, q_ref[0] * HEAD_DIM ** -0.5, 0.0).astype(BF16)
    slope = slope_ref[...]
    blocks_per_page = PAGE_SIZE // SEL_BLOCK
    pen_rows = jnp.concatenate([pen_ref[0, g] for g in range(groups) for _ in range(rpg // (2 * nq))],
                               axis=0).astype(BF16)

    @pl.when(step == 0)
    def _():
        m_s[...] = jnp.full(m_s.shape, -jnp.inf, F32)
        l_s[...] = jnp.zeros(l_s.shape, F32)
        acc_s[...] = jnp.zeros(acc_s.shape, F32)

    def attend(carry, kt, vt, bias, mask=None):
        m, l, acc = carry
        s = jnp.dot(qb, kt, preferred_element_type=F32) + bias
        if mask is not None:
            s = jnp.where(mask, s, NEG_BIG)
        m_new = jnp.maximum(m, jnp.max(s, axis=-1, keepdims=True))
        p = jnp.exp(s - m_new)
        alpha = jnp.exp(m - m_new)
        return m_new, alpha * l + jnp.sum(p, axis=-1, keepdims=True), alpha * acc + _dot_nt(p.astype(BF16), vt)

    e_row = lax.broadcasted_iota(jnp.int32, (n_sel, PAGE_SIZE), 0)
    e_col = lax.broadcasted_iota(jnp.int32, (n_sel, PAGE_SIZE), 1)
    col = lax.broadcasted_iota(jnp.int32, (rows, PAGE_SIZE), 1)
    col_f = col.astype(F32)
    carry = (m_s[...], l_s[...], acc_s[...])
    for u in range(pp):
        page = step * pp + u
        expand = jnp.where(e_row == page * blocks_per_page + e_col // SEL_BLOCK, 1.0, 0.0).astype(BF16)
        bias = (jnp.dot(pen_rows, expand, preferred_element_type=F32)
                + slope * (col_f + (page * PAGE_SIZE - past).astype(F32)))
        carry = attend(carry, k_refs[u][0, 0].reshape(width, PAGE_SIZE).astype(BF16),
                       v_refs[u][0, 0].reshape(width, PAGE_SIZE).astype(BF16), bias)
    m_s[...], l_s[...], acc_s[...] = carry

    @pl.when(step == n_steps - 1)
    def _():
        gates = jax.nn.sigmoid(gt_ref[0])
        qi = lax.broadcasted_iota(jnp.int32, (rows, PAGE_SIZE), 0) % nq
        ok_n = (col <= qi) & (col < nq)
        pen_new = pen_rows[:, past // SEL_BLOCK:past // SEL_BLOCK + 1].astype(F32)
        _, l_sel, acc_sel = attend((m_s[...], l_s[...], acc_s[...]), ksn_ref[0].astype(BF16),
                                   vsn_ref[0].astype(BF16), slope * col_f + pen_new, ok_n)
        o_sel = _fold_heads(acc_sel, own) / l_sel
        wqi = lax.broadcasted_iota(jnp.int32, (rows, wbuf), 0) % nq
        wc = lax.broadcasted_iota(jnp.int32, (rows, wbuf), 1)
        dist_w = wqi + wbuf - wc
        ok_w = (dist_w >= 0) & (dist_w < WINDOW) & (past - wbuf + wc >= 0)
        init = (jnp.full((rows, 1), -jnp.inf, F32), jnp.zeros((rows, 1), F32), jnp.zeros((rows, width), F32))
        carry_w = attend(init, wk_ref[0, 0].reshape(width, wbuf).astype(BF16),
                         wv_ref[0, 0].reshape(width, wbuf).astype(BF16), slope * (wc - wbuf).astype(F32), ok_w)
        _, l_win, acc_win = attend(carry_w, kwn_ref[0].astype(BF16), vwn_ref[0].astype(BF16), slope * col_f, ok_n)
        o_win = _fold_heads(acc_win, own) / l_win
        o_ref[0] = (gates[:, 0:1] * oc_ref[0, :, 0:HEAD_DIM] + gates[:, 1:2] * o_sel + gates[:, 2:3] * o_win)


def nsa_decode_sel(q_tiled, slope_rows, pen, o_c, gt_rows, ks_new, vs_new, kw_new, vw_new, win_k, win_v,
                   cache_k, cache_v, page_table, layer):
    b, rows, _ = q_tiled.shape
    groups = cache_k.shape[2]
    n_pages = page_table.shape[1]
    past = n_pages * PAGE_SIZE
    pp = _SEL_PAGES_PER_STEP
    wbuf = win_k.shape[4]
    assert n_pages % pp == 0 and past % SEL_BLOCK == 0
    n_steps = n_pages // pp
    per_seq = lambda a: pl.BlockSpec((1,) + a.shape[1:], lambda s, j, pt: (s,) + (0,) * (a.ndim - 1))
    page = lambda u: pl.BlockSpec((1, 1, groups, HEAD_DIM, PAGE_SIZE),
                                  lambda s, j, pt: (layer, pt[s, pp * j + u], 0, 0, 0))
    win = pl.BlockSpec((1, 1, groups, HEAD_DIM, wbuf), lambda s, j, pt: (layer, s, 0, 0, 0))
    grid_spec = pltpu.PrefetchScalarGridSpec(
        num_scalar_prefetch=1,
        grid=(b, n_steps),
        in_specs=[per_seq(q_tiled), pl.BlockSpec(slope_rows.shape, lambda s, j, pt: (0, 0)),
                  per_seq(pen), per_seq(o_c), per_seq(gt_rows),
                  per_seq(ks_new), per_seq(vs_new), per_seq(kw_new), per_seq(vw_new), win, win]
        + [page(u) for u in range(pp)] * 2,
        out_specs=pl.BlockSpec((1, rows, HEAD_DIM), lambda s, j, pt: (s, 0, 0)),
        scratch_shapes=[pltpu.VMEM((rows, 1), F32), pltpu.VMEM((rows, 1), F32),
                        pltpu.VMEM((rows, groups * HEAD_DIM), F32)],
    )
    return pl.pallas_call(
        functools.partial(_nsa_decode_sel_body, n_steps, past, wbuf),
        grid_spec=grid_spec,
        out_shape=jax.ShapeDtypeStruct((b, rows, HEAD_DIM), F32),
        compiler_params=_cparams(("parallel", "arbitrary")),
        name="nsa_decode_sel",
    )(page_table, q_tiled, slope_rows, pen, o_c, gt_rows, ks_new, vs_new, kw_new, vw_new, win_k, win_v,
      *([cache_k] * pp), *([cache_v] * pp))


def _heads_to_rows(x, n, t, heads):
    return x.reshape(n, t, heads, HEAD_DIM).transpose(0, 2, 1, 3).reshape(n, heads * t, HEAD_DIM)


def _rows_to_heads(x, n, t, heads):
    return x.reshape(n, heads, t, HEAD_DIM).transpose(0, 2, 1, 3).reshape(n * t, heads * HEAD_DIM)


def _new_kv_cols(x, n, t):
    return jnp.pad(x.reshape(n, t, x.shape[-1]).transpose(0, 2, 1), ((0, 0), (0, 0), (0, PAGE_SIZE - t)))


def _tokens_minor(a):
    return a.transpose(0, 1, 3, 4, 2)


def _layer_ab(layer, l, yp, ys, dims, p):
    bp, tp, bs, ts = dims
    d_rnn = p["conv_w"].shape[2]
    heads = (p["w_in_ab"].shape[2] - 2 * d_rnn) // (3 * HEAD_DIM)
    hw = heads * HEAD_DIM
    w_in = p["w_in_ab"][l].astype(BF16)
    cuts = [0, d_rnn, 2 * d_rnn, 2 * d_rnn + hw, 2 * d_rnn + 2 * hw, 2 * d_rnn + 3 * hw]
    ws = [w_in[:, a:b] for a, b in zip(cuts[:-1], cuts[1:])]
    w_out = p["w_out_ab"][l].astype(BF16)
    w_out = [w_out[:d_rnn], w_out[d_rnn:]]
    rg = (p["conv_w"][l], p["conv_b"][l], _block_diag(p["w_rg_a"][l]).astype(BF16), p["b_rg_a"][l],
          _block_diag(p["w_rg_i"][l]).astype(BF16), p["b_rg_i"][l], p["rg_lambda"][l])
    slopes = _alibi_slopes(heads)
    g = p["norm_mix"][layer]
    f32x5 = [(i, F32) for i in range(5)]

    xr, gr, q, k, v, kb, vb = norm_proj(yp, g, ws, f32x5 + [(3, BF16), (4, BF16)])
    seq = lambda a: a.reshape(bp, tp, a.shape[-1])
    r_p, conv_p, h_p = rglru(seq(xr), seq(gr), jnp.zeros((bp, CONV_W - 1, d_rnn), F32),
                             jnp.zeros((bp, d_rnn), F32), *rg)
    att = moba_prompt(seq(q), seq(kb), seq(vb), block_mean(seq(k), MOBA_BLOCK), slopes)
    yp = proj_residual([r_p.reshape(bp * tp, d_rnn), att.reshape(bp * tp, hw)], w_out, yp)
    out_p = (k.reshape(bp, tp, heads, HEAD_DIM), v.reshape(bp, tp, heads, HEAD_DIM), conv_p, h_p)

    xr, gr, q, k, v = norm_proj(ys, g, ws, f32x5)
    seq = lambda a: a.reshape(bs, ts, a.shape[-1])
    r_s, conv_s, h_s = rglru(seq(xr), seq(gr), p["state_conv"][l], p["state_rglru"][l], *rg)
    att = moba_decode(jnp.tile(_heads_to_rows(q, bs, ts, heads), (1, 1, heads)), _new_kv_cols(k, bs, ts),
                      _new_kv_cols(v, bs, ts), _tokens_minor(p["cache_moba_k"]), _tokens_minor(p["cache_moba_v"]),
                      p["page_table"], l, slopes)
    ys = proj_residual([r_s.reshape(bs * ts, d_rnn), _rows_to_heads(att, bs, ts, heads)], w_out, ys)
    out_s = (k.reshape(bs, ts, heads, HEAD_DIM), v.reshape(bs, ts, heads, HEAD_DIM), conv_s, h_s)
    return yp, ys, out_p, out_s


def _layer_c(layer, l, yp, ys, dims, p):
    bp, tp, bs, ts = dims
    groups = p["cache_cmp_k"].shape[3]
    gw = groups * HEAD_DIM
    w_in = p["w_in_c"][l]
    heads = (w_in.shape[1] - 6 * gw) // (HEAD_DIM + 3)
    qw = heads * HEAD_DIM
    cuts = [0, qw] + [qw + (i + 1) * gw for i in range(6)]
    w_f32 = [w_in[:, a:b] for a, b in zip(cuts[:-1], cuts[1:])]
    w_gt = w_in[:, cuts[-1]:]
    slopes = _alibi_slopes(heads)
    g = p["norm_mix"][layer]
    w_out = [p["w_out_c"][l].astype(BF16)]
    cmp_k = (p["w_cmp_k1"][l], p["w_cmp_k2"][l], p["pe_cmp_k"][l])
    cmp_v = (p["w_cmp_v1"][l], p["w_cmp_v2"][l], p["pe_cmp_v"][l])
    kv4 = lambda a, n, t: a.reshape(n, t, groups, HEAD_DIM)

    ws = ([w.astype(BF16) for w in w_f32] + [_pad_group_cols(w_gt, 3 * heads // groups).astype(BF16)]
          + [_pad_group_cols(w, HEAD_DIM).astype(BF16) for w in w_f32[3:7]])
    outs = [(i, F32) for i in range(8)] + [(i, BF16) for i in range(8, 12)]
    q, kc, vc, ks, vs, kw, vw, gt, ksp, vsp, kwp, vwp = norm_proj(yp, g, ws, outs)
    seq = lambda a: a.reshape(bp, tp, a.shape[-1])
    ck = nsa_compress(seq(kc), *_compress_weights(*cmp_k, groups), True)
    cv = nsa_compress(seq(vc), *_compress_weights(*cmp_v, groups), False)
    cover = _nsa_cover(tp // CMP_STRIDE, max(-(-tp // SEL_BLOCK), SEL_TOPK))
    att = nsa_prompt(seq(q), seq(gt), ck, cv, cover, seq(ksp), seq(vsp), seq(kwp), seq(vwp), slopes)
    yp = proj_residual([att.reshape(bp * tp, qw)], w_out, yp)
    w_keep = min(WINDOW, tp)
    out_p = tuple(kv4(a, bp, tp) for a in (kc, vc, ks, vs)) + tuple(kv4(a, bp, tp)[:, tp - w_keep:] for a in (kw, vw))

    ws = [w.astype(BF16) for w in w_f32] + [jnp.pad(w_gt, ((0, 0), (0, LANES - w_gt.shape[1]))).astype(BF16)]
    q, kc, vc, ks, vs, kw, vw, gt = norm_proj(ys, g, ws, [(i, F32) for i in range(8)])
    q_rows = _heads_to_rows(q, bs, ts, heads)
    q_tiled = jnp.tile(q_rows, (1, 1, groups))
    q_rows = jnp.pad(q_rows, ((0, 0), (0, 0), (0, LANES - HEAD_DIM)))
    slope_rows = jnp.asarray(np.repeat(slopes, ts).reshape(heads * ts, 1))
    gt_rows = gt[:, :3 * heads].reshape(bs, ts, heads, 3).transpose(0, 2, 1, 3).reshape(bs, heads * ts, 3)
    n_pages = p["page_table"].shape[1]
    past = n_pages * PAGE_SIZE
    cover = _nsa_cover(n_pages * _CHUNKS_PER_PAGE, max(-(-(past + ts) // SEL_BLOCK), SEL_TOPK))
    dec_w = []
    for w1, w2, pe in (cmp_k, cmp_v):
        w1j = w1.reshape(CMP_LEN, HEAD_DIM, HEAD_DIM)
        dec_w.append((jnp.concatenate([w1j[:CMP_STRIDE], w1j[CMP_STRIDE:]], axis=2).astype(BF16), pe,
                      jnp.pad(w2, ((0, LANES - HEAD_DIM), (0, LANES - HEAD_DIM))).astype(BF16)))
    (wjk, pek, w2k), (wjv, pev, w2v) = dec_w
    ab_k = nsa_page_ab(_tokens_minor(p["cache_cmp_k"]), l, wjk)
    ab_v = nsa_page_ab(_tokens_minor(p["cache_cmp_v"]), l, wjv)
    o_c, pen = nsa_decode_cmp(q_rows, slope_rows, ab_k, ab_v, p["page_table"], wjk, wjv, pek, pev, w2k, w2v,
                              cover, ts)
    att = nsa_decode_sel(q_tiled, slope_rows, pen, o_c, gt_rows, _new_kv_cols(ks, bs, ts),
                         _new_kv_cols(vs, bs, ts), _new_kv_cols(kw, bs, ts), _new_kv_cols(vw, bs, ts),
                         _tokens_minor(p["state_win_k"]), _tokens_minor(p["state_win_v"]),
                         _tokens_minor(p["cache_sel_k"]), _tokens_minor(p["cache_sel_v"]), p["page_table"], l)
    ys = proj_residual([_rows_to_heads(att, bs, ts, heads)], w_out, ys)
    w_buf = p["state_win_k"].shape[2]
    win = lambda state, new: jnp.concatenate([state[l], kv4(new, bs, ts)], axis=1)[:, -w_buf:]
    out_s = tuple(kv4(a, bs, ts) for a in (kc, vc, ks, vs)) + (win(p["state_win_k"], kw), win(p["state_win_v"], vw))
    return yp, ys, out_p, out_s


def kernel(x_prompt, x_sample, cache_moba_k, cache_moba_v, cache_cmp_k, cache_cmp_v, cache_sel_k, cache_sel_v,
           state_win_k, state_win_v, state_conv, state_rglru, page_table, norm_mix, norm_ffn, norm_final,
           w_in_ab, w_out_ab, conv_w, conv_b, w_rg_a, b_rg_a, w_rg_i, b_rg_i, rg_lambda, w_in_c, w_out_c,
           w_cmp_k1, w_cmp_k2, pe_cmp_k, w_cmp_v1, w_cmp_v2, pe_cmp_v, w_route_group, b_route_group,
           w_route_expert, b_route_expert, w_exp_gate, w_exp_up, w_exp_down):
    p = dict(cache_moba_k=cache_moba_k, cache_moba_v=cache_moba_v, cache_cmp_k=cache_cmp_k,
             cache_cmp_v=cache_cmp_v, cache_sel_k=cache_sel_k, cache_sel_v=cache_sel_v,
             state_win_k=state_win_k, state_win_v=state_win_v, state_conv=state_conv, state_rglru=state_rglru,
             page_table=page_table, norm_mix=norm_mix, w_in_ab=w_in_ab, w_out_ab=w_out_ab, conv_w=conv_w,
             conv_b=conv_b, w_rg_a=w_rg_a, b_rg_a=b_rg_a, w_rg_i=w_rg_i, b_rg_i=b_rg_i, rg_lambda=rg_lambda,
             w_in_c=w_in_c, w_out_c=w_out_c, w_cmp_k1=w_cmp_k1, w_cmp_k2=w_cmp_k2, pe_cmp_k=pe_cmp_k,
             w_cmp_v1=w_cmp_v1, w_cmp_v2=w_cmp_v2, pe_cmp_v=pe_cmp_v)
    bp, tp, d = x_prompt.shape
    bs, ts, _ = x_sample.shape
    dims = (bp, tp, bs, ts)
    yp = x_prompt.reshape(bp * tp, d)
    ys = x_sample.reshape(bs * ts, d)
    ab_p, ab_s, c_p, c_s = [], [], [], []
    for layer in range(norm_mix.shape[0]):
        l = layer // 2
        if layer % 2 == 0:
            yp, ys, o_p, o_s = _layer_ab(layer, l, yp, ys, dims, p)
            ab_p.append(o_p)
            ab_s.append(o_s)
        else:
            yp, ys, o_p, o_s = _layer_c(layer, l, yp, ys, dims, p)
            c_p.append(o_p)
            c_s.append(o_s)
        wr, br = _router_weights(w_route_group[layer], b_route_group[layer], w_route_expert[layer],
                                 b_route_expert[layer])
        moe_w = (w_exp_gate[layer].astype(BF16), w_exp_up[layer].astype(BF16), w_exp_down[layer].astype(BF16))
        yp = hier_moe_residual(yp, norm_ffn[layer], wr, br, *moe_w)
        ys = hier_moe_residual(ys, norm_ffn[layer], wr, br, *moe_w)
    y_prompt = rmsnorm(yp, norm_final).reshape(bp, tp, d)
    y_sample = rmsnorm(ys, norm_final).reshape(bs, ts, d)
    stack = lambda outs, i: jnp.stack([o[i] for o in outs])
    res = [y_prompt, y_sample]
    for i in range(4):
        res += [stack(ab_p, i), stack(ab_s, i)]
    for i in range(6):
        res += [stack(c_p, i), stack(c_s, i)]
    return tuple(res)
```

```python
import functools

import numpy as np
import jax
import jax.numpy as jnp
from jax import lax
from jax.experimental import pallas as pl
from jax.experimental.pallas import tpu as pltpu

F32 = jnp.float32
BF16 = jnp.bfloat16
HIGHEST = lax.Precision.HIGHEST

HEAD_DIM = 64
LANES = 128
SUBLANES = 8
CONV_W = 4
RG_C = 8.0
MOBA_BLOCK = 256
MOBA_TOPK = 3
CMP_STRIDE = 16
CMP_LEN = 32
SEL_BLOCK = 64
SEL_TOPK = 16
WINDOW = 512
N_GROUPS = 4
EXPERTS_PER_GROUP = 8
N_EXPERTS = N_GROUPS * EXPERTS_PER_GROUP
PAGE_SIZE = 128
RMS_EPS = 1e-6
NEG_BIG = -1e30
VMEM_LIMIT = 56 * 1024 * 1024
MOE_BLOCK_ROWS = 256


def _cparams(sem):
    return pltpu.CompilerParams(dimension_semantics=sem, vmem_limit_bytes=VMEM_LIMIT)


def _alibi_slopes(n):
    return np.asarray(2.0 ** (-8.0 * np.arange(1, n + 1) / n), dtype=np.float32)


def _row_tile(m, pref=512):
    return pref if m % pref == 0 else m


def _dot_nt(a, b, precision=None):
    return lax.dot_general(a, b, (((1,), (1,)), ((), ())), precision=precision,
                           preferred_element_type=F32)


def _norm_proj_body(out_map, x_ref, g_ref, *refs):
    n_w = max(out_map) + 1
    w_refs, o_refs = refs[:n_w], refs[n_w:]
    x = x_ref[...]
    y = x * lax.rsqrt(jnp.mean(x * x, axis=-1, keepdims=True) + RMS_EPS) * g_ref[...]
    yb = y.astype(BF16)
    prods = [jnp.dot(yb, w_ref[...], preferred_element_type=F32) for w_ref in w_refs]
    for wi, o_ref in zip(out_map, o_refs):
        o_ref[...] = prods[wi].astype(o_ref.dtype)


def norm_proj(x, g, ws, outs):
    m, d = x.shape
    tm = _row_tile(m)
    out_map = tuple(wi for wi, _ in outs)
    return pl.pallas_call(
        functools.partial(_norm_proj_body, out_map),
        grid=(m // tm,),
        in_specs=[pl.BlockSpec((tm, d), lambda i: (i, 0)), pl.BlockSpec((1, d), lambda i: (0, 0))]
        + [pl.BlockSpec(w.shape, lambda i: (0, 0)) for w in ws],
        out_specs=[pl.BlockSpec((tm, ws[wi].shape[1]), lambda i: (i, 0)) for wi, _ in outs],
        out_shape=[jax.ShapeDtypeStruct((m, ws[wi].shape[1]), dt) for wi, dt in outs],
        compiler_params=_cparams(("parallel",)),
        name="norm_proj",
    )(x, g.reshape(1, d), *ws)


def _proj_res_body(n_in, *refs):
    x_refs, w_refs = refs[:n_in], refs[n_in:2 * n_in]
    res_ref, o_ref = refs[2 * n_in], refs[2 * n_in + 1]
    acc = jnp.dot(x_refs[0][...].astype(BF16), w_refs[0][...], preferred_element_type=F32)
    for x_ref, w_ref in zip(x_refs[1:], w_refs[1:]):
        acc = acc + jnp.dot(x_ref[...].astype(BF16), w_ref[...], preferred_element_type=F32)
    o_ref[...] = res_ref[...] + acc


def proj_residual(xs, ws, res):
    m, d = res.shape
    tm = _row_tile(m)
    return pl.pallas_call(
        functools.partial(_proj_res_body, len(xs)),
        grid=(m // tm,),
        in_specs=[pl.BlockSpec((tm, x.shape[1]), lambda i: (i, 0)) for x in xs]
        + [pl.BlockSpec(w.shape, lambda i: (0, 0)) for w in ws]
        + [pl.BlockSpec((tm, d), lambda i: (i, 0))],
        out_specs=pl.BlockSpec((tm, d), lambda i: (i, 0)),
        out_shape=jax.ShapeDtypeStruct((m, d), F32),
        compiler_params=_cparams(("parallel",)),
        name="proj_residual",
    )(*xs, *ws, res)


def _rmsnorm_body(x_ref, g_ref, o_ref):
    x = x_ref[...]
    o_ref[...] = x * lax.rsqrt(jnp.mean(x * x, axis=-1, keepdims=True) + RMS_EPS) * g_ref[...]


def rmsnorm(x, g):
    m, d = x.shape
    tm = _row_tile(m)
    return pl.pallas_call(
        _rmsnorm_body,
        grid=(m // tm,),
        in_specs=[pl.BlockSpec((tm, d), lambda i: (i, 0)), pl.BlockSpec((1, d), lambda i: (0, 0))],
        out_specs=pl.BlockSpec((tm, d), lambda i: (i, 0)),
        out_shape=jax.ShapeDtypeStruct((m, d), F32),
        compiler_params=_cparams(("parallel",)),
        name="rmsnorm",
    )(x, g.reshape(1, d))


_XIN_PAD = 8


def _rglru_body(xr_ref, gr_ref, cb_ref, h0_ref, cw_ref, cbias_ref, wa_ref, ba_ref, wi_ref, bi_ref,
                lam_ref, y_ref, tail_ref, hl_ref, xin_s, a_s, b_s, h_s):
    tt = xr_ref.shape[1]
    keep = CONV_W - 1

    @pl.when(pl.program_id(1) == 0)
    def _():
        xin_s[_XIN_PAD - keep:_XIN_PAD, :] = cb_ref[0]
        h_s[...] = h0_ref[0]

    xin_s[_XIN_PAD:_XIN_PAD + tt, :] = xr_ref[0]
    xc = cbias_ref[...]
    for k in range(CONV_W):
        lo = _XIN_PAD - keep + k
        xc = xc + xin_s[lo:lo + tt, :] * cw_ref[k:k + 1, :]
    new_tail = xin_s[_XIN_PAD + tt - keep:_XIN_PAD + tt, :]
    xin_s[_XIN_PAD - keep:_XIN_PAD, :] = new_tail
    tail_ref[0] = new_tail

    xcb = xc.astype(BF16)
    r = jax.nn.sigmoid(jnp.dot(xcb, wa_ref[...], preferred_element_type=F32) + ba_ref[...])
    gi = jax.nn.sigmoid(jnp.dot(xcb, wi_ref[...], preferred_element_type=F32) + bi_ref[...])
    nl = -lam_ref[...]
    softplus = jnp.maximum(nl, 0.0) + jnp.log1p(jnp.exp(-jnp.abs(nl)))
    log_a = (-RG_C) * r * softplus
    a = jnp.exp(log_a)
    a_s[...] = a
    b_s[...] = jnp.sqrt(1.0 - a * a) * (gi * xc)

    def step(t, h):
        h = a_s[pl.ds(t, 1), :] * h + b_s[pl.ds(t, 1), :]
        b_s[pl.ds(t, 1), :] = h
        return h

    h = lax.fori_loop(0, tt, step, h_s[...], unroll=min(tt, 8))
    h_s[...] = h
    hl_ref[0] = h
    y_ref[0] = b_s[...] * jax.nn.gelu(gr_ref[0])


def rglru(xr, gr, conv_buf, h0, conv_w, conv_b, wa_bd, b_a, wi_bd, b_i, lam):
    n, t, w = xr.shape
    tt = min(t, 512)
    assert t % tt == 0 and tt >= CONV_W - 1
    row = lambda v: v.reshape(1, w)
    full = lambda shp: pl.BlockSpec(shp, lambda b, s: (0,) * len(shp))
    y, tail, hl = pl.pallas_call(
        _rglru_body,
        grid=(n, t // tt),
        in_specs=[pl.BlockSpec((1, tt, w), lambda b, s: (b, s, 0)),
                  pl.BlockSpec((1, tt, w), lambda b, s: (b, s, 0)),
                  pl.BlockSpec((1, CONV_W - 1, w), lambda b, s: (b, 0, 0)),
                  pl.BlockSpec((1, 1, w), lambda b, s: (b, 0, 0)),
                  full((CONV_W, w)), full((1, w)), full((w, w)), full((1, w)), full((w, w)),
                  full((1, w)), full((1, w))],
        out_specs=[pl.BlockSpec((1, tt, w), lambda b, s: (b, s, 0)),
                   pl.BlockSpec((1, CONV_W - 1, w), lambda b, s: (b, 0, 0)),
                   pl.BlockSpec((1, 1, w), lambda b, s: (b, 0, 0))],
        out_shape=[jax.ShapeDtypeStruct((n, t, w), F32),
                   jax.ShapeDtypeStruct((n, CONV_W - 1, w), F32),
                   jax.ShapeDtypeStruct((n, 1, w), F32)],
        scratch_shapes=[pltpu.VMEM((_XIN_PAD + tt, w), F32), pltpu.VMEM((tt, w), F32),
                        pltpu.VMEM((tt, w), F32), pltpu.VMEM((1, w), F32)],
        compiler_params=_cparams(("parallel", "arbitrary")),
        name="rglru",
    )(xr, gr, conv_buf, h0.reshape(n, 1, w), conv_w, row(conv_b), wa_bd, row(b_a), wi_bd, row(b_i), row(lam))
    return y, tail, hl.reshape(n, w)


def _block_diag(w):
    h, d, _ = w.shape
    eye = jnp.eye(h, dtype=w.dtype)
    return (eye[:, None, :, None] * w[:, :, None, :]).reshape(h * d, h * d)


def _block_mean_body(blk, k_ref, o_ref):
    x = k_ref[0]
    nb = x.shape[0] // blk
    o_ref[0] = jnp.mean(x.reshape(nb, blk, x.shape[1]), axis=1)


def block_mean(k, blk):
    n, t, w = k.shape
    nb = t // blk
    g = SUBLANES if nb % SUBLANES == 0 else nb
    return pl.pallas_call(
        functools.partial(_block_mean_body, blk),
        grid=(n, nb // g),
        in_specs=[pl.BlockSpec((1, g * blk, w), lambda b, s: (b, s, 0))],
        out_specs=pl.BlockSpec((1, g, w), lambda b, s: (b, s, 0)),
        out_shape=jax.ShapeDtypeStruct((n, nb, w), F32),
        compiler_params=_cparams(("parallel", "parallel")),
        name="block_mean",
    )(k)


def _first_argmax(g, idx_f):
    m = jnp.max(g, axis=-1, keepdims=True)
    return jnp.min(jnp.where(g == m, idx_f, jnp.inf), axis=-1, keepdims=True)


def _split_hi_lo(x):
    hi = x.astype(BF16)
    lo = (x - hi.astype(F32)).astype(BF16)
    return hi, lo


_CHUNK_ROWS = 128


def _row_chunks(x):
    return [x[r:r + _CHUNK_ROWS] for r in range(0, x.shape[0], _CHUNK_ROWS)]


def _flash_init(rows):
    return [(jnp.full((_CHUNK_ROWS, 1), -jnp.inf, F32), jnp.zeros((_CHUNK_ROWS, LANES), F32))
            for _ in range(rows // _CHUNK_ROWS)]


def _causal_chunk_masks(tq, tk):
    r = lax.broadcasted_iota(jnp.int32, (_CHUNK_ROWS, tk), 0)
    c = lax.broadcasted_iota(jnp.int32, (_CHUNK_ROWS, tk), 1)
    return [c <= r + off for off in range(0, tq, _CHUNK_ROWS)]


LOG2E = 1.4426950408889634
_QSCALE = HEAD_DIM ** -0.5 * LOG2E


def _flash_update(s_all, v_ones, carry, masks=None):
    stats, ps = [], []
    for ci, (m, _) in enumerate(carry):
        s = s_all[ci * _CHUNK_ROWS:(ci + 1) * _CHUNK_ROWS]
        if masks is not None:
            s = jnp.where(masks[ci % len(masks)], s, NEG_BIG)
        m_new = jnp.maximum(m, jnp.max(s, axis=-1, keepdims=True))
        ps.append(jnp.exp2(s - m_new).astype(BF16))
        stats.append((m_new, jnp.exp2(m - m_new)))
    pv_all = jnp.dot(jnp.concatenate(ps, axis=0), v_ones, preferred_element_type=F32)
    return [(m_new, alpha * acc + pv_all[ci * _CHUNK_ROWS:(ci + 1) * _CHUNK_ROWS])
            for ci, ((m_new, alpha), (_, acc)) in enumerate(zip(stats, carry))]


def _moba_prompt_body(nbk, slopes_ref, q_ref, km_ref, kb_ref, vb_ref, o_ref):
    p_idx = pl.program_id(1)
    i = pl.program_id(2)
    tq = q_ref.shape[1]
    q = q_ref[0]
    lane = lax.broadcasted_iota(jnp.int32, (tq, LANES), 1)
    lane_f = lane.astype(F32)
    km = km_ref[0]
    halves = []
    for hh in range(2):
        in_head = (lane >= hh * HEAD_DIM) & (lane < (hh + 1) * HEAD_DIM)
        qh = jnp.where(in_head, q, 0.0)
        gate = _dot_nt(qh, km, HIGHEST)
        g = jnp.where(lane < i, gate, -jnp.inf)
        sel = lane == i
        for _ in range(MOBA_TOPK):
            pick = lane_f == _first_argmax(g, lane_f)
            sel = sel | (pick & (lane < i))
            g = jnp.where(pick, -jnp.inf, g)
        slope = slopes_ref[2 * p_idx + hh]
        pb = jnp.where(sel, 0.0, NEG_BIG) + slope * (lane_f * MOBA_BLOCK)
        pb_hi, pb_lo = _split_hi_lo(pb)
        pb_lo = pltpu.roll(pb_lo.astype(F32), nbk, 1)
        s_hi, s_lo = _split_hi_lo(jnp.full((tq, LANES), slope, F32))
        aug = jnp.where(lane < nbk, pb_hi.astype(F32),
                        jnp.where(lane < 2 * nbk, pb_lo,
                                  jnp.where(lane == 2 * nbk, s_hi.astype(F32),
                                            jnp.where(lane == 2 * nbk + 1, s_lo.astype(F32), 0.0))))
        halves.append(jnp.concatenate([(qh * _QSCALE).astype(BF16), aug.astype(BF16)], axis=1))
    qa = jnp.concatenate(halves, axis=0)

    klane = lax.broadcasted_iota(jnp.int32, (MOBA_BLOCK, LANES), 1)
    krow = lax.broadcasted_iota(jnp.int32, (MOBA_BLOCK, LANES), 0).astype(F32)
    ones_lane = (HEAD_DIM, 0)
    causal_masks = _causal_chunk_masks(tq, MOBA_BLOCK)

    def scores(j):
        kt = kb_ref[0, pl.ds(pl.multiple_of(j * MOBA_BLOCK, MOBA_BLOCK), MOBA_BLOCK), :]
        kaug = jnp.where((klane == j) | (klane == nbk + j), 1.0,
                         jnp.where((klane == 2 * nbk) | (klane == 2 * nbk + 1), krow, 0.0)).astype(BF16)
        return _dot_nt(qa, jnp.concatenate([kt, kaug], axis=1))

    def update(j, s, carry, causal):
        vt = vb_ref[0, pl.ds(pl.multiple_of(j * MOBA_BLOCK, MOBA_BLOCK), MOBA_BLOCK), :]
        new = []
        for hh in range(2):
            in_head = (klane >= hh * HEAD_DIM) & (klane < (hh + 1) * HEAD_DIM)
            v_h = jnp.where(in_head, vt, jnp.where(klane == ones_lane[hh], 1.0, 0.0).astype(BF16))
            new.append(_flash_update(s[hh * tq:(hh + 1) * tq], v_h, carry[hh], causal_masks if causal else None))
        return new

    def body(j, c):
        s, carry = c
        return scores(j + 1), update(j, s, carry, False)

    s, carry = lax.fori_loop(0, i, body, (scores(0), [_flash_init(tq), _flash_init(tq)]))
    carry = update(i, s, carry, True)
    outs = [jnp.concatenate([acc / acc[:, ones_lane[hh]:ones_lane[hh] + 1] for _, acc in carry[hh]], axis=0)
            for hh in range(2)]
    o_ref[0] = jnp.where(lane < HEAD_DIM, outs[0], outs[1])


def moba_prompt(q, kb, vb, kmean, slopes):
    n, t, w = q.shape
    nbk = t // MOBA_BLOCK
    assert t % MOBA_BLOCK == 0 and nbk >= MOBA_TOPK and 2 * nbk + 2 <= LANES
    km = jnp.pad(kmean, ((0, 0), (0, LANES - nbk), (0, 0)))
    grid_spec = pltpu.PrefetchScalarGridSpec(
        num_scalar_prefetch=0,
        grid=(n, w // LANES, nbk),
        in_specs=[pl.BlockSpec(memory_space=pltpu.SMEM),
                  pl.BlockSpec((1, MOBA_BLOCK, LANES), lambda b, p, i: (b, i, p)),
                  pl.BlockSpec((1, LANES, LANES), lambda b, p, i: (b, 0, p)),
                  pl.BlockSpec((1, t, LANES), lambda b, p, i: (b, 0, p)),
                  pl.BlockSpec((1, t, LANES), lambda b, p, i: (b, 0, p))],
        out_specs=pl.BlockSpec((1, MOBA_BLOCK, LANES), lambda b, p, i: (b, i, p)),
    )
    return pl.pallas_call(
        functools.partial(_moba_prompt_body, nbk),
        grid_spec=grid_spec,
        out_shape=jax.ShapeDtypeStruct((n, t, w), F32),
        compiler_params=_cparams(("parallel", "parallel", "arbitrary")),
        name="moba_prompt",
    )(jnp.asarray(slopes * np.float32(LOG2E)), q, km, kb, vb)


_ROUTE_E, _ROUTE_W, _ROUTE_RANK = 0, 2, 4


def _router_body(x_ref, g_ref, wr_ref, br_ref, xn_ref, route_ref, cnt_ref, carry_s):
    tm = x_ref.shape[0]

    @pl.when(pl.program_id(0) == 0)
    def _():
        carry_s[...] = jnp.zeros_like(carry_s)

    x = x_ref[...]
    xn = x * lax.rsqrt(jnp.mean(x * x, axis=-1, keepdims=True) + RMS_EPS) * g_ref[...]
    xn_ref[...] = xn
    logits = jnp.dot(xn, wr_ref[...], precision=HIGHEST, preferred_element_type=F32) + br_ref[...]
    lane = lax.broadcasted_iota(jnp.int32, (tm, LANES), 1)
    is_g = lane < N_GROUPS
    gl = jnp.where(is_g, logits, -jnp.inf)
    gmax = jnp.max(gl, axis=-1, keepdims=True)
    g_idx = jnp.min(jnp.where(gl == gmax, lane, LANES), axis=-1, keepdims=True)
    g_w = 1.0 / jnp.sum(jnp.where(is_g, jnp.exp(gl - gmax), 0.0), axis=-1, keepdims=True)
    e_lo = N_GROUPS + g_idx * EXPERTS_PER_GROUP
    el = jnp.where((lane >= e_lo) & (lane < e_lo + EXPERTS_PER_GROUP), logits, -jnp.inf)
    v1 = jnp.max(el, axis=-1, keepdims=True)
    i1 = jnp.min(jnp.where(el == v1, lane, LANES), axis=-1, keepdims=True)
    el2 = jnp.where(lane == i1, -jnp.inf, el)
    v2 = jnp.max(el2, axis=-1, keepdims=True)
    i2 = jnp.min(jnp.where(el2 == v2, lane, LANES), axis=-1, keepdims=True)
    e21 = jnp.exp(v2 - v1)
    p1 = 1.0 / (1.0 + e21)
    w1 = g_w * p1
    w2 = g_w * (e21 * p1)
    hot1 = lane == i1
    hot2 = lane == i2
    cnt = jnp.where(hot1 | hot2, 1.0, 0.0)
    r = lax.broadcasted_iota(jnp.int32, (tm, tm), 0)
    c = lax.broadcasted_iota(jnp.int32, (tm, tm), 1)
    tri = jnp.where(c < r, 1.0, 0.0).astype(BF16)
    before = jnp.dot(tri, cnt.astype(BF16), preferred_element_type=F32) + carry_s[...]
    r1 = jnp.sum(jnp.where(hot1, before, 0.0), axis=-1, keepdims=True)
    r2 = jnp.sum(jnp.where(hot2, before, 0.0), axis=-1, keepdims=True)
    carry_s[...] = carry_s[...] + jnp.sum(cnt, axis=0, keepdims=True)
    cnt_ref[...] = carry_s[...]
    route = jnp.zeros((tm, LANES), F32)
    for pos, val in ((_ROUTE_E, (i1 - N_GROUPS).astype(F32)), (_ROUTE_E + 1, (i2 - N_GROUPS).astype(F32)),
                     (_ROUTE_W, w1), (_ROUTE_W + 1, w2), (_ROUTE_RANK, r1), (_ROUTE_RANK + 1, r2)):
        route = jnp.where(lane == pos, val, route)
    route_ref[...] = route


def moe_router(y, g, wr, br):
    m, d = y.shape
    tm = _row_tile(m, 256)
    return pl.pallas_call(
        _router_body,
        grid=(m // tm,),
        in_specs=[pl.BlockSpec((tm, d), lambda i: (i, 0)), pl.BlockSpec((1, d), lambda i: (0, 0)),
                  pl.BlockSpec((d, LANES), lambda i: (0, 0)), pl.BlockSpec((1, LANES), lambda i: (0, 0))],
        out_specs=[pl.BlockSpec((tm, d), lambda i: (i, 0)), pl.BlockSpec((tm, LANES), lambda i: (i, 0)),
                   pl.BlockSpec((1, LANES), lambda i: (0, 0))],
        out_shape=[jax.ShapeDtypeStruct((m, d), F32), jax.ShapeDtypeStruct((m, LANES), F32),
                   jax.ShapeDtypeStruct((1, LANES), F32)],
        scratch_shapes=[pltpu.VMEM((1, LANES), F32)],
        compiler_params=_cparams(("arbitrary",)),
        name="moe_router",
    )(y, g.reshape(1, d), wr, br)


def _row_copy(src, s_row, dst, d_row, sem):
    return pltpu.make_async_copy(src.at[pl.ds(s_row, 1), :], dst.at[pl.ds(d_row, 1), :], sem)


def _scatter_body(dest_ref, xn_ref, xs_in_ref, xs_ref, sem):
    del xs_in_ref
    tm = xn_ref.shape[0]
    base = pl.program_id(0) * tm

    def issue(r, _):
        for s in range(2):
            _row_copy(xn_ref, r, xs_ref, dest_ref[2 * (base + r) + s], sem).start()
        return 0

    lax.fori_loop(0, tm, issue, 0, unroll=4)
    for _ in range(2):
        pltpu.make_async_copy(xn_ref, xs_ref.at[pl.ds(0, tm), :], sem).wait()


def moe_scatter(dest, xn, rows):
    m, d = xn.shape
    tm = _row_tile(m, 256)
    grid_spec = pltpu.PrefetchScalarGridSpec(
        num_scalar_prefetch=1,
        grid=(m // tm,),
        in_specs=[pl.BlockSpec((tm, d), lambda i, dest: (i, 0)), pl.BlockSpec(memory_space=pl.ANY)],
        out_specs=pl.BlockSpec(memory_space=pl.ANY),
        scratch_shapes=[pltpu.SemaphoreType.DMA(())],
    )
    return pl.pallas_call(
        _scatter_body,
        grid_spec=grid_spec,
        out_shape=jax.ShapeDtypeStruct((rows, d), F32),
        input_output_aliases={2: 0},
        compiler_params=_cparams(("arbitrary",)),
        name="moe_scatter",
    )(dest, xn, jnp.zeros((rows, d), F32))


def _expert_body(blk_e_ref, nused_ref, xs_ref, w1_ref, w3_ref, w2_ref, ys_ref):
    del blk_e_ref
    used = pl.program_id(0) < nused_ref[0]

    @pl.when(used)
    def _():
        x = xs_ref[...].astype(BF16)
        h = jax.nn.silu(jnp.dot(x, w1_ref[0], preferred_element_type=F32))
        h = h * jnp.dot(x, w3_ref[0], preferred_element_type=F32)
        ys_ref[...] = jnp.dot(h.astype(BF16), w2_ref[0], preferred_element_type=F32)

    @pl.when(jnp.logical_not(used))
    def _():
        ys_ref[...] = jnp.zeros_like(ys_ref)


def moe_experts(blk_e, nused, xs, w1, w3, w2):
    rows, d = xs.shape
    te = MOE_BLOCK_ROWS
    de = w1.shape[2]
    grid_spec = pltpu.PrefetchScalarGridSpec(
        num_scalar_prefetch=2,
        grid=(rows // te,),
        in_specs=[pl.BlockSpec((te, d), lambda i, be, nu: (i, 0)),
                  pl.BlockSpec((1, d, de), lambda i, be, nu: (be[i], 0, 0)),
                  pl.BlockSpec((1, d, de), lambda i, be, nu: (be[i], 0, 0)),
                  pl.BlockSpec((1, de, d), lambda i, be, nu: (be[i], 0, 0))],
        out_specs=pl.BlockSpec((te, d), lambda i, be, nu: (i, 0)),
    )
    return pl.pallas_call(
        _expert_body,
        grid_spec=grid_spec,
        out_shape=jax.ShapeDtypeStruct((rows, d), F32),
        compiler_params=_cparams(("arbitrary",)),
        name="moe_experts",
    )(blk_e, nused, xs, w1, w3, w2)


def _combine_body(dest_ref, route_ref, res_ref, ys_ref, o_ref, buf, sem):
    tm = res_ref.shape[0]
    base = pl.program_id(0) * tm

    def issue(r, _):
        for s in range(2):
            _row_copy(ys_ref, dest_ref[2 * (base + r) + s], buf.at[s], r, sem).start()
        return 0

    lax.fori_loop(0, tm, issue, 0, unroll=4)
    for s in range(2):
        pltpu.make_async_copy(ys_ref.at[pl.ds(0, tm), :], buf.at[s], sem).wait()
    route = route_ref[...]
    w1 = route[:, _ROUTE_W:_ROUTE_W + 1]
    w2 = route[:, _ROUTE_W + 1:_ROUTE_W + 2]
    o_ref[...] = res_ref[...] + (w1 * buf[0] + w2 * buf[1])


def moe_combine(dest, route, res, ys):
    m, d = res.shape
    tm = _row_tile(m, 256)
    grid_spec = pltpu.PrefetchScalarGridSpec(
        num_scalar_prefetch=1,
        grid=(m // tm,),
        in_specs=[pl.BlockSpec((tm, LANES), lambda i, dest: (i, 0)),
                  pl.BlockSpec((tm, d), lambda i, dest: (i, 0)),
                  pl.BlockSpec(memory_space=pl.ANY)],
        out_specs=pl.BlockSpec((tm, d), lambda i, dest: (i, 0)),
        scratch_shapes=[pltpu.VMEM((2, tm, d), F32), pltpu.SemaphoreType.DMA(())],
    )
    return pl.pallas_call(
        _combine_body,
        grid_spec=grid_spec,
        out_shape=jax.ShapeDtypeStruct((m, d), F32),
        compiler_params=_cparams(("arbitrary",)),
        name="moe_combine",
    )(dest, route, res, ys)


def hier_moe_residual(y, g, wr, br, w1, w3, w2):
    m, d = y.shape
    te = MOE_BLOCK_ROWS
    xn, route, cnt = moe_router(y, g, wr, br)
    counts = cnt[0, N_GROUPS:N_GROUPS + N_EXPERTS].astype(jnp.int32)
    padded = (counts + te - 1) // te * te
    pend = jnp.cumsum(padded)
    pstart = pend - padded
    e = route[:, _ROUTE_E:_ROUTE_E + 2].astype(jnp.int32)
    rank = route[:, _ROUTE_RANK:_ROUTE_RANK + 2].astype(jnp.int32)
    dest = (pstart[e] + rank).reshape(-1)
    n_blk = -(-(2 * m + N_EXPERTS * (te - 1)) // te)
    blk_start = jnp.arange(n_blk, dtype=jnp.int32) * te
    blk_e = jnp.minimum(jnp.sum((pend[None, :] <= blk_start[:, None]).astype(jnp.int32), axis=1), N_EXPERTS - 1)
    nused = (pend[-1:] // te).astype(jnp.int32)
    xs = moe_scatter(dest, xn, n_blk * te)
    ys = moe_experts(blk_e, nused, xs, w1, w3, w2)
    return moe_combine(dest, route, y, ys)


def _router_weights(w_rg, b_rg, w_re, b_re):
    d = w_rg.shape[0]
    pad = LANES - N_GROUPS - N_EXPERTS
    wr = jnp.concatenate([w_rg, w_re, jnp.zeros((d, pad), F32)], axis=1)
    br = jnp.concatenate([b_rg, b_re, jnp.zeros((pad,), F32)]).reshape(1, LANES)
    return wr, br


_POS_LANE = HEAD_DIM
_TILE_BIAS_LANE = _POS_LANE + 4


def _compress_tail(ab, bias, w2_ref, add_pos):
    c = ab.shape[0] - SUBLANES
    gw = ab.shape[1] // 2
    h = ab[0:c, 0:gw] + ab[1:c + 1, gw:2 * gw] + bias
    out = jnp.dot(jax.nn.gelu(h).astype(BF16), w2_ref[...], preferred_element_type=F32)
    if add_pos:
        lane = lax.broadcasted_iota(jnp.int32, out.shape, 1) % LANES
        cend = lax.broadcasted_iota(jnp.int32, out.shape, 0) * CMP_STRIDE + (CMP_LEN - 1)
        hi = ((cend // 256) * 256).astype(F32)
        lo = (cend % 256).astype(F32)
        out = jnp.where((lane == _POS_LANE) | (lane == _POS_LANE + 1), hi,
                        jnp.where((lane == _POS_LANE + 2) | (lane == _POS_LANE + 3), lo, out))
    return out


def _compress_body(add_pos, n_k, *refs):
    k_refs = refs[:n_k]
    wab_ref, pe_ref, w2_ref, o_ref, ab_s = refs[n_k:]
    c = k_refs[0].shape[1] // CMP_STRIDE
    gw = n_k * LANES
    acc = jnp.zeros((c + SUBLANES, 2 * gw), F32)
    for j in range(CMP_STRIDE):
        xk = jnp.concatenate([r[0, pl.ds(j, c, stride=CMP_STRIDE), :] for r in k_refs], axis=1)
        xj = jnp.concatenate([xk, pe_ref[j:j + 1, :], pe_ref[CMP_STRIDE + j:CMP_STRIDE + j + 1, :],
                              jnp.zeros((SUBLANES - 2, gw), F32)], axis=0).astype(BF16)
        acc = acc + jnp.dot(xj, wab_ref[j], preferred_element_type=F32)
    bias = acc[c:c + 1, 0:gw] + acc[c + 1:c + 2, gw:2 * gw]
    ab_s[0:c, :] = acc[0:c]
    ab_s[c:c + SUBLANES, :] = jnp.zeros((SUBLANES, 2 * gw), F32)
    o_ref[0] = _compress_tail(ab_s, bias, w2_ref, add_pos).astype(o_ref.dtype)


def _compress_weights(w1, w2, pe, groups):
    d = w2.shape[0]
    w1j = w1.reshape(CMP_LEN, d, d)
    eye = jnp.eye(groups, dtype=F32)
    bd = (eye[None, :, None, :, None] * w1j[:, None, :, None, :]).reshape(CMP_LEN, groups * d, groups * d)
    wab = jnp.concatenate([bd[:CMP_STRIDE], bd[CMP_STRIDE:]], axis=2).astype(BF16)
    pe_t = jnp.tile(pe, (1, groups))
    w2p = jnp.concatenate([w2, jnp.zeros((d, LANES - d), F32)], axis=1)
    w2bd = (eye[:, None, :, None] * w2p[None, :, None, :]).reshape(groups * d, groups * LANES).astype(BF16)
    return wab, pe_t, w2bd


def nsa_compress(k, wab, pe_t, w2bd, add_pos):
    n, t, gw = k.shape
    c = t // CMP_STRIDE
    n_k = gw // LANES
    full = lambda shp: pl.BlockSpec(shp, lambda b: (0,) * len(shp))
    return pl.pallas_call(
        functools.partial(_compress_body, add_pos, n_k),
        grid=(n,),
        in_specs=[pl.BlockSpec((1, t, LANES), functools.partial(lambda u, b: (b, 0, u), u)) for u in range(n_k)]
        + [full(wab.shape), full(pe_t.shape), full(w2bd.shape)],
        out_specs=pl.BlockSpec((1, c, w2bd.shape[1]), lambda b: (b, 0, 0)),
        out_shape=jax.ShapeDtypeStruct((n, c, w2bd.shape[1]), BF16),
        scratch_shapes=[pltpu.VMEM((c + SUBLANES, 2 * gw), F32)],
        compiler_params=_cparams(("parallel",)),
        name="nsa_compress",
    )(*([k] * n_k), wab, pe_t, w2bd)


def _stack_queries(q, slopes, lane):
    rows = []
    for r, slope in enumerate(slopes):
        q2 = q[:, (r // 2) * LANES:(r // 2 + 1) * LANES]
        if r % 2:
            q2 = pltpu.roll(q2, HEAD_DIM, 1)
        s_hi, s_lo = _split_hi_lo(jnp.full(q2.shape, slope, F32))
        odd = ((lane - _POS_LANE) % 2) == 1
        feat = jnp.where(odd, s_lo.astype(F32), s_hi.astype(F32))
        rows.append(jnp.where(lane < HEAD_DIM, q2 * _QSCALE,
                              jnp.where(lane < _POS_LANE + 4, feat,
                                        jnp.where(lane == _TILE_BIAS_LANE, 1.0, 0.0))).astype(BF16))
    return jnp.concatenate(rows, axis=0)


def _select_blocks(imp, qb, lane):
    vis = lane <= qb
    forced = (lane == 0) | (lane == qb) | (lane == qb - 1)
    score = jnp.where(vis & jnp.logical_not(forced), imp, -jnp.inf)
    lane_f = lane.astype(F32)
    sel = forced
    for _ in range(SEL_TOPK - 3):
        pick = lane_f == _first_argmax(score, lane_f)
        sel = sel | pick
        score = jnp.where(pick, -jnp.inf, score)
    return sel & vis


def _nsa_prompt_body(slopes_ref, q_ref, gt_ref, ck_ref, cv_ref, cover_ref, ks_ref, vs_ref, kw_ref, vw_ref,
                     o_ref):
    g = pl.program_id(1)
    i = pl.program_id(2)
    tq = q_ref.shape[1]
    r_heads = q_ref.shape[2] // HEAD_DIM
    rows = r_heads * tq
    t0 = i * tq
    lane = lax.broadcasted_iota(jnp.int32, (tq, LANES), 1)
    slopes = [slopes_ref[r_heads * g + r] for r in range(r_heads)]
    cpt = tq // _CHUNK_ROWS
    qa_chunks = _row_chunks(_stack_queries(q_ref[0], slopes, lane))

    ck = ck_ref[0]
    cv = cv_ref[0]
    cend = lax.broadcasted_iota(jnp.int32, (_CHUNK_ROWS, ck.shape[0]), 1) * CMP_STRIDE + (CMP_LEN - 1)
    crow = lax.broadcasted_iota(jnp.int32, (_CHUNK_ROWS, ck.shape[0]), 0)
    o_c, psum = [], [None] * cpt
    for ci, qc in enumerate(qa_chunks):
        ok_c = cend <= t0 + (ci % cpt) * _CHUNK_ROWS + crow
        s_c = jnp.where(ok_c, _dot_nt(qc, ck), NEG_BIG)
        e_c = jnp.where(ok_c, jnp.exp2(s_c - jnp.max(s_c, axis=-1, keepdims=True)), 0.0)
        p_c = e_c / jnp.maximum(jnp.sum(e_c, axis=-1, keepdims=True), 1e-30)
        o_c.append(jnp.dot(p_c.astype(BF16), cv, preferred_element_type=F32))
        psum[ci % cpt] = p_c if psum[ci % cpt] is None else psum[ci % cpt] + p_c

    imp = jnp.dot(jnp.concatenate(psum, axis=0), cover_ref[...], precision=HIGHEST,
                  preferred_element_type=F32)
    qb = (t0 + lax.broadcasted_iota(jnp.int32, (tq, LANES), 0)) // SEL_BLOCK
    pen_chunks = _row_chunks(jnp.where(_select_blocks(imp, qb, lane), 0.0, NEG_BIG).astype(BF16))
    qa = jnp.concatenate(qa_chunks, axis=0)
    qs = jnp.concatenate([qa, jnp.concatenate([pen_chunks[ci % cpt] for ci in range(len(qa_chunks))], axis=0)],
                         axis=1)

    tk = tq
    klane = lax.broadcasted_iota(jnp.int32, (tk, LANES), 1)
    krow = lax.broadcasted_iota(jnp.int32, (tk, LANES), 0)
    krow_b = krow.astype(F32).astype(BF16)
    is_pos = (klane == _POS_LANE) | (klane == _POS_LANE + 1)
    is_off = (klane == _POS_LANE + 2) | (klane == _POS_LANE + 3)
    blocks_per_tile = tk // SEL_BLOCK
    one_b = jnp.ones((tk, LANES), BF16)
    causal_masks = _causal_chunk_masks(tq, tk)

    def keys_with_pos(k_ref, j):
        kt = k_ref[0, pl.ds(pl.multiple_of(j * tk, tk), tk), :]
        off_b = jnp.full((tk, LANES), ((j - i) * tk).astype(F32), F32).astype(BF16)
        return jnp.where(is_pos, krow_b, jnp.where(is_off, off_b, kt))

    def values_with_ones(v_ref, j):
        return jnp.where(klane == _POS_LANE, one_b, v_ref[0, pl.ds(pl.multiple_of(j * tk, tk), tk), :])

    def sel_scores(j):
        onehot = jnp.where(klane == j * blocks_per_tile + krow // SEL_BLOCK, 1.0, 0.0).astype(BF16)
        return _dot_nt(qs, jnp.concatenate([keys_with_pos(ks_ref, j), onehot], axis=1))

    def sel_body(j, c):
        s, carry = c
        return sel_scores(j + 1), _flash_update(s, values_with_ones(vs_ref, j), carry)

    def finish(carry):
        return [acc / acc[:, _POS_LANE:_POS_LANE + 1] for _, acc in carry]

    s, carry = lax.fori_loop(0, i, sel_body, (sel_scores(0), _flash_init(rows)))
    o_s = finish(_flash_update(s, values_with_ones(vs_ref, i), carry, causal_masks))

    n_win = WINDOW // tk + 1
    kws, vws = [], []
    for w in range(n_win):
        j_true = i - (n_win - 1) + w
        j = jnp.maximum(j_true, 0)
        tile_bias = jnp.full((tk, LANES), jnp.where(j_true >= 0, 0.0, NEG_BIG), F32).astype(BF16)
        kws.append(jnp.where(klane == _TILE_BIAS_LANE, tile_bias, keys_with_pos(kw_ref, j)))
        vws.append(values_with_ones(vw_ref, j))
    s_w = _dot_nt(qa, jnp.concatenate(kws, axis=0))
    wr = lax.broadcasted_iota(jnp.int32, (_CHUNK_ROWS, n_win * tk), 0)
    wc = lax.broadcasted_iota(jnp.int32, (_CHUNK_ROWS, n_win * tk), 1)
    ps = []
    for ci in range(rows // _CHUNK_ROWS):
        r_off = wr + (ci % cpt) * _CHUNK_ROWS
        newest = (n_win - 1) * tk
        ok = ((wc < tk) & (wc > r_off)) | ((wc >= tk) & (wc < newest)) | ((wc >= newest) & (wc - newest <= r_off))
        s = jnp.where(ok, s_w[ci * _CHUNK_ROWS:(ci + 1) * _CHUNK_ROWS], NEG_BIG)
        ps.append(jnp.exp2(s - jnp.max(s, axis=-1, keepdims=True)).astype(BF16))
    acc_w = jnp.dot(jnp.concatenate(ps, axis=0), jnp.concatenate(vws, axis=0), preferred_element_type=F32)
    o_w = finish([(None, a) for a in _row_chunks(acc_w)])

    gates = _row_chunks(jax.nn.sigmoid(gt_ref[0]))
    outs = []
    for r in range(r_heads):
        parts = []
        for u in range(cpt):
            ci = r * cpt + u
            gr = [gates[u][:, 3 * r + b:3 * r + b + 1] for b in range(3)]
            parts.append(gr[0] * o_c[ci] + gr[1] * o_s[ci] + gr[2] * o_w[ci])
        outs.append(jnp.concatenate(parts, axis=0))
    lane_o = lax.broadcasted_iota(jnp.int32, (tq, LANES), 1)
    pairs = [jnp.where(lane_o < HEAD_DIM, outs[2 * u], pltpu.roll(outs[2 * u + 1], HEAD_DIM, 1))
             for u in range(r_heads // 2)]
    o_ref[0] = jnp.concatenate(pairs, axis=1)


def nsa_prompt(q, gt, ck, cv, cover, ks, vs, kw, vw, slopes):
    n, t, w = q.shape
    groups = ks.shape[2] // LANES
    qw = w // groups
    tq = 256
    c = ck.shape[1]
    assert t % tq == 0 and -(-t // SEL_BLOCK) <= LANES and t // SEL_BLOCK >= SEL_TOPK and WINDOW % tq == 0
    kv_spec = lambda rows: pl.BlockSpec((1, rows, LANES), lambda b, g, i: (b, 0, g))
    grid_spec = pltpu.PrefetchScalarGridSpec(
        num_scalar_prefetch=0,
        grid=(n, groups, t // tq),
        in_specs=[pl.BlockSpec(memory_space=pltpu.SMEM),
                  pl.BlockSpec((1, tq, qw), lambda b, g, i: (b, i, g)),
                  pl.BlockSpec((1, tq, LANES), lambda b, g, i: (b, i, g)),
                  kv_spec(c), kv_spec(c),
                  pl.BlockSpec(cover.shape, lambda b, g, i: (0, 0)),
                  kv_spec(t), kv_spec(t), kv_spec(t), kv_spec(t)],
        out_specs=pl.BlockSpec((1, tq, qw), lambda b, g, i: (b, i, g)),
    )
    return pl.pallas_call(
        _nsa_prompt_body,
        grid_spec=grid_spec,
        out_shape=jax.ShapeDtypeStruct((n, t, w), F32),
        compiler_params=_cparams(("parallel", "parallel", "arbitrary")),
        name="nsa_prompt",
    )(jnp.asarray(slopes * np.float32(LOG2E)), q, gt, ck, cv, cover, ks, vs, kw, vw)


def _nsa_cover(n_cmp_rows, n_sel):
    i = np.arange(n_cmp_rows)[:, None]
    j = np.arange(-(-n_sel // LANES) * LANES)[None, :]
    cov = (CMP_STRIDE * i < SEL_BLOCK * (j + 1)) & (CMP_STRIDE * i + CMP_LEN > SEL_BLOCK * j) & (j < n_sel)
    return jnp.asarray(cov.astype(np.float32))


def _pad_group_cols(w, width):
    d, gw = w.shape
    g = gw // width
    return jnp.pad(w.reshape(d, g, width), ((0, 0), (0, 0), (0, LANES - width))).reshape(d, g * LANES)


def _own_head_block(rows, width, nq):
    r = lax.broadcasted_iota(jnp.int32, (rows, width), 0)
    c = lax.broadcasted_iota(jnp.int32, (rows, width), 1)
    return (c // HEAD_DIM) == (r // nq)


def _fold_heads(x, own):
    width = x.shape[1]
    fold = jnp.where(lax.broadcasted_iota(jnp.int32, (width, HEAD_DIM), 0) % HEAD_DIM
                     == lax.broadcasted_iota(jnp.int32, (width, HEAD_DIM), 1), 1.0, 0.0)
    return jnp.dot(jnp.where(own, x, 0.0), fold, precision=HIGHEST, preferred_element_type=F32)


_MOBA_BLOCKS_PER_STEP = 2


def _moba_decode_body(nblk, past, pt_ref, q_ref, slope_ref, kn_ref, vn_ref, *refs):
    del pt_ref
    n_page_refs = _MOBA_BLOCKS_PER_STEP * (MOBA_BLOCK // PAGE_SIZE)
    k_refs, v_refs = refs[:n_page_refs], refs[n_page_refs:2 * n_page_refs]
    o_ref, m_s, l_s, g_s, acc_s = refs[2 * n_page_refs:]
    j = pl.program_id(1)
    rows, width = q_ref.shape[1], q_ref.shape[2]
    nq = rows // (width // HEAD_DIM)
    own = _own_head_block(rows, width, nq)
    q_hi, q_lo = _split_hi_lo(jnp.where(own, q_ref[0] * HEAD_DIM ** -0.5, 0.0))
    slope = slope_ref[...]
    lane = lax.broadcasted_iota(jnp.int32, (rows, LANES), 1)
    lane_f = lane.astype(F32)

    @pl.when(j == 0)
    def _():
        m_s[...] = jnp.full(m_s.shape, NEG_BIG, F32)
        l_s[...] = jnp.zeros(l_s.shape, F32)
        g_s[...] = jnp.zeros(g_s.shape, F32)

    lhs = jnp.concatenate([q_hi, q_lo], axis=0)
    pages_per_blk = MOBA_BLOCK // PAGE_SIZE
    span = jnp.concatenate([lane_f + pg * PAGE_SIZE for pg in range(pages_per_blk)], axis=1)
    for bk in range(_MOBA_BLOCKS_PER_STEP):
        jb = j * _MOBA_BLOCKS_PER_STEP + bk
        pages = slice(bk * pages_per_blk, (bk + 1) * pages_per_blk)
        kt = jnp.concatenate([r[0, 0].reshape(width, PAGE_SIZE) for r in k_refs[pages]], axis=1).astype(BF16)
        vt = jnp.concatenate([r[0, 0].reshape(width, PAGE_SIZE) for r in v_refs[pages]], axis=1).astype(BF16)
        s2 = jnp.dot(lhs, kt, preferred_element_type=F32)
        gate = jnp.sum(s2[:rows] + s2[rows:], axis=-1, keepdims=True)
        s = s2[:rows] + slope * (span + (jb * MOBA_BLOCK - past).astype(F32))
        m_j = jnp.max(s, axis=-1, keepdims=True)
        p = jnp.exp(s - m_j)
        m_s[...] = jnp.where(lane == jb, m_j, m_s[...])
        l_s[...] = jnp.where(lane == jb, jnp.sum(p, axis=-1, keepdims=True), l_s[...])
        g_s[...] = jnp.where(lane == jb, gate, g_s[...])
        acc_s[jb] = _dot_nt(p.astype(BF16), vt)

    @pl.when(j == nblk // _MOBA_BLOCKS_PER_STEP - 1)
    def _():
        g = jnp.where(lane < nblk, g_s[...], -jnp.inf)
        sel = jnp.zeros(g.shape, jnp.bool_)
        for _ in range(MOBA_TOPK):
            pick = lane_f == _first_argmax(g, lane_f)
            sel = sel | pick
            g = jnp.where(pick, -jnp.inf, g)
        sel = sel & (lane < nblk)
        ok_o = (lane < nq) & (lane <= lax.broadcasted_iota(jnp.int32, (rows, LANES), 0) % nq)
        s_o = jnp.dot(q_hi, kn_ref[0].astype(BF16), preferred_element_type=F32) + slope * lane_f
        s_o = jnp.where(ok_o, s_o, NEG_BIG)
        m_o = jnp.max(s_o, axis=-1, keepdims=True)
        p_o = jnp.exp(s_o - m_o)
        l_o = jnp.sum(p_o, axis=-1, keepdims=True)
        acc_o = _dot_nt(p_o.astype(BF16), vn_ref[0].astype(BF16))
        m_all = jnp.maximum(jnp.max(jnp.where(sel, m_s[...], -jnp.inf), axis=-1, keepdims=True), m_o)
        w = jnp.where(sel, jnp.exp(m_s[...] - m_all), 0.0)
        w_o = jnp.exp(m_o - m_all)
        den = jnp.sum(w * l_s[...], axis=-1, keepdims=True) + w_o * l_o
        num = w_o * acc_o
        for jj in range(nblk):
            num = num + w[:, jj:jj + 1] * acc_s[jj]
        o_ref[0] = _fold_heads(num, own) / den


def moba_decode(q_tiled, kn_t, vn_t, cache_k, cache_v, page_table, layer, slopes):
    b, rows, width = q_tiled.shape
    heads = cache_k.shape[2]
    nq = rows // heads
    n_pages = page_table.shape[1]
    past = n_pages * PAGE_SIZE
    pages_per_blk = MOBA_BLOCK // PAGE_SIZE
    nblk = past // MOBA_BLOCK
    pages_per_step = _MOBA_BLOCKS_PER_STEP * pages_per_blk
    assert past % MOBA_BLOCK == 0 and MOBA_TOPK <= nblk <= LANES and nblk % _MOBA_BLOCKS_PER_STEP == 0
    slope_rows = jnp.asarray(np.repeat(slopes, nq).reshape(rows, 1))
    page = lambda pg: pl.BlockSpec((1, 1, heads, HEAD_DIM, PAGE_SIZE),
                                   lambda s, j, pt: (layer, pt[s, pages_per_step * j + pg], 0, 0, 0))
    pages = [page(pg) for pg in range(pages_per_step)]
    per_seq = lambda a: pl.BlockSpec((1,) + a.shape[1:], lambda s, j, pt: (s, 0, 0))
    grid_spec = pltpu.PrefetchScalarGridSpec(
        num_scalar_prefetch=1,
        grid=(b, nblk // _MOBA_BLOCKS_PER_STEP),
        in_specs=[per_seq(q_tiled), pl.BlockSpec((rows, 1), lambda s, j, pt: (0, 0)), per_seq(kn_t), per_seq(vn_t)]
        + pages + pages,
        out_specs=pl.BlockSpec((1, rows, HEAD_DIM), lambda s, j, pt: (s, 0, 0)),
        scratch_shapes=[pltpu.VMEM((rows, LANES), F32), pltpu.VMEM((rows, LANES), F32),
                        pltpu.VMEM((rows, LANES), F32), pltpu.VMEM((nblk, rows, width), F32)],
    )
    return pl.pallas_call(
        functools.partial(_moba_decode_body, nblk, past),
        grid_spec=grid_spec,
        out_shape=jax.ShapeDtypeStruct((b, rows, HEAD_DIM), F32),
        compiler_params=_cparams(("parallel", "arbitrary")),
        name="moba_decode",
    )(page_table, q_tiled, slope_rows, kn_t, vn_t, *([cache_k] * pages_per_step), *([cache_v] * pages_per_step))


_CHUNKS_PER_PAGE = PAGE_SIZE // CMP_STRIDE


def _page_ab_body(k_ref, wj_ref, o_ref, xs):
    pages, groups = k_ref.shape[1], k_ref.shape[2]
    for g in range(groups):
        xs[...] = jnp.swapaxes(k_ref[0, :, g], 1, 2)
        acc = jnp.zeros((pages * _CHUNKS_PER_PAGE, LANES), F32)
        for j in range(CMP_STRIDE):
            x = xs[:, pl.ds(j, _CHUNKS_PER_PAGE, stride=CMP_STRIDE), :]
            x = x.reshape(pages * _CHUNKS_PER_PAGE, HEAD_DIM).astype(BF16)
            acc = acc + jnp.dot(x, wj_ref[j], preferred_element_type=F32)
        o_ref[:, g * LANES:(g + 1) * LANES] = acc


def nsa_page_ab(cache, layer, wj):
    _, n_phys, groups, _, _ = cache.shape
    pages = next(p for p in (32, 16, 8, 4, 2, 1) if n_phys % p == 0)
    return pl.pallas_call(
        _page_ab_body,
        grid=(n_phys // pages,),
        in_specs=[pl.BlockSpec((1, pages, groups, HEAD_DIM, PAGE_SIZE), lambda i: (layer, i, 0, 0, 0)),
                  pl.BlockSpec(wj.shape, lambda i: (0, 0, 0))],
        out_specs=pl.BlockSpec((pages * _CHUNKS_PER_PAGE, groups * LANES), lambda i: (i, 0)),
        out_shape=jax.ShapeDtypeStruct((n_phys * _CHUNKS_PER_PAGE, groups * LANES), F32),
        scratch_shapes=[pltpu.VMEM((pages, PAGE_SIZE, HEAD_DIM), F32)],
        compiler_params=_cparams(("parallel",)),
        name="nsa_page_ab",
    )(cache, wj)


def _page_ab_copy(ab_hbm, pt_ref, s, p, buf, sem):
    return pltpu.make_async_copy(ab_hbm.at[pt_ref[s, p]], buf.at[pl.ds(p * _CHUNKS_PER_PAGE, _CHUNKS_PER_PAGE), :], sem)


def _decode_queries(q_ref, slope_ref):
    q = q_ref[0]
    lane = lax.broadcasted_iota(jnp.int32, q.shape, 1)
    s_hi, s_lo = _split_hi_lo(jnp.broadcast_to(slope_ref[...], q.shape))
    feat = jnp.where(((lane - _POS_LANE) % 2) == 1, s_lo.astype(F32), s_hi.astype(F32))
    return jnp.where(lane < HEAD_DIM, q * HEAD_DIM ** -0.5,
                     jnp.where(lane < _POS_LANE + 4, feat, 0.0)).astype(BF16)


def _nsa_decode_cmp_body(n_pages, past, pt_ref, q_ref, slope_ref, abk_ref, abv_ref, wjk_ref, wjv_ref, pek_ref,
                         pev_ref, w2k_ref, w2v_ref, cover_ref, oc_ref, pen_ref, bufk, bufv, sem):
    s = pl.program_id(0)
    c = n_pages * _CHUNKS_PER_PAGE
    rows = q_ref.shape[1]
    groups = abk_ref.shape[2] // LANES
    rpg = rows // groups
    nq = pen_ref.shape[2] // 2

    def issue(p, _):
        _page_ab_copy(abk_ref, pt_ref, s, p, bufk, sem.at[0]).start()
        _page_ab_copy(abv_ref, pt_ref, s, p, bufv, sem.at[1]).start()
        return 0

    def drain(p, _):
        _page_ab_copy(abk_ref, pt_ref, s, p, bufk, sem.at[0]).wait()
        _page_ab_copy(abv_ref, pt_ref, s, p, bufv, sem.at[1]).wait()
        return 0

    lax.fori_loop(0, n_pages, issue, 0)
    pad = jnp.zeros((SUBLANES, bufk.shape[1]), F32)
    bufk[c:c + SUBLANES, :] = pad
    bufv[c:c + SUBLANES, :] = pad
    lax.fori_loop(0, n_pages, drain, 0)

    def tokens(buf, wj_ref, pe_ref, w2_ref, g, add_pos):
        acc = jnp.zeros((SUBLANES, LANES), F32)
        for j in range(CMP_STRIDE):
            xj = jnp.concatenate([pe_ref[j:j + 1, :], pe_ref[CMP_STRIDE + j:CMP_STRIDE + j + 1, :],
                                  jnp.zeros((SUBLANES - 2, HEAD_DIM), F32)], axis=0).astype(BF16)
            acc = acc + jnp.dot(xj, wj_ref[j], preferred_element_type=F32)
        bias = acc[0:1] + pltpu.roll(acc, HEAD_DIM, 1)[1:2]
        slab = slice(g * LANES, (g + 1) * LANES)
        h = buf[0:c, slab] + pltpu.roll(buf[1:c + 1, slab], HEAD_DIM, 1) + bias
        out = jnp.dot(jax.nn.gelu(h).astype(BF16), w2_ref[...], preferred_element_type=F32)
        if add_pos:
            lane = lax.broadcasted_iota(jnp.int32, out.shape, 1)
            cend = lax.broadcasted_iota(jnp.int32, out.shape, 0) * CMP_STRIDE + (CMP_LEN - 1)
            hi = ((cend // 256) * 256).astype(F32)
            lo = (cend % 256).astype(F32)
            out = jnp.where((lane == _POS_LANE) | (lane == _POS_LANE + 1), hi,
                            jnp.where((lane == _POS_LANE + 2) | (lane == _POS_LANE + 3), lo, out))
        return out.astype(BF16)

    qa = _decode_queries(q_ref, slope_ref)
    for g in range(groups):
        ck = tokens(bufk, wjk_ref, pek_ref, w2k_ref, g, True)
        cv = tokens(bufv, wjv_ref, pev_ref, w2v_ref, g, False)
        s_c = _dot_nt(qa[g * rpg:(g + 1) * rpg], ck)
        cend = lax.broadcasted_iota(jnp.int32, s_c.shape, 1) * CMP_STRIDE + (CMP_LEN - 1)
        t_q = past + lax.broadcasted_iota(jnp.int32, s_c.shape, 0) % nq
        ok = cend <= t_q
        s_c = jnp.where(ok, s_c, NEG_BIG)
        e_c = jnp.where(ok, jnp.exp(s_c - jnp.max(s_c, axis=-1, keepdims=True)), 0.0)
        p_c = e_c / jnp.maximum(jnp.sum(e_c, axis=-1, keepdims=True), 1e-30)
        oc_ref[0, g * rpg:(g + 1) * rpg, :] = jnp.dot(p_c.astype(BF16), cv, preferred_element_type=F32)
        half = p_c[0:2 * nq]
        for u in range(1, rpg // (2 * nq)):
            half = half + p_c[2 * nq * u:2 * nq * (u + 1)]
        psum = half + pltpu.roll(half, nq, 0)
        imp = jnp.dot(psum, cover_ref[...], precision=HIGHEST, preferred_element_type=F32)
        lane = lax.broadcasted_iota(jnp.int32, imp.shape, 1)
        qb = (past + lax.broadcasted_iota(jnp.int32, imp.shape, 0) % nq) // SEL_BLOCK
        pen_ref[0, g] = jnp.where(_select_blocks(imp, qb, lane), 0.0, NEG_BIG)


def nsa_decode_cmp(q_rows, slope_rows, ab_k, ab_v, page_table, wjk, wjv, pek, pev, w2k, w2v, cover, nq):
    b, rows, _ = q_rows.shape
    n_pages = page_table.shape[1]
    past = n_pages * PAGE_SIZE
    groups = ab_k.shape[1] // LANES
    c = n_pages * _CHUNKS_PER_PAGE
    assert past % CMP_STRIDE == 0 and nq < CMP_STRIDE and 2 * nq == SUBLANES
    ab_k3 = ab_k.reshape(-1, _CHUNKS_PER_PAGE, groups * LANES)
    ab_v3 = ab_v.reshape(-1, _CHUNKS_PER_PAGE, groups * LANES)
    full = lambda a: pl.BlockSpec(a.shape, lambda s, pt: (0,) * a.ndim)
    grid_spec = pltpu.PrefetchScalarGridSpec(
        num_scalar_prefetch=1,
        grid=(b,),
        in_specs=[pl.BlockSpec((1, rows, LANES), lambda s, pt: (s, 0, 0)), full(slope_rows),
                  pl.BlockSpec(memory_space=pl.ANY), pl.BlockSpec(memory_space=pl.ANY),
                  full(wjk), full(wjv), full(pek), full(pev), full(w2k), full(w2v), full(cover)],
        out_specs=[pl.BlockSpec((1, rows, LANES), lambda s, pt: (s, 0, 0)),
                   pl.BlockSpec((1, groups, 2 * nq, cover.shape[1]), lambda s, pt: (s, 0, 0, 0))],
        scratch_shapes=[pltpu.VMEM((c + SUBLANES, groups * LANES), F32),
                        pltpu.VMEM((c + SUBLANES, groups * LANES), F32),
                        pltpu.SemaphoreType.DMA((2,))],
    )
    return pl.pallas_call(
        functools.partial(_nsa_decode_cmp_body, n_pages, past),
        grid_spec=grid_spec,
        out_shape=[jax.ShapeDtypeStruct((b, rows, LANES), F32),
                   jax.ShapeDtypeStruct((b, groups, 2 * nq, cover.shape[1]), F32)],
        compiler_params=_cparams(("arbitrary",)),
        name="nsa_decode_cmp",
    )(page_table, q_rows, slope_rows, ab_k3, ab_v3, wjk, wjv, pek, pev, w2k, w2v, cover)


_SEL_PAGES_PER_STEP = 4


def _nsa_decode_sel_body(n_steps, past, wbuf, pt_ref, q_ref, slope_ref, pen_ref, oc_ref, gt_ref, ksn_ref, vsn_ref,
                         kwn_ref, vwn_ref, wk_ref, wv_ref, *refs):
    del pt_ref
    pp = _SEL_PAGES_PER_STEP
    k_refs, v_refs = refs[:pp], refs[pp:2 * pp]
    o_ref, m_s, l_s, acc_s = refs[2 * pp:]
    step = pl.program_id(1)
    rows, width = q_ref.shape[1], q_ref.shape[2]
    groups = pen_ref.shape[1]
    rpg = rows // groups
    nq = pen_ref.shape[2] // 2
    n_sel = pen_ref.shape[3]
    own = _own_head_block(rows, width, rpg)
    qb = jnp.where(own, q_ref[0] * HEAD_DIM ** -0.5, 0.0).astype(BF16)
    slope = slope_ref[...]
    blocks_per_page = PAGE_SIZE // SEL_BLOCK
    pen_rows = jnp.concatenate([pen_ref[0, g] for g in range(groups) for _ in range(rpg // (2 * nq))],
                               axis=0).astype(BF16)

    @pl.when(step == 0)
    def _():
        m_s[...] = jnp.full(m_s.shape, -jnp.inf, F32)
        l_s[...] = jnp.zeros(l_s.shape, F32)
        acc_s[...] = jnp.zeros(acc_s.shape, F32)

    def attend(carry, kt, vt, bias, mask=None):
        m, l, acc = carry
        s = jnp.dot(qb, kt, preferred_element_type=F32) + bias
        if mask is not None:
            s = jnp.where(mask, s, NEG_BIG)
        m_new = jnp.maximum(m, jnp.max(s, axis=-1, keepdims=True))
        p = jnp.exp(s - m_new)
        alpha = jnp.exp(m - m_new)
        return m_new, alpha * l + jnp.sum(p, axis=-1, keepdims=True), alpha * acc + _dot_nt(p.astype(BF16), vt)

    col = lax.broadcasted_iota(jnp.int32, (rows, PAGE_SIZE), 1)
    col_f = col.astype(F32)
    span = pp * PAGE_SIZE
    e_row = lax.broadcasted_iota(jnp.int32, (n_sel, span), 0)
    e_col = lax.broadcasted_iota(jnp.int32, (n_sel, span), 1)
    expand = jnp.where(e_row == step * (pp * blocks_per_page) + e_col // SEL_BLOCK, 1.0, 0.0).astype(BF16)
    pos = lax.broadcasted_iota(jnp.int32, (rows, span), 1).astype(F32) + (step * span - past).astype(F32)
    bias = jnp.dot(pen_rows, expand, preferred_element_type=F32) + slope * pos
    kt = jnp.concatenate([r[0, 0].reshape(width, PAGE_SIZE) for r in k_refs], axis=1).astype(BF16)
    vt = jnp.concatenate([r[0, 0].reshape(width, PAGE_SIZE) for r in v_refs], axis=1).astype(BF16)
    m_s[...], l_s[...], acc_s[...] = attend((m_s[...], l_s[...], acc_s[...]), kt, vt, bias)

    @pl.when(step == n_steps - 1)
    def _():
        gates = jax.nn.sigmoid(gt_ref[0])
        qi = lax.broadcasted_iota(jnp.int32, (rows, PAGE_SIZE), 0) % nq
        ok_n = (col <= qi) & (col < nq)
        pen_new = pen_rows[:, past // SEL_BLOCK:past // SEL_BLOCK + 1].astype(F32)
        _, l_sel, acc_sel = attend((m_s[...], l_s[...], acc_s[...]), ksn_ref[0].astype(BF16),
                                   vsn_ref[0].astype(BF16), slope * col_f + pen_new, ok_n)
        o_sel = _fold_heads(acc_sel, own) / l_sel
        wqi = lax.broadcasted_iota(jnp.int32, (rows, wbuf), 0) % nq
        wc = lax.broadcasted_iota(jnp.int32, (rows, wbuf), 1)
        dist_w = wqi + wbuf - wc
        ok_w = (dist_w >= 0) & (dist_w < WINDOW) & (past - wbuf + wc >= 0)
        init = (jnp.full((rows, 1), -jnp.inf, F32), jnp.zeros((rows, 1), F32), jnp.zeros((rows, width), F32))
        carry_w = attend(init, wk_ref[0, 0].reshape(width, wbuf).astype(BF16),
                         wv_ref[0, 0].reshape(width, wbuf).astype(BF16), slope * (wc - wbuf).astype(F32), ok_w)
        _, l_win, acc_win = attend(carry_w, kwn_ref[0].astype(BF16), vwn_ref[0].astype(BF16), slope * col_f, ok_n)
        o_win = _fold_heads(acc_win, own) / l_win
        o_ref[0] = (gates[:, 0:1] * oc_ref[0, :, 0:HEAD_DIM] + gates[:, 1:2] * o_sel + gates[:, 2:3] * o_win)


def nsa_decode_sel(q_tiled, slope_rows, pen, o_c, gt_rows, ks_new, vs_new, kw_new, vw_new, win_k, win_v,
                   cache_k, cache_v, page_table, layer):
    b, rows, _ = q_tiled.shape
    groups = cache_k.shape[2]
    n_pages = page_table.shape[1]
    past = n_pages * PAGE_SIZE
    pp = _SEL_PAGES_PER_STEP
    wbuf = win_k.shape[4]
    assert n_pages % pp == 0 and past % SEL_BLOCK == 0
    n_steps = n_pages // pp
    per_seq = lambda a: pl.BlockSpec((1,) + a.shape[1:], lambda s, j, pt: (s,) + (0,) * (a.ndim - 1))
    page = lambda u: pl.BlockSpec((1, 1, groups, HEAD_DIM, PAGE_SIZE),
                                  lambda s, j, pt: (layer, pt[s, pp * j + u], 0, 0, 0))
    win = pl.BlockSpec((1, 1, groups, HEAD_DIM, wbuf), lambda s, j, pt: (layer, s, 0, 0, 0))
    grid_spec = pltpu.PrefetchScalarGridSpec(
        num_scalar_prefetch=1,
        grid=(b, n_steps),
        in_specs=[per_seq(q_tiled), pl.BlockSpec(slope_rows.shape, lambda s, j, pt: (0, 0)),
                  per_seq(pen), per_seq(o_c), per_seq(gt_rows),
                  per_seq(ks_new), per_seq(vs_new), per_seq(kw_new), per_seq(vw_new), win, win]
        + [page(u) for u in range(pp)] * 2,
        out_specs=pl.BlockSpec((1, rows, HEAD_DIM), lambda s, j, pt: (s, 0, 0)),
        scratch_shapes=[pltpu.VMEM((rows, 1), F32), pltpu.VMEM((rows, 1), F32),
                        pltpu.VMEM((rows, groups * HEAD_DIM), F32)],
    )
    return pl.pallas_call(
        functools.partial(_nsa_decode_sel_body, n_steps, past, wbuf),
        grid_spec=grid_spec,
        out_shape=jax.ShapeDtypeStruct((b, rows, HEAD_DIM), F32),
        compiler_params=_cparams(("parallel", "arbitrary")),
        name="nsa_decode_sel",
    )(page_table, q_tiled, slope_rows, pen, o_c, gt_rows, ks_new, vs_new, kw_new, vw_new, win_k, win_v,
      *([cache_k] * pp), *([cache_v] * pp))


def _heads_to_rows(x, n, t, heads):
    return x.reshape(n, t, heads, HEAD_DIM).transpose(0, 2, 1, 3).reshape(n, heads * t, HEAD_DIM)


def _rows_to_heads(x, n, t, heads):
    return x.reshape(n, heads, t, HEAD_DIM).transpose(0, 2, 1, 3).reshape(n * t, heads * HEAD_DIM)


def _new_kv_cols(x, n, t):
    return jnp.pad(x.reshape(n, t, x.shape[-1]).transpose(0, 2, 1), ((0, 0), (0, 0), (0, PAGE_SIZE - t)))


def _tokens_minor(a):
    return a.transpose(0, 1, 3, 4, 2)


def _layer_ab(layer, l, yp, ys, dims, p):
    bp, tp, bs, ts = dims
    d_rnn = p["conv_w"].shape[2]
    heads = (p["w_in_ab"].shape[2] - 2 * d_rnn) // (3 * HEAD_DIM)
    hw = heads * HEAD_DIM
    w_in = p["w_in_ab"][l].astype(BF16)
    cuts = [0, d_rnn, 2 * d_rnn, 2 * d_rnn + hw, 2 * d_rnn + 2 * hw, 2 * d_rnn + 3 * hw]
    ws = [w_in[:, a:b] for a, b in zip(cuts[:-1], cuts[1:])]
    w_out = p["w_out_ab"][l].astype(BF16)
    w_out = [w_out[:d_rnn], w_out[d_rnn:]]
    rg = (p["conv_w"][l], p["conv_b"][l], _block_diag(p["w_rg_a"][l]).astype(BF16), p["b_rg_a"][l],
          _block_diag(p["w_rg_i"][l]).astype(BF16), p["b_rg_i"][l], p["rg_lambda"][l])
    slopes = _alibi_slopes(heads)
    g = p["norm_mix"][layer]
    f32x5 = [(i, F32) for i in range(5)]

    xr, gr, q, k, v, kb, vb = norm_proj(yp, g, ws, f32x5 + [(3, BF16), (4, BF16)])
    seq = lambda a: a.reshape(bp, tp, a.shape[-1])
    r_p, conv_p, h_p = rglru(seq(xr), seq(gr), jnp.zeros((bp, CONV_W - 1, d_rnn), F32),
                             jnp.zeros((bp, d_rnn), F32), *rg)
    att = moba_prompt(seq(q), seq(kb), seq(vb), block_mean(seq(k), MOBA_BLOCK), slopes)
    yp = proj_residual([r_p.reshape(bp * tp, d_rnn), att.reshape(bp * tp, hw)], w_out, yp)
    out_p = (k.reshape(bp, tp, heads, HEAD_DIM), v.reshape(bp, tp, heads, HEAD_DIM), conv_p, h_p)

    xr, gr, q, k, v = norm_proj(ys, g, ws, f32x5)
    seq = lambda a: a.reshape(bs, ts, a.shape[-1])
    r_s, conv_s, h_s = rglru(seq(xr), seq(gr), p["state_conv"][l], p["state_rglru"][l], *rg)
    att = moba_decode(jnp.tile(_heads_to_rows(q, bs, ts, heads), (1, 1, heads)), _new_kv_cols(k, bs, ts),
                      _new_kv_cols(v, bs, ts), _tokens_minor(p["cache_moba_k"]), _tokens_minor(p["cache_moba_v"]),
                      p["page_table"], l, slopes)
    ys = proj_residual([r_s.reshape(bs * ts, d_rnn), _rows_to_heads(att, bs, ts, heads)], w_out, ys)
    out_s = (k.reshape(bs, ts, heads, HEAD_DIM), v.reshape(bs, ts, heads, HEAD_DIM), conv_s, h_s)
    return yp, ys, out_p, out_s


def _layer_c(layer, l, yp, ys, dims, p):
    bp, tp, bs, ts = dims
    groups = p["cache_cmp_k"].shape[3]
    gw = groups * HEAD_DIM
    w_in = p["w_in_c"][l]
    heads = (w_in.shape[1] - 6 * gw) // (HEAD_DIM + 3)
    qw = heads * HEAD_DIM
    cuts = [0, qw] + [qw + (i + 1) * gw for i in range(6)]
    w_f32 = [w_in[:, a:b] for a, b in zip(cuts[:-1], cuts[1:])]
    w_gt = w_in[:, cuts[-1]:]
    slopes = _alibi_slopes(heads)
    g = p["norm_mix"][layer]
    w_out = [p["w_out_c"][l].astype(BF16)]
    cmp_k = (p["w_cmp_k1"][l], p["w_cmp_k2"][l], p["pe_cmp_k"][l])
    cmp_v = (p["w_cmp_v1"][l], p["w_cmp_v2"][l], p["pe_cmp_v"][l])
    kv4 = lambda a, n, t: a.reshape(n, t, groups, HEAD_DIM)

    ws = ([w.astype(BF16) for w in w_f32] + [_pad_group_cols(w_gt, 3 * heads // groups).astype(BF16)]
          + [_pad_group_cols(w, HEAD_DIM).astype(BF16) for w in w_f32[3:7]])
    outs = [(i, F32) for i in range(8)] + [(i, BF16) for i in range(8, 12)]
    q, kc, vc, ks, vs, kw, vw, gt, ksp, vsp, kwp, vwp = norm_proj(yp, g, ws, outs)
    seq = lambda a: a.reshape(bp, tp, a.shape[-1])
    ck = nsa_compress(seq(kc), *_compress_weights(*cmp_k, groups), True)
    cv = nsa_compress(seq(vc), *_compress_weights(*cmp_v, groups), False)
    cover = _nsa_cover(tp // CMP_STRIDE, max(-(-tp // SEL_BLOCK), SEL_TOPK))
    att = nsa_prompt(seq(q), seq(gt), ck, cv, cover, seq(ksp), seq(vsp), seq(kwp), seq(vwp), slopes)
    yp = proj_residual([att.reshape(bp * tp, qw)], w_out, yp)
    w_keep = min(WINDOW, tp)
    out_p = tuple(kv4(a, bp, tp) for a in (kc, vc, ks, vs)) + tuple(kv4(a, bp, tp)[:, tp - w_keep:] for a in (kw, vw))

    ws = [w.astype(BF16) for w in w_f32] + [jnp.pad(w_gt, ((0, 0), (0, LANES - w_gt.shape[1]))).astype(BF16)]
    q, kc, vc, ks, vs, kw, vw, gt = norm_proj(ys, g, ws, [(i, F32) for i in range(8)])
    q_rows = _heads_to_rows(q, bs, ts, heads)
    q_tiled = jnp.tile(q_rows, (1, 1, groups))
    q_rows = jnp.pad(q_rows, ((0, 0), (0, 0), (0, LANES - HEAD_DIM)))
    slope_rows = jnp.asarray(np.repeat(slopes, ts).reshape(heads * ts, 1))
    gt_rows = gt[:, :3 * heads].reshape(bs, ts, heads, 3).transpose(0, 2, 1, 3).reshape(bs, heads * ts, 3)
    n_pages = p["page_table"].shape[1]
    past = n_pages * PAGE_SIZE
    cover = _nsa_cover(n_pages * _CHUNKS_PER_PAGE, max(-(-(past + ts) // SEL_BLOCK), SEL_TOPK))
    dec_w = []
    for w1, w2, pe in (cmp_k, cmp_v):
        w1j = w1.reshape(CMP_LEN, HEAD_DIM, HEAD_DIM)
        dec_w.append((jnp.concatenate([w1j[:CMP_STRIDE], w1j[CMP_STRIDE:]], axis=2).astype(BF16), pe,
                      jnp.pad(w2, ((0, LANES - HEAD_DIM), (0, LANES - HEAD_DIM))).astype(BF16)))
    (wjk, pek, w2k), (wjv, pev, w2v) = dec_w
    ab_k = nsa_page_ab(_tokens_minor(p["cache_cmp_k"]), l, wjk)
    ab_v = nsa_page_ab(_tokens_minor(p["cache_cmp_v"]), l, wjv)
    o_c, pen = nsa_decode_cmp(q_rows, slope_rows, ab_k, ab_v, p["page_table"], wjk, wjv, pek, pev, w2k, w2v,
                              cover, ts)
    att = nsa_decode_sel(q_tiled, slope_rows, pen, o_c, gt_rows, _new_kv_cols(ks, bs, ts),
                         _new_kv_cols(vs, bs, ts), _new_kv_cols(kw, bs, ts), _new_kv_cols(vw, bs, ts),
                         _tokens_minor(p["state_win_k"]), _tokens_minor(p["state_win_v"]),
                         _tokens_minor(p["cache_sel_k"]), _tokens_minor(p["cache_sel_v"]), p["page_table"], l)
    ys = proj_residual([_rows_to_heads(att, bs, ts, heads)], w_out, ys)
    w_buf = p["state_win_k"].shape[2]
    win = lambda state, new: jnp.concatenate([state[l], kv4(new, bs, ts)], axis=1)[:, -w_buf:]
    out_s = tuple(kv4(a, bs, ts) for a in (kc, vc, ks, vs)) + (win(p["state_win_k"], kw), win(p["state_win_v"], vw))
    return yp, ys, out_p, out_s


def kernel(x_prompt, x_sample, cache_moba_k, cache_moba_v, cache_cmp_k, cache_cmp_v, cache_sel_k, cache_sel_v,
           state_win_k, state_win_v, state_conv, state_rglru, page_table, norm_mix, norm_ffn, norm_final,
           w_in_ab, w_out_ab, conv_w, conv_b, w_rg_a, b_rg_a, w_rg_i, b_rg_i, rg_lambda, w_in_c, w_out_c,
           w_cmp_k1, w_cmp_k2, pe_cmp_k, w_cmp_v1, w_cmp_v2, pe_cmp_v, w_route_group, b_route_group,
           w_route_expert, b_route_expert, w_exp_gate, w_exp_up, w_exp_down):
    p = dict(cache_moba_k=cache_moba_k, cache_moba_v=cache_moba_v, cache_cmp_k=cache_cmp_k,
             cache_cmp_v=cache_cmp_v, cache_sel_k=cache_sel_k, cache_sel_v=cache_sel_v,
             state_win_k=state_win_k, state_win_v=state_win_v, state_conv=state_conv, state_rglru=state_rglru,
             page_table=page_table, norm_mix=norm_mix, w_in_ab=w_in_ab, w_out_ab=w_out_ab, conv_w=conv_w,
             conv_b=conv_b, w_rg_a=w_rg_a, b_rg_a=b_rg_a, w_rg_i=w_rg_i, b_rg_i=b_rg_i, rg_lambda=rg_lambda,
             w_in_c=w_in_c, w_out_c=w_out_c, w_cmp_k1=w_cmp_k1, w_cmp_k2=w_cmp_k2, pe_cmp_k=pe_cmp_k,
             w_cmp_v1=w_cmp_v1, w_cmp_v2=w_cmp_v2, pe_cmp_v=pe_cmp_v)
    bp, tp, d = x_prompt.shape
    bs, ts, _ = x_sample.shape
    dims = (bp, tp, bs, ts)
    yp = x_prompt.reshape(bp * tp, d)
    ys = x_sample.reshape(bs * ts, d)
    ab_p, ab_s, c_p, c_s = [], [], [], []
    for layer in range(norm_mix.shape[0]):
        l = layer // 2
        if layer % 2 == 0:
            yp, ys, o_p, o_s = _layer_ab(layer, l, yp, ys, dims, p)
            ab_p.append(o_p)
            ab_s.append(o_s)
        else:
            yp, ys, o_p, o_s = _layer_c(layer, l, yp, ys, dims, p)
            c_p.append(o_p)
            c_s.append(o_s)
        wr, br = _router_weights(w_route_group[layer], b_route_group[layer], w_route_expert[layer],
                                 b_route_expert[layer])
        moe_w = (w_exp_gate[layer].astype(BF16), w_exp_up[layer].astype(BF16), w_exp_down[layer].astype(BF16))
        yp = hier_moe_residual(yp, norm_ffn[layer], wr, br, *moe_w)
        ys = hier_moe_residual(ys, norm_ffn[layer], wr, br, *moe_w)
    y_prompt = rmsnorm(yp, norm_final).reshape(bp, tp, d)
    y_sample = rmsnorm(ys, norm_final).reshape(bs, ts, d)
    stack = lambda outs, i: jnp.stack([o[i] for o in outs])
    res = [y_prompt, y_sample]
    for i in range(4):
        res += [stack(ab_p, i), stack(ab_s, i)]
    for i in range(6):
        res += [stack(c_p, i), stack(c_s, i)]
    return tuple(res)
```

```python
import functools

import numpy as np
import jax
import jax.numpy as jnp
from jax import lax
from jax.experimental import pallas as pl
from jax.experimental.pallas import tpu as pltpu

F32 = jnp.float32
BF16 = jnp.bfloat16
HIGHEST = lax.Precision.HIGHEST

HEAD_DIM = 64
LANES = 128
SUBLANES = 8
CONV_W = 4
RG_C = 8.0
MOBA_BLOCK = 256
MOBA_TOPK = 3
CMP_STRIDE = 16
CMP_LEN = 32
SEL_BLOCK = 64
SEL_TOPK = 16
WINDOW = 512
N_GROUPS = 4
EXPERTS_PER_GROUP = 8
N_EXPERTS = N_GROUPS * EXPERTS_PER_GROUP
PAGE_SIZE = 128
RMS_EPS = 1e-6
NEG_BIG = -1e30
VMEM_LIMIT = 56 * 1024 * 1024
MOE_BLOCK_ROWS = 256


def _cparams(sem):
    return pltpu.CompilerParams(dimension_semantics=sem, vmem_limit_bytes=VMEM_LIMIT)


def _alibi_slopes(n):
    return np.asarray(2.0 ** (-8.0 * np.arange(1, n + 1) / n), dtype=np.float32)


def _row_tile(m, pref=512):
    return pref if m % pref == 0 else m


def _dot_nt(a, b, precision=None):
    return lax.dot_general(a, b, (((1,), (1,)), ((), ())), precision=precision,
                           preferred_element_type=F32)


def _norm_proj_body(out_map, x_ref, g_ref, *refs):
    n_w = max(wi for wi, _ in out_map) + 1
    w_refs, o_refs = refs[:n_w], refs[n_w:]
    x = x_ref[...]
    y = x * lax.rsqrt(jnp.mean(x * x, axis=-1, keepdims=True) + RMS_EPS) * g_ref[...]
    yb = y.astype(BF16)
    prods = [jnp.dot(yb, w_ref[...], preferred_element_type=F32) for w_ref in w_refs]
    for (wi, feature_major), o_ref in zip(out_map, o_refs):
        if feature_major:
            o_ref[0] = prods[wi].T.astype(o_ref.dtype)
        else:
            o_ref[...] = prods[wi].astype(o_ref.dtype)


def norm_proj(x, g, ws, outs, seq_len=None):
    m, d = x.shape
    tm = _row_tile(m)
    outs = [o if len(o) == 3 else (o[0], o[1], False) for o in outs]
    out_map = tuple((wi, fm) for wi, _, fm in outs)
    tiles = (seq_len or tm) // tm
    out_specs, out_shape = [], []
    for wi, dt, fm in outs:
        cols = ws[wi].shape[1]
        if fm:
            assert seq_len % tm == 0
            out_specs.append(pl.BlockSpec((1, cols, tm), lambda i: (i // tiles, 0, i % tiles)))
            out_shape.append(jax.ShapeDtypeStruct((m // seq_len, cols, seq_len), dt))
        else:
            out_specs.append(pl.BlockSpec((tm, cols), lambda i: (i, 0)))
            out_shape.append(jax.ShapeDtypeStruct((m, cols), dt))
    return pl.pallas_call(
        functools.partial(_norm_proj_body, out_map),
        grid=(m // tm,),
        in_specs=[pl.BlockSpec((tm, d), lambda i: (i, 0)), pl.BlockSpec((1, d), lambda i: (0, 0))]
        + [pl.BlockSpec(w.shape, lambda i: (0, 0)) for w in ws],
        out_specs=out_specs,
        out_shape=out_shape,
        compiler_params=_cparams(("parallel",)),
        name="norm_proj",
    )(x, g.reshape(1, d), *ws)


def _proj_res_body(n_in, *refs):
    x_refs, w_refs = refs[:n_in], refs[n_in:2 * n_in]
    res_ref, o_ref = refs[2 * n_in], refs[2 * n_in + 1]
    acc = jnp.dot(x_refs[0][...].astype(BF16), w_refs[0][...], preferred_element_type=F32)
    for x_ref, w_ref in zip(x_refs[1:], w_refs[1:]):
        acc = acc + jnp.dot(x_ref[...].astype(BF16), w_ref[...], preferred_element_type=F32)
    o_ref[...] = res_ref[...] + acc


def proj_residual(xs, ws, res):
    m, d = res.shape
    tm = _row_tile(m)
    return pl.pallas_call(
        functools.partial(_proj_res_body, len(xs)),
        grid=(m // tm,),
        in_specs=[pl.BlockSpec((tm, x.shape[1]), lambda i: (i, 0)) for x in xs]
        + [pl.BlockSpec(w.shape, lambda i: (0, 0)) for w in ws]
        + [pl.BlockSpec((tm, d), lambda i: (i, 0))],
        out_specs=pl.BlockSpec((tm, d), lambda i: (i, 0)),
        out_shape=jax.ShapeDtypeStruct((m, d), F32),
        compiler_params=_cparams(("parallel",)),
        name="proj_residual",
    )(*xs, *ws, res)


def _rmsnorm_body(x_ref, g_ref, o_ref):
    x = x_ref[...]
    o_ref[...] = x * lax.rsqrt(jnp.mean(x * x, axis=-1, keepdims=True) + RMS_EPS) * g_ref[...]


def rmsnorm(x, g):
    m, d = x.shape
    tm = _row_tile(m)
    return pl.pallas_call(
        _rmsnorm_body,
        grid=(m // tm,),
        in_specs=[pl.BlockSpec((tm, d), lambda i: (i, 0)), pl.BlockSpec((1, d), lambda i: (0, 0))],
        out_specs=pl.BlockSpec((tm, d), lambda i: (i, 0)),
        out_shape=jax.ShapeDtypeStruct((m, d), F32),
        compiler_params=_cparams(("parallel",)),
        name="rmsnorm",
    )(x, g.reshape(1, d))


_XIN_PAD = 8


def _rglru_body(xr_ref, gr_ref, cb_ref, h0_ref, cw_ref, cbias_ref, wa_ref, ba_ref, wi_ref, bi_ref,
                lam_ref, y_ref, tail_ref, hl_ref, xin_s, a_s, b_s, h_s):
    tt = xr_ref.shape[1]
    keep = CONV_W - 1

    @pl.when(pl.program_id(1) == 0)
    def _():
        xin_s[_XIN_PAD - keep:_XIN_PAD, :] = cb_ref[0]
        h_s[...] = h0_ref[0]

    xin_s[_XIN_PAD:_XIN_PAD + tt, :] = xr_ref[0]
    xc = cbias_ref[...]
    for k in range(CONV_W):
        lo = _XIN_PAD - keep + k
        xc = xc + xin_s[lo:lo + tt, :] * cw_ref[k:k + 1, :]
    new_tail = xin_s[_XIN_PAD + tt - keep:_XIN_PAD + tt, :]
    xin_s[_XIN_PAD - keep:_XIN_PAD, :] = new_tail
    tail_ref[0] = new_tail

    xcb = xc.astype(BF16)
    r = jax.nn.sigmoid(jnp.dot(xcb, wa_ref[...], preferred_element_type=F32) + ba_ref[...])
    gi = jax.nn.sigmoid(jnp.dot(xcb, wi_ref[...], preferred_element_type=F32) + bi_ref[...])
    nl = -lam_ref[...]
    softplus = jnp.maximum(nl, 0.0) + jnp.log1p(jnp.exp(-jnp.abs(nl)))
    log_a = (-RG_C) * r * softplus
    a = jnp.exp(log_a)
    a_s[...] = a
    b_s[...] = jnp.sqrt(1.0 - a * a) * (gi * xc)

    def step(t, h):
        h = a_s[pl.ds(t, 1), :] * h + b_s[pl.ds(t, 1), :]
        b_s[pl.ds(t, 1), :] = h
        return h

    h = lax.fori_loop(0, tt, step, h_s[...], unroll=min(tt, 8))
    h_s[...] = h
    hl_ref[0] = h
    y_ref[0] = b_s[...] * jax.nn.gelu(gr_ref[0])


def rglru(xr, gr, conv_buf, h0, conv_w, conv_b, wa_bd, b_a, wi_bd, b_i, lam):
    n, t, w = xr.shape
    tt = min(t, 512)
    assert t % tt == 0 and tt >= CONV_W - 1
    row = lambda v: v.reshape(1, w)
    full = lambda shp: pl.BlockSpec(shp, lambda b, s: (0,) * len(shp))
    y, tail, hl = pl.pallas_call(
        _rglru_body,
        grid=(n, t // tt),
        in_specs=[pl.BlockSpec((1, tt, w), lambda b, s: (b, s, 0)),
                  pl.BlockSpec((1, tt, w), lambda b, s: (b, s, 0)),
                  pl.BlockSpec((1, CONV_W - 1, w), lambda b, s: (b, 0, 0)),
                  pl.BlockSpec((1, 1, w), lambda b, s: (b, 0, 0)),
                  full((CONV_W, w)), full((1, w)), full((w, w)), full((1, w)), full((w, w)),
                  full((1, w)), full((1, w))],
        out_specs=[pl.BlockSpec((1, tt, w), lambda b, s: (b, s, 0)),
                   pl.BlockSpec((1, CONV_W - 1, w), lambda b, s: (b, 0, 0)),
                   pl.BlockSpec((1, 1, w), lambda b, s: (b, 0, 0))],
        out_shape=[jax.ShapeDtypeStruct((n, t, w), F32),
                   jax.ShapeDtypeStruct((n, CONV_W - 1, w), F32),
                   jax.ShapeDtypeStruct((n, 1, w), F32)],
        scratch_shapes=[pltpu.VMEM((_XIN_PAD + tt, w), F32), pltpu.VMEM((tt, w), F32),
                        pltpu.VMEM((tt, w), F32), pltpu.VMEM((1, w), F32)],
        compiler_params=_cparams(("parallel", "arbitrary")),
        name="rglru",
    )(xr, gr, conv_buf, h0.reshape(n, 1, w), conv_w, row(conv_b), wa_bd, row(b_a), wi_bd, row(b_i), row(lam))
    return y, tail, hl.reshape(n, w)


def _block_diag(w):
    h, d, _ = w.shape
    eye = jnp.eye(h, dtype=w.dtype)
    return (eye[:, None, :, None] * w[:, :, None, :]).reshape(h * d, h * d)


def _block_mean_body(blk, k_ref, o_ref):
    x = k_ref[0]
    nb = x.shape[0] // blk
    o_ref[0] = jnp.mean(x.reshape(nb, blk, x.shape[1]), axis=1)


def block_mean(k, blk):
    n, t, w = k.shape
    nb = t // blk
    g = SUBLANES if nb % SUBLANES == 0 else nb
    return pl.pallas_call(
        functools.partial(_block_mean_body, blk),
        grid=(n, nb // g),
        in_specs=[pl.BlockSpec((1, g * blk, w), lambda b, s: (b, s, 0))],
        out_specs=pl.BlockSpec((1, g, w), lambda b, s: (b, s, 0)),
        out_shape=jax.ShapeDtypeStruct((n, nb, w), F32),
        compiler_params=_cparams(("parallel", "parallel")),
        name="block_mean",
    )(k)


def _first_argmax(g, idx_f):
    m = jnp.max(g, axis=-1, keepdims=True)
    return jnp.min(jnp.where(g == m, idx_f, jnp.inf), axis=-1, keepdims=True)


def _split_hi_lo(x):
    hi = x.astype(BF16)
    lo = (x - hi.astype(F32)).astype(BF16)
    return hi, lo


_CHUNK_ROWS = 128


def _row_chunks(x):
    return [x[r:r + _CHUNK_ROWS] for r in range(0, x.shape[0], _CHUNK_ROWS)]


def _flash_init(rows):
    return [(jnp.full((_CHUNK_ROWS, 1), -jnp.inf, F32), jnp.zeros((_CHUNK_ROWS, LANES), F32))
            for _ in range(rows // _CHUNK_ROWS)]


def _causal_chunk_masks(tq, tk):
    r = lax.broadcasted_iota(jnp.int32, (_CHUNK_ROWS, tk), 0)
    c = lax.broadcasted_iota(jnp.int32, (_CHUNK_ROWS, tk), 1)
    return [c <= r + off for off in range(0, tq, _CHUNK_ROWS)]


LOG2E = 1.4426950408889634
_QSCALE = HEAD_DIM ** -0.5 * LOG2E


def _flash_update(s_all, v_ones, carry, masks=None):
    stats, ps = [], []
    for ci, (m, _) in enumerate(carry):
        s = s_all[ci * _CHUNK_ROWS:(ci + 1) * _CHUNK_ROWS]
        if masks is not None:
            s = jnp.where(masks[ci % len(masks)], s, NEG_BIG)
        m_new = jnp.maximum(m, jnp.max(s, axis=-1, keepdims=True))
        ps.append(jnp.exp2(s - m_new).astype(BF16))
        stats.append((m_new, jnp.exp2(m - m_new)))
    pv_all = jnp.dot(jnp.concatenate(ps, axis=0), v_ones, preferred_element_type=F32)
    return [(m_new, alpha * acc + pv_all[ci * _CHUNK_ROWS:(ci + 1) * _CHUNK_ROWS])
            for ci, ((m_new, alpha), (_, acc)) in enumerate(zip(stats, carry))]


def _moba_prompt_body(nbk, slopes_ref, q_ref, km_ref, kb_ref, vb_ref, o_ref):
    p_idx = pl.program_id(1)
    i = pl.program_id(2)
    tq = q_ref.shape[1]
    q = q_ref[0]
    lane = lax.broadcasted_iota(jnp.int32, (tq, LANES), 1)
    lane_f = lane.astype(F32)
    km = km_ref[0]
    halves = []
    for hh in range(2):
        in_head = (lane >= hh * HEAD_DIM) & (lane < (hh + 1) * HEAD_DIM)
        qh = jnp.where(in_head, q, 0.0)
        gate = _dot_nt(qh, km, HIGHEST)
        g = jnp.where(lane < i, gate, -jnp.inf)
        sel = lane == i
        for _ in range(MOBA_TOPK):
            pick = lane_f == _first_argmax(g, lane_f)
            sel = sel | (pick & (lane < i))
            g = jnp.where(pick, -jnp.inf, g)
        slope = slopes_ref[2 * p_idx + hh]
        pb = jnp.where(sel, 0.0, NEG_BIG) + slope * (lane_f * MOBA_BLOCK)
        pb_hi, pb_lo = _split_hi_lo(pb)
        pb_lo = pltpu.roll(pb_lo.astype(F32), nbk, 1)
        s_hi, s_lo = _split_hi_lo(jnp.full((tq, LANES), slope, F32))
        aug = jnp.where(lane < nbk, pb_hi.astype(F32),
                        jnp.where(lane < 2 * nbk, pb_lo,
                                  jnp.where(lane == 2 * nbk, s_hi.astype(F32),
                                            jnp.where(lane == 2 * nbk + 1, s_lo.astype(F32), 0.0))))
        halves.append(jnp.concatenate([(qh * _QSCALE).astype(BF16), aug.astype(BF16)], axis=1))
    qa = jnp.concatenate(halves, axis=0)

    klane = lax.broadcasted_iota(jnp.int32, (MOBA_BLOCK, LANES), 1)
    krow = lax.broadcasted_iota(jnp.int32, (MOBA_BLOCK, LANES), 0).astype(F32)
    ones_lane = (HEAD_DIM, 0)
    causal_masks = _causal_chunk_masks(tq, MOBA_BLOCK)

    def scores(j):
        kt = kb_ref[0, pl.ds(pl.multiple_of(j * MOBA_BLOCK, MOBA_BLOCK), MOBA_BLOCK), :]
        kaug = jnp.where((klane == j) | (klane == nbk + j), 1.0,
                         jnp.where((klane == 2 * nbk) | (klane == 2 * nbk + 1), krow, 0.0)).astype(BF16)
        return _dot_nt(qa, jnp.concatenate([kt, kaug], axis=1))

    def update(j, s, carry, causal):
        vt = vb_ref[0, pl.ds(pl.multiple_of(j * MOBA_BLOCK, MOBA_BLOCK), MOBA_BLOCK), :]
        new = []
        for hh in range(2):
            in_head = (klane >= hh * HEAD_DIM) & (klane < (hh + 1) * HEAD_DIM)
            v_h = jnp.where(in_head, vt, jnp.where(klane == ones_lane[hh], 1.0, 0.0).astype(BF16))
            new.append(_flash_update(s[hh * tq:(hh + 1) * tq], v_h, carry[hh], causal_masks if causal else None))
        return new

    def body(j, c):
        s, carry = c
        return scores(j + 1), update(j, s, carry, False)

    s, carry = lax.fori_loop(0, i, body, (scores(0), [_flash_init(tq), _flash_init(tq)]))
    carry = update(i, s, carry, True)
    outs = [jnp.concatenate([acc / acc[:, ones_lane[hh]:ones_lane[hh] + 1] for _, acc in carry[hh]], axis=0)
            for hh in range(2)]
    o_ref[0] = jnp.where(lane < HEAD_DIM, outs[0], outs[1])


def moba_prompt(q, kb, vb, kmean, slopes):
    n, t, w = q.shape
    nbk = t // MOBA_BLOCK
    assert t % MOBA_BLOCK == 0 and nbk >= MOBA_TOPK and 2 * nbk + 2 <= LANES
    km = jnp.pad(kmean, ((0, 0), (0, LANES - nbk), (0, 0)))
    grid_spec = pltpu.PrefetchScalarGridSpec(
        num_scalar_prefetch=0,
        grid=(n, w // LANES, nbk),
        in_specs=[pl.BlockSpec(memory_space=pltpu.SMEM),
                  pl.BlockSpec((1, MOBA_BLOCK, LANES), lambda b, p, i: (b, i, p)),
                  pl.BlockSpec((1, LANES, LANES), lambda b, p, i: (b, 0, p)),
                  pl.BlockSpec((1, t, LANES), lambda b, p, i: (b, 0, p)),
                  pl.BlockSpec((1, t, LANES), lambda b, p, i: (b, 0, p))],
        out_specs=pl.BlockSpec((1, MOBA_BLOCK, LANES), lambda b, p, i: (b, i, p)),
    )
    return pl.pallas_call(
        functools.partial(_moba_prompt_body, nbk),
        grid_spec=grid_spec,
        out_shape=jax.ShapeDtypeStruct((n, t, w), F32),
        compiler_params=_cparams(("parallel", "parallel", "arbitrary")),
        name="moba_prompt",
    )(jnp.asarray(slopes * np.float32(LOG2E)), q, km, kb, vb)


_ROUTE_E, _ROUTE_W, _ROUTE_RANK = 0, 2, 4


def _router_body(x_ref, g_ref, wr_ref, br_ref, xn_ref, route_ref, cnt_ref, carry_s):
    tm = x_ref.shape[0]

    @pl.when(pl.program_id(0) == 0)
    def _():
        carry_s[...] = jnp.zeros_like(carry_s)

    x = x_ref[...]
    xn = x * lax.rsqrt(jnp.mean(x * x, axis=-1, keepdims=True) + RMS_EPS) * g_ref[...]
    xn_ref[...] = xn
    logits = jnp.dot(xn, wr_ref[...], precision=HIGHEST, preferred_element_type=F32) + br_ref[...]
    lane = lax.broadcasted_iota(jnp.int32, (tm, LANES), 1)
    is_g = lane < N_GROUPS
    gl = jnp.where(is_g, logits, -jnp.inf)
    gmax = jnp.max(gl, axis=-1, keepdims=True)
    g_idx = jnp.min(jnp.where(gl == gmax, lane, LANES), axis=-1, keepdims=True)
    g_w = 1.0 / jnp.sum(jnp.where(is_g, jnp.exp(gl - gmax), 0.0), axis=-1, keepdims=True)
    e_lo = N_GROUPS + g_idx * EXPERTS_PER_GROUP
    el = jnp.where((lane >= e_lo) & (lane < e_lo + EXPERTS_PER_GROUP), logits, -jnp.inf)
    v1 = jnp.max(el, axis=-1, keepdims=True)
    i1 = jnp.min(jnp.where(el == v1, lane, LANES), axis=-1, keepdims=True)
    el2 = jnp.where(lane == i1, -jnp.inf, el)
    v2 = jnp.max(el2, axis=-1, keepdims=True)
    i2 = jnp.min(jnp.where(el2 == v2, lane, LANES), axis=-1, keepdims=True)
    e21 = jnp.exp(v2 - v1)
    p1 = 1.0 / (1.0 + e21)
    w1 = g_w * p1
    w2 = g_w * (e21 * p1)
    hot1 = lane == i1
    hot2 = lane == i2
    cnt = jnp.where(hot1 | hot2, 1.0, 0.0)
    r = lax.broadcasted_iota(jnp.int32, (tm, tm), 0)
    c = lax.broadcasted_iota(jnp.int32, (tm, tm), 1)
    tri = jnp.where(c < r, 1.0, 0.0).astype(BF16)
    before = jnp.dot(tri, cnt.astype(BF16), preferred_element_type=F32) + carry_s[...]
    r1 = jnp.sum(jnp.where(hot1, before, 0.0), axis=-1, keepdims=True)
    r2 = jnp.sum(jnp.where(hot2, before, 0.0), axis=-1, keepdims=True)
    carry_s[...] = carry_s[...] + jnp.sum(cnt, axis=0, keepdims=True)
    cnt_ref[...] = carry_s[...]
    route = jnp.zeros((tm, LANES), F32)
    for pos, val in ((_ROUTE_E, (i1 - N_GROUPS).astype(F32)), (_ROUTE_E + 1, (i2 - N_GROUPS).astype(F32)),
                     (_ROUTE_W, w1), (_ROUTE_W + 1, w2), (_ROUTE_RANK, r1), (_ROUTE_RANK + 1, r2)):
        route = jnp.where(lane == pos, val, route)
    route_ref[...] = route


def moe_router(y, g, wr, br):
    m, d = y.shape
    tm = _row_tile(m, 256)
    return pl.pallas_call(
        _router_body,
        grid=(m // tm,),
        in_specs=[pl.BlockSpec((tm, d), lambda i: (i, 0)), pl.BlockSpec((1, d), lambda i: (0, 0)),
                  pl.BlockSpec((d, LANES), lambda i: (0, 0)), pl.BlockSpec((1, LANES), lambda i: (0, 0))],
        out_specs=[pl.BlockSpec((tm, d), lambda i: (i, 0)), pl.BlockSpec((tm, LANES), lambda i: (i, 0)),
                   pl.BlockSpec((1, LANES), lambda i: (0, 0))],
        out_shape=[jax.ShapeDtypeStruct((m, d), F32), jax.ShapeDtypeStruct((m, LANES), F32),
                   jax.ShapeDtypeStruct((1, LANES), F32)],
        scratch_shapes=[pltpu.VMEM((1, LANES), F32)],
        compiler_params=_cparams(("arbitrary",)),
        name="moe_router",
    )(y, g.reshape(1, d), wr, br)


def _row_copy(src, s_row, dst, d_row, sem):
    return pltpu.make_async_copy(src.at[pl.ds(s_row, 1), :], dst.at[pl.ds(d_row, 1), :], sem)


def _scatter_body(dest_ref, xn_ref, xs_in_ref, xs_ref, sem):
    del xs_in_ref
    tm = xn_ref.shape[0]
    base = pl.program_id(0) * tm

    def issue(r, _):
        for s in range(2):
            _row_copy(xn_ref, r, xs_ref, dest_ref[2 * (base + r) + s], sem).start()
        return 0

    lax.fori_loop(0, tm, issue, 0, unroll=4)
    for _ in range(2):
        pltpu.make_async_copy(xn_ref, xs_ref.at[pl.ds(0, tm), :], sem).wait()


def moe_scatter(dest, xn, rows):
    m, d = xn.shape
    tm = _row_tile(m, 256)
    grid_spec = pltpu.PrefetchScalarGridSpec(
        num_scalar_prefetch=1,
        grid=(m // tm,),
        in_specs=[pl.BlockSpec((tm, d), lambda i, dest: (i, 0)), pl.BlockSpec(memory_space=pl.ANY)],
        out_specs=pl.BlockSpec(memory_space=pl.ANY),
        scratch_shapes=[pltpu.SemaphoreType.DMA(())],
    )
    return pl.pallas_call(
        _scatter_body,
        grid_spec=grid_spec,
        out_shape=jax.ShapeDtypeStruct((rows, d), F32),
        input_output_aliases={2: 0},
        compiler_params=_cparams(("arbitrary",)),
        name="moe_scatter",
    )(dest, xn, jnp.zeros((rows, d), F32))


def _expert_body(blk_e_ref, nused_ref, xs_ref, w1_ref, w3_ref, w2_ref, ys_ref):
    del blk_e_ref
    used = pl.program_id(0) < nused_ref[0]

    @pl.when(used)
    def _():
        x = xs_ref[...].astype(BF16)
        h = jax.nn.silu(jnp.dot(x, w1_ref[0], preferred_element_type=F32))
        h = h * jnp.dot(x, w3_ref[0], preferred_element_type=F32)
        ys_ref[...] = jnp.dot(h.astype(BF16), w2_ref[0], preferred_element_type=F32)

    @pl.when(jnp.logical_not(used))
    def _():
        ys_ref[...] = jnp.zeros_like(ys_ref)


def moe_experts(blk_e, nused, xs, w1, w3, w2):
    rows, d = xs.shape
    te = MOE_BLOCK_ROWS
    de = w1.shape[2]
    grid_spec = pltpu.PrefetchScalarGridSpec(
        num_scalar_prefetch=2,
        grid=(rows // te,),
        in_specs=[pl.BlockSpec((te, d), lambda i, be, nu: (i, 0)),
                  pl.BlockSpec((1, d, de), lambda i, be, nu: (be[i], 0, 0)),
                  pl.BlockSpec((1, d, de), lambda i, be, nu: (be[i], 0, 0)),
                  pl.BlockSpec((1, de, d), lambda i, be, nu: (be[i], 0, 0))],
        out_specs=pl.BlockSpec((te, d), lambda i, be, nu: (i, 0)),
    )
    return pl.pallas_call(
        _expert_body,
        grid_spec=grid_spec,
        out_shape=jax.ShapeDtypeStruct((rows, d), F32),
        compiler_params=_cparams(("arbitrary",)),
        name="moe_experts",
    )(blk_e, nused, xs, w1, w3, w2)


def _combine_body(dest_ref, route_ref, res_ref, ys_ref, o_ref, buf, sem):
    tm = res_ref.shape[0]
    base = pl.program_id(0) * tm

    def issue(r, _):
        for s in range(2):
            _row_copy(ys_ref, dest_ref[2 * (base + r) + s], buf.at[s], r, sem).start()
        return 0

    lax.fori_loop(0, tm, issue, 0, unroll=4)
    for s in range(2):
        pltpu.make_async_copy(ys_ref.at[pl.ds(0, tm), :], buf.at[s], sem).wait()
    route = route_ref[...]
    w1 = route[:, _ROUTE_W:_ROUTE_W + 1]
    w2 = route[:, _ROUTE_W + 1:_ROUTE_W + 2]
    o_ref[...] = res_ref[...] + (w1 * buf[0] + w2 * buf[1])


def moe_combine(dest, route, res, ys):
    m, d = res.shape
    tm = _row_tile(m, 256)
    grid_spec = pltpu.PrefetchScalarGridSpec(
        num_scalar_prefetch=1,
        grid=(m // tm,),
        in_specs=[pl.BlockSpec((tm, LANES), lambda i, dest: (i, 0)),
                  pl.BlockSpec((tm, d), lambda i, dest: (i, 0)),
                  pl.BlockSpec(memory_space=pl.ANY)],
        out_specs=pl.BlockSpec((tm, d), lambda i, dest: (i, 0)),
        scratch_shapes=[pltpu.VMEM((2, tm, d), F32), pltpu.SemaphoreType.DMA(())],
    )
    return pl.pallas_call(
        _combine_body,
        grid_spec=grid_spec,
        out_shape=jax.ShapeDtypeStruct((m, d), F32),
        compiler_params=_cparams(("arbitrary",)),
        name="moe_combine",
    )(dest, route, res, ys)


def hier_moe_residual(y, g, wr, br, w1, w3, w2):
    m, d = y.shape
    te = MOE_BLOCK_ROWS
    xn, route, cnt = moe_router(y, g, wr, br)
    counts = cnt[0, N_GROUPS:N_GROUPS + N_EXPERTS].astype(jnp.int32)
    padded = (counts + te - 1) // te * te
    pend = jnp.cumsum(padded)
    pstart = pend - padded
    e = route[:, _ROUTE_E:_ROUTE_E + 2].astype(jnp.int32)
    rank = route[:, _ROUTE_RANK:_ROUTE_RANK + 2].astype(jnp.int32)
    dest = (pstart[e] + rank).reshape(-1)
    n_blk = -(-(2 * m + N_EXPERTS * (te - 1)) // te)
    blk_start = jnp.arange(n_blk, dtype=jnp.int32) * te
    blk_e = jnp.minimum(jnp.sum((pend[None, :] <= blk_start[:, None]).astype(jnp.int32), axis=1), N_EXPERTS - 1)
    nused = (pend[-1:] // te).astype(jnp.int32)
    xs = moe_scatter(dest, xn, n_blk * te)
    ys = moe_experts(blk_e, nused, xs, w1, w3, w2)
    return moe_combine(dest, route, y, ys)


def _router_weights(w_rg, b_rg, w_re, b_re):
    d = w_rg.shape[0]
    pad = LANES - N_GROUPS - N_EXPERTS
    wr = jnp.concatenate([w_rg, w_re, jnp.zeros((d, pad), F32)], axis=1)
    br = jnp.concatenate([b_rg, b_re, jnp.zeros((pad,), F32)]).reshape(1, LANES)
    return wr, br


_POS_LANE = HEAD_DIM
_TILE_BIAS_LANE = _POS_LANE + 4


def _compress_tail(ab, bias, w2_ref, add_pos):
    c = ab.shape[0] - SUBLANES
    gw = ab.shape[1] // 2
    h = ab[0:c, 0:gw] + ab[1:c + 1, gw:2 * gw] + bias
    out = jnp.dot(jax.nn.gelu(h).astype(BF16), w2_ref[...], preferred_element_type=F32)
    if add_pos:
        lane = lax.broadcasted_iota(jnp.int32, out.shape, 1) % LANES
        cend = lax.broadcasted_iota(jnp.int32, out.shape, 0) * CMP_STRIDE + (CMP_LEN - 1)
        hi = ((cend // 256) * 256).astype(F32)
        lo = (cend % 256).astype(F32)
        out = jnp.where((lane == _POS_LANE) | (lane == _POS_LANE + 1), hi,
                        jnp.where((lane == _POS_LANE + 2) | (lane == _POS_LANE + 3), lo, out))
    return out


def _compress_body(add_pos, n_k, *refs):
    k_refs = refs[:n_k]
    wab_ref, pe_ref, w2_ref, o_ref, ab_s = refs[n_k:]
    c = k_refs[0].shape[1] // CMP_STRIDE
    gw = n_k * LANES
    acc = jnp.zeros((c + SUBLANES, 2 * gw), F32)
    for j in range(CMP_STRIDE):
        xk = jnp.concatenate([r[0, pl.ds(j, c, stride=CMP_STRIDE), :] for r in k_refs], axis=1)
        xj = jnp.concatenate([xk, pe_ref[j:j + 1, :], pe_ref[CMP_STRIDE + j:CMP_STRIDE + j + 1, :],
                              jnp.zeros((SUBLANES - 2, gw), F32)], axis=0).astype(BF16)
        acc = acc + jnp.dot(xj, wab_ref[j], preferred_element_type=F32)
    bias = acc[c:c + 1, 0:gw] + acc[c + 1:c + 2, gw:2 * gw]
    ab_s[0:c, :] = acc[0:c]
    ab_s[c:c + SUBLANES, :] = jnp.zeros((SUBLANES, 2 * gw), F32)
    o_ref[0] = _compress_tail(ab_s, bias, w2_ref, add_pos).astype(o_ref.dtype)


def _compress_weights(w1, w2, pe, groups):
    d = w2.shape[0]
    w1j = w1.reshape(CMP_LEN, d, d)
    eye = jnp.eye(groups, dtype=F32)
    bd = (eye[None, :, None, :, None] * w1j[:, None, :, None, :]).reshape(CMP_LEN, groups * d, groups * d)
    wab = jnp.concatenate([bd[:CMP_STRIDE], bd[CMP_STRIDE:]], axis=2).astype(BF16)
    pe_t = jnp.tile(pe, (1, groups))
    w2p = jnp.concatenate([w2, jnp.zeros((d, LANES - d), F32)], axis=1)
    w2bd = (eye[:, None, :, None] * w2p[None, :, None, :]).reshape(groups * d, groups * LANES).astype(BF16)
    return wab, pe_t, w2bd


def nsa_compress(k, wab, pe_t, w2bd, add_pos):
    n, t, gw = k.shape
    c = t // CMP_STRIDE
    n_k = gw // LANES
    full = lambda shp: pl.BlockSpec(shp, lambda b: (0,) * len(shp))
    return pl.pallas_call(
        functools.partial(_compress_body, add_pos, n_k),
        grid=(n,),
        in_specs=[pl.BlockSpec((1, t, LANES), functools.partial(lambda u, b: (b, 0, u), u)) for u in range(n_k)]
        + [full(wab.shape), full(pe_t.shape), full(w2bd.shape)],
        out_specs=pl.BlockSpec((1, c, w2bd.shape[1]), lambda b: (b, 0, 0)),
        out_shape=jax.ShapeDtypeStruct((n, c, w2bd.shape[1]), BF16),
        scratch_shapes=[pltpu.VMEM((c + SUBLANES, 2 * gw), F32)],
        compiler_params=_cparams(("parallel",)),
        name="nsa_compress",
    )(*([k] * n_k), wab, pe_t, w2bd)


def _stack_queries(q, slopes, lane):
    rows = []
    for r, slope in enumerate(slopes):
        q2 = q[:, (r // 2) * LANES:(r // 2 + 1) * LANES]
        if r % 2:
            q2 = pltpu.roll(q2, HEAD_DIM, 1)
        s_hi, s_lo = _split_hi_lo(jnp.full(q2.shape, slope, F32))
        odd = ((lane - _POS_LANE) % 2) == 1
        feat = jnp.where(odd, s_lo.astype(F32), s_hi.astype(F32))
        rows.append(jnp.where(lane < HEAD_DIM, q2 * _QSCALE,
                              jnp.where(lane < _POS_LANE + 4, feat,
                                        jnp.where(lane == _TILE_BIAS_LANE, 1.0, 0.0))).astype(BF16))
    return jnp.concatenate(rows, axis=0)


def _select_blocks(imp, qb, lane):
    vis = lane <= qb
    forced = (lane == 0) | (lane == qb) | (lane == qb - 1)
    score = jnp.where(vis & jnp.logical_not(forced), imp, -jnp.inf)
    lane_f = lane.astype(F32)
    sel = forced
    for _ in range(SEL_TOPK - 3):
        pick = lane_f == _first_argmax(score, lane_f)
        sel = sel | pick
        score = jnp.where(pick, -jnp.inf, score)
    return sel & vis


def _nsa_prompt_body(slopes_ref, q_ref, gt_ref, ck_ref, cv_ref, cover_ref, ks_ref, vs_ref, kw_ref, vw_ref,
                     o_ref):
    g = pl.program_id(1)
    i = pl.program_id(2)
    tq = q_ref.shape[1]
    r_heads = q_ref.shape[2] // HEAD_DIM
    rows = r_heads * tq
    t0 = i * tq
    lane = lax.broadcasted_iota(jnp.int32, (tq, LANES), 1)
    slopes = [slopes_ref[r_heads * g + r] for r in range(r_heads)]
    cpt = tq // _CHUNK_ROWS
    qa_chunks = _row_chunks(_stack_queries(q_ref[0], slopes, lane))

    ck = ck_ref[0]
    cv = cv_ref[0]
    cend = lax.broadcasted_iota(jnp.int32, (_CHUNK_ROWS, ck.shape[0]), 1) * CMP_STRIDE + (CMP_LEN - 1)
    crow = lax.broadcasted_iota(jnp.int32, (_CHUNK_ROWS, ck.shape[0]), 0)
    o_c, psum = [], [None] * cpt
    for ci, qc in enumerate(qa_chunks):
        ok_c = cend <= t0 + (ci % cpt) * _CHUNK_ROWS + crow
        s_c = jnp.where(ok_c, _dot_nt(qc, ck), NEG_BIG)
        e_c = jnp.where(ok_c, jnp.exp2(s_c - jnp.max(s_c, axis=-1, keepdims=True)), 0.0)
        p_c = e_c / jnp.maximum(jnp.sum(e_c, axis=-1, keepdims=True), 1e-30)
        o_c.append(jnp.dot(p_c.astype(BF16), cv, preferred_element_type=F32))
        psum[ci % cpt] = p_c if psum[ci % cpt] is None else psum[ci % cpt] + p_c

    imp = jnp.dot(jnp.concatenate(psum, axis=0), cover_ref[...], precision=HIGHEST,
                  preferred_element_type=F32)
    qb = (t0 + lax.broadcasted_iota(jnp.int32, (tq, LANES), 0)) // SEL_BLOCK
    pen_chunks = _row_chunks(jnp.where(_select_blocks(imp, qb, lane), 0.0, NEG_BIG).astype(BF16))
    qa = jnp.concatenate(qa_chunks, axis=0)
    qs = jnp.concatenate([qa, jnp.concatenate([pen_chunks[ci % cpt] for ci in range(len(qa_chunks))], axis=0)],
                         axis=1)

    tk = tq
    klane = lax.broadcasted_iota(jnp.int32, (tk, LANES), 1)
    krow = lax.broadcasted_iota(jnp.int32, (tk, LANES), 0)
    krow_b = krow.astype(F32).astype(BF16)
    is_pos = (klane == _POS_LANE) | (klane == _POS_LANE + 1)
    is_off = (klane == _POS_LANE + 2) | (klane == _POS_LANE + 3)
    blocks_per_tile = tk // SEL_BLOCK
    one_b = jnp.ones((tk, LANES), BF16)
    causal_masks = _causal_chunk_masks(tq, tk)

    def keys_with_pos(k_ref, j):
        kt = k_ref[0, pl.ds(pl.multiple_of(j * tk, tk), tk), :]
        off_b = jnp.full((tk, LANES), ((j - i) * tk).astype(F32), F32).astype(BF16)
        return jnp.where(is_pos, krow_b, jnp.where(is_off, off_b, kt))

    def values_with_ones(v_ref, j):
        return jnp.where(klane == _POS_LANE, one_b, v_ref[0, pl.ds(pl.multiple_of(j * tk, tk), tk), :])

    def sel_scores(j):
        onehot = jnp.where(klane == j * blocks_per_tile + krow // SEL_BLOCK, 1.0, 0.0).astype(BF16)
        return _dot_nt(qs, jnp.concatenate([keys_with_pos(ks_ref, j), onehot], axis=1))

    def sel_body(j, c):
        s, carry = c
        return sel_scores(j + 1), _flash_update(s, values_with_ones(vs_ref, j), carry)

    def finish(carry):
        return [acc / acc[:, _POS_LANE:_POS_LANE + 1] for _, acc in carry]

    s, carry = lax.fori_loop(0, i, sel_body, (sel_scores(0), _flash_init(rows)))
    o_s = finish(_flash_update(s, values_with_ones(vs_ref, i), carry, causal_masks))

    n_win = WINDOW // tk + 1
    kws, vws = [], []
    for w in range(n_win):
        j_true = i - (n_win - 1) + w
        j = jnp.maximum(j_true, 0)
        tile_bias = jnp.full((tk, LANES), jnp.where(j_true >= 0, 0.0, NEG_BIG), F32).astype(BF16)
        kws.append(jnp.where(klane == _TILE_BIAS_LANE, tile_bias, keys_with_pos(kw_ref, j)))
        vws.append(values_with_ones(vw_ref, j))
    s_w = _dot_nt(qa, jnp.concatenate(kws, axis=0))
    wr = lax.broadcasted_iota(jnp.int32, (_CHUNK_ROWS, n_win * tk), 0)
    wc = lax.broadcasted_iota(jnp.int32, (_CHUNK_ROWS, n_win * tk), 1)
    ps = []
    for ci in range(rows // _CHUNK_ROWS):
        r_off = wr + (ci % cpt) * _CHUNK_ROWS
        newest = (n_win - 1) * tk
        ok = ((wc < tk) & (wc > r_off)) | ((wc >= tk) & (wc < newest)) | ((wc >= newest) & (wc - newest <= r_off))
        s = jnp.where(ok, s_w[ci * _CHUNK_ROWS:(ci + 1) * _CHUNK_ROWS], NEG_BIG)
        ps.append(jnp.exp2(s - jnp.max(s, axis=-1, keepdims=True)).astype(BF16))
    acc_w = jnp.dot(jnp.concatenate(ps, axis=0), jnp.concatenate(vws, axis=0), preferred_element_type=F32)
    o_w = finish([(None, a) for a in _row_chunks(acc_w)])

    gates = _row_chunks(jax.nn.sigmoid(gt_ref[0]))
    outs = []
    for r in range(r_heads):
        parts = []
        for u in range(cpt):
            ci = r * cpt + u
            gr = [gates[u][:, 3 * r + b:3 * r + b + 1] for b in range(3)]
            parts.append(gr[0] * o_c[ci] + gr[1] * o_s[ci] + gr[2] * o_w[ci])
        outs.append(jnp.concatenate(parts, axis=0))
    lane_o = lax.broadcasted_iota(jnp.int32, (tq, LANES), 1)
    pairs = [jnp.where(lane_o < HEAD_DIM, outs[2 * u], pltpu.roll(outs[2 * u + 1], HEAD_DIM, 1))
             for u in range(r_heads // 2)]
    o_ref[0] = jnp.concatenate(pairs, axis=1)


def nsa_prompt(q, gt, ck, cv, cover, ks, vs, kw, vw, slopes):
    n, t, w = q.shape
    groups = ks.shape[2] // LANES
    qw = w // groups
    tq = 256
    c = ck.shape[1]
    assert t % tq == 0 and -(-t // SEL_BLOCK) <= LANES and t // SEL_BLOCK >= SEL_TOPK and WINDOW % tq == 0
    kv_spec = lambda rows: pl.BlockSpec((1, rows, LANES), lambda b, g, i: (b, 0, g))
    grid_spec = pltpu.PrefetchScalarGridSpec(
        num_scalar_prefetch=0,
        grid=(n, groups, t // tq),
        in_specs=[pl.BlockSpec(memory_space=pltpu.SMEM),
                  pl.BlockSpec((1, tq, qw), lambda b, g, i: (b, i, g)),
                  pl.BlockSpec((1, tq, LANES), lambda b, g, i: (b, i, g)),
                  kv_spec(c), kv_spec(c),
                  pl.BlockSpec(cover.shape, lambda b, g, i: (0, 0)),
                  kv_spec(t), kv_spec(t), kv_spec(t), kv_spec(t)],
        out_specs=pl.BlockSpec((1, tq, qw), lambda b, g, i: (b, i, g)),
    )
    return pl.pallas_call(
        _nsa_prompt_body,
        grid_spec=grid_spec,
        out_shape=jax.ShapeDtypeStruct((n, t, w), F32),
        compiler_params=_cparams(("parallel", "parallel", "arbitrary")),
        name="nsa_prompt",
    )(jnp.asarray(slopes * np.float32(LOG2E)), q, gt, ck, cv, cover, ks, vs, kw, vw)


def _nsa_cover(n_cmp_rows, n_sel):
    i = np.arange(n_cmp_rows)[:, None]
    j = np.arange(-(-n_sel // LANES) * LANES)[None, :]
    cov = (CMP_STRIDE * i < SEL_BLOCK * (j + 1)) & (CMP_STRIDE * i + CMP_LEN > SEL_BLOCK * j) & (j < n_sel)
    return jnp.asarray(cov.astype(np.float32))


def _pad_group_cols(w, width):
    d, gw = w.shape
    g = gw // width
    return jnp.pad(w.reshape(d, g, width), ((0, 0), (0, 0), (0, LANES - width))).reshape(d, g * LANES)


def _own_head_block(rows, width, nq):
    r = lax.broadcasted_iota(jnp.int32, (rows, width), 0)
    c = lax.broadcasted_iota(jnp.int32, (rows, width), 1)
    return (c // HEAD_DIM) == (r // nq)


def _fold_heads(x, own):
    width = x.shape[1]
    fold = jnp.where(lax.broadcasted_iota(jnp.int32, (width, HEAD_DIM), 0) % HEAD_DIM
                     == lax.broadcasted_iota(jnp.int32, (width, HEAD_DIM), 1), 1.0, 0.0)
    return jnp.dot(jnp.where(own, x, 0.0), fold, precision=HIGHEST, preferred_element_type=F32)


_MOBA_BLOCKS_PER_STEP = 4


def _moba_decode_body(nblk, past, pt_ref, q_ref, slope_ref, kn_ref, vn_ref, *refs):
    del pt_ref
    n_page_refs = _MOBA_BLOCKS_PER_STEP * (MOBA_BLOCK // PAGE_SIZE)
    k_refs, v_refs = refs[:n_page_refs], refs[n_page_refs:2 * n_page_refs]
    o_ref, m_s, l_s, g_s, acc_s = refs[2 * n_page_refs:]
    j = pl.program_id(1)
    rows, width = q_ref.shape[1], q_ref.shape[2]
    nq = rows // (width // HEAD_DIM)
    own = _own_head_block(rows, width, nq)
    q_hi, q_lo = _split_hi_lo(jnp.where(own, q_ref[0] * HEAD_DIM ** -0.5, 0.0))
    slope = slope_ref[...]
    lane = lax.broadcasted_iota(jnp.int32, (rows, LANES), 1)
    lane_f = lane.astype(F32)

    @pl.when(j == 0)
    def _():
        m_s[...] = jnp.full(m_s.shape, NEG_BIG, F32)
        l_s[...] = jnp.zeros(l_s.shape, F32)
        g_s[...] = jnp.zeros(g_s.shape, F32)

    lhs = jnp.concatenate([q_hi, q_lo], axis=0)
    pages_per_blk = MOBA_BLOCK // PAGE_SIZE
    span = jnp.concatenate([lane_f + pg * PAGE_SIZE for pg in range(pages_per_blk)], axis=1)
    for bk in range(_MOBA_BLOCKS_PER_STEP):
        jb = j * _MOBA_BLOCKS_PER_STEP + bk
        pages = slice(bk * pages_per_blk, (bk + 1) * pages_per_blk)
        kt = jnp.concatenate([r[0, 0].reshape(width, PAGE_SIZE) for r in k_refs[pages]], axis=1).astype(BF16)
        vt = jnp.concatenate([r[0, 0].reshape(width, PAGE_SIZE) for r in v_refs[pages]], axis=1).astype(BF16)
        s2 = jnp.dot(lhs, kt, preferred_element_type=F32)
        gate = jnp.sum(s2[:rows] + s2[rows:], axis=-1, keepdims=True)
        s = s2[:rows] + slope * (span + (jb * MOBA_BLOCK - past).astype(F32))
        m_j = jnp.max(s, axis=-1, keepdims=True)
        p = jnp.exp(s - m_j)
        m_s[...] = jnp.where(lane == jb, m_j, m_s[...])
        l_s[...] = jnp.where(lane == jb, jnp.sum(p, axis=-1, keepdims=True), l_s[...])
        g_s[...] = jnp.where(lane == jb, gate, g_s[...])
        acc_s[jb] = _dot_nt(p.astype(BF16), vt)

    @pl.when(j == nblk // _MOBA_BLOCKS_PER_STEP - 1)
    def _():
        g = jnp.where(lane < nblk, g_s[...], -jnp.inf)
        sel = jnp.zeros(g.shape, jnp.bool_)
        for _ in range(MOBA_TOPK):
            pick = lane_f == _first_argmax(g, lane_f)
            sel = sel | pick
            g = jnp.where(pick, -jnp.inf, g)
        sel = sel & (lane < nblk)
        ok_o = (lane < nq) & (lane <= lax.broadcasted_iota(jnp.int32, (rows, LANES), 0) % nq)
        s_o = jnp.dot(q_hi, kn_ref[0].astype(BF16), preferred_element_type=F32) + slope * lane_f
        s_o = jnp.where(ok_o, s_o, NEG_BIG)
        m_o = jnp.max(s_o, axis=-1, keepdims=True)
        p_o = jnp.exp(s_o - m_o)
        l_o = jnp.sum(p_o, axis=-1, keepdims=True)
        acc_o = _dot_nt(p_o.astype(BF16), vn_ref[0].astype(BF16))
        m_all = jnp.maximum(jnp.max(jnp.where(sel, m_s[...], -jnp.inf), axis=-1, keepdims=True), m_o)
        w = jnp.where(sel, jnp.exp(m_s[...] - m_all), 0.0)
        w_o = jnp.exp(m_o - m_all)
        den = jnp.sum(w * l_s[...], axis=-1, keepdims=True) + w_o * l_o
        num = w_o * acc_o
        for jj in range(nblk):
            num = num + w[:, jj:jj + 1] * acc_s[jj]
        o_ref[0] = _fold_heads(num, own) / den


def moba_decode(q_tiled, kn_t, vn_t, cache_k, cache_v, page_table, layer, slopes):
    b, rows, width = q_tiled.shape
    heads = cache_k.shape[2]
    nq = rows // heads
    n_pages = page_table.shape[1]
    past = n_pages * PAGE_SIZE
    pages_per_blk = MOBA_BLOCK // PAGE_SIZE
    nblk = past // MOBA_BLOCK
    pages_per_step = _MOBA_BLOCKS_PER_STEP * pages_per_blk
    assert past % MOBA_BLOCK == 0 and MOBA_TOPK <= nblk <= LANES and nblk % _MOBA_BLOCKS_PER_STEP == 0
    slope_rows = jnp.asarray(np.repeat(slopes, nq).reshape(rows, 1))
    page = lambda pg: pl.BlockSpec((1, 1, heads, HEAD_DIM, PAGE_SIZE),
                                   lambda s, j, pt: (layer, pt[s, pages_per_step * j + pg], 0, 0, 0))
    pages = [page(pg) for pg in range(pages_per_step)]
    per_seq = lambda a: pl.BlockSpec((1,) + a.shape[1:], lambda s, j, pt: (s, 0, 0))
    grid_spec = pltpu.PrefetchScalarGridSpec(
        num_scalar_prefetch=1,
        grid=(b, nblk // _MOBA_BLOCKS_PER_STEP),
        in_specs=[per_seq(q_tiled), pl.BlockSpec((rows, 1), lambda s, j, pt: (0, 0)), per_seq(kn_t), per_seq(vn_t)]
        + pages + pages,
        out_specs=pl.BlockSpec((1, rows, HEAD_DIM), lambda s, j, pt: (s, 0, 0)),
        scratch_shapes=[pltpu.VMEM((rows, LANES), F32), pltpu.VMEM((rows, LANES), F32),
                        pltpu.VMEM((rows, LANES), F32), pltpu.VMEM((nblk, rows, width), F32)],
    )
    return pl.pallas_call(
        functools.partial(_moba_decode_body, nblk, past),
        grid_spec=grid_spec,
        out_shape=jax.ShapeDtypeStruct((b, rows, HEAD_DIM), F32),
        compiler_params=_cparams(("parallel", "arbitrary")),
        name="moba_decode",
    )(page_table, q_tiled, slope_rows, kn_t, vn_t, *([cache_k] * pages_per_step), *([cache_v] * pages_per_step))


_CHUNKS_PER_PAGE = PAGE_SIZE // CMP_STRIDE


def _page_ab_body(k_ref, wj_ref, o_ref, xs):
    pages, groups = k_ref.shape[1], k_ref.shape[2]
    for g in range(groups):
        xs[...] = jnp.swapaxes(k_ref[0, :, g], 1, 2)
        acc = jnp.zeros((pages * _CHUNKS_PER_PAGE, LANES), F32)
        for j in range(CMP_STRIDE):
            x = xs[:, pl.ds(j, _CHUNKS_PER_PAGE, stride=CMP_STRIDE), :]
            x = x.reshape(pages * _CHUNKS_PER_PAGE, HEAD_DIM).astype(BF16)
            acc = acc + jnp.dot(x, wj_ref[j], preferred_element_type=F32)
        o_ref[:, g * LANES:(g + 1) * LANES] = acc


def nsa_page_ab(cache, layer, wj):
    _, n_phys, groups, _, _ = cache.shape
    pages = next(p for p in (32, 16, 8, 4, 2, 1) if n_phys % p == 0)
    return pl.pallas_call(
        _page_ab_body,
        grid=(n_phys // pages,),
        in_specs=[pl.BlockSpec((1, pages, groups, HEAD_DIM, PAGE_SIZE), lambda i: (layer, i, 0, 0, 0)),
                  pl.BlockSpec(wj.shape, lambda i: (0, 0, 0))],
        out_specs=pl.BlockSpec((pages * _CHUNKS_PER_PAGE, groups * LANES), lambda i: (i, 0)),
        out_shape=jax.ShapeDtypeStruct((n_phys * _CHUNKS_PER_PAGE, groups * LANES), F32),
        scratch_shapes=[pltpu.VMEM((pages, PAGE_SIZE, HEAD_DIM), F32)],
        compiler_params=_cparams(("parallel",)),
        name="nsa_page_ab",
    )(cache, wj)


def _page_ab_copy(ab_hbm, pt_ref, s, p, buf, sem):
    return pltpu.make_async_copy(ab_hbm.at[pt_ref[s, p]], buf.at[pl.ds(p * _CHUNKS_PER_PAGE, _CHUNKS_PER_PAGE), :], sem)


def _decode_queries(q_ref, slope_ref):
    q = q_ref[0]
    lane = lax.broadcasted_iota(jnp.int32, q.shape, 1)
    s_hi, s_lo = _split_hi_lo(jnp.broadcast_to(slope_ref[...], q.shape))
    feat = jnp.where(((lane - _POS_LANE) % 2) == 1, s_lo.astype(F32), s_hi.astype(F32))
    return jnp.where(lane < HEAD_DIM, q * HEAD_DIM ** -0.5,
                     jnp.where(lane < _POS_LANE + 4, feat, 0.0)).astype(BF16)


def _nsa_decode_cmp_body(n_pages, past, pt_ref, q_ref, slope_ref, abk_ref, abv_ref, wjk_ref, wjv_ref, pek_ref,
                         pev_ref, w2k_ref, w2v_ref, cover_ref, oc_ref, pen_ref, bufk, bufv, sem):
    s = pl.program_id(0)
    c = n_pages * _CHUNKS_PER_PAGE
    rows = q_ref.shape[1]
    groups = abk_ref.shape[2] // LANES
    rpg = rows // groups
    nq = pen_ref.shape[2] // 2

    def issue(p, _):
        _page_ab_copy(abk_ref, pt_ref, s, p, bufk, sem.at[0]).start()
        _page_ab_copy(abv_ref, pt_ref, s, p, bufv, sem.at[1]).start()
        return 0

    def drain(p, _):
        _page_ab_copy(abk_ref, pt_ref, s, p, bufk, sem.at[0]).wait()
        _page_ab_copy(abv_ref, pt_ref, s, p, bufv, sem.at[1]).wait()
        return 0

    lax.fori_loop(0, n_pages, issue, 0)
    pad = jnp.zeros((SUBLANES, bufk.shape[1]), F32)
    bufk[c:c + SUBLANES, :] = pad
    bufv[c:c + SUBLANES, :] = pad
    lax.fori_loop(0, n_pages, drain, 0)

    def tokens(buf, wj_ref, pe_ref, w2_ref, g, add_pos):
        acc = jnp.zeros((SUBLANES, LANES), F32)
        for j in range(CMP_STRIDE):
            xj = jnp.concatenate([pe_ref[j:j + 1, :], pe_ref[CMP_STRIDE + j:CMP_STRIDE + j + 1, :],
                                  jnp.zeros((SUBLANES - 2, HEAD_DIM), F32)], axis=0).astype(BF16)
            acc = acc + jnp.dot(xj, wj_ref[j], preferred_element_type=F32)
        bias = acc[0:1] + pltpu.roll(acc, HEAD_DIM, 1)[1:2]
        slab = slice(g * LANES, (g + 1) * LANES)
        h = buf[0:c, slab] + pltpu.roll(buf[1:c + 1, slab], HEAD_DIM, 1) + bias
        out = jnp.dot(jax.nn.gelu(h).astype(BF16), w2_ref[...], preferred_element_type=F32)
        if add_pos:
            lane = lax.broadcasted_iota(jnp.int32, out.shape, 1)
            cend = lax.broadcasted_iota(jnp.int32, out.shape, 0) * CMP_STRIDE + (CMP_LEN - 1)
            hi = ((cend // 256) * 256).astype(F32)
            lo = (cend % 256).astype(F32)
            out = jnp.where((lane == _POS_LANE) | (lane == _POS_LANE + 1), hi,
                            jnp.where((lane == _POS_LANE + 2) | (lane == _POS_LANE + 3), lo, out))
        return out.astype(BF16)

    qa = _decode_queries(q_ref, slope_ref)
    for g in range(groups):
        ck = tokens(bufk, wjk_ref, pek_ref, w2k_ref, g, True)
        cv = tokens(bufv, wjv_ref, pev_ref, w2v_ref, g, False)
        s_c = _dot_nt(qa[g * rpg:(g + 1) * rpg], ck)
        cend = lax.broadcasted_iota(jnp.int32, s_c.shape, 1) * CMP_STRIDE + (CMP_LEN - 1)
        t_q = past + lax.broadcasted_iota(jnp.int32, s_c.shape, 0) % nq
        ok = cend <= t_q
        s_c = jnp.where(ok, s_c, NEG_BIG)
        e_c = jnp.where(ok, jnp.exp(s_c - jnp.max(s_c, axis=-1, keepdims=True)), 0.0)
        p_c = e_c / jnp.maximum(jnp.sum(e_c, axis=-1, keepdims=True), 1e-30)
        oc_ref[0, g * rpg:(g + 1) * rpg, :] = jnp.dot(p_c.astype(BF16), cv, preferred_element_type=F32)
        half = p_c[0:2 * nq]
        for u in range(1, rpg // (2 * nq)):
            half = half + p_c[2 * nq * u:2 * nq * (u + 1)]
        psum = half + pltpu.roll(half, nq, 0)
        imp = jnp.dot(psum, cover_ref[...], precision=HIGHEST, preferred_element_type=F32)
        lane = lax.broadcasted_iota(jnp.int32, imp.shape, 1)
        qb = (past + lax.broadcasted_iota(jnp.int32, imp.shape, 0) % nq) // SEL_BLOCK
        pen_ref[0, g] = jnp.where(_select_blocks(imp, qb, lane), 0.0, NEG_BIG)


def nsa_decode_cmp(q_rows, slope_rows, ab_k, ab_v, page_table, wjk, wjv, pek, pev, w2k, w2v, cover, nq):
    b, rows, _ = q_rows.shape
    n_pages = page_table.shape[1]
    past = n_pages * PAGE_SIZE
    groups = ab_k.shape[1] // LANES
    c = n_pages * _CHUNKS_PER_PAGE
    assert past % CMP_STRIDE == 0 and nq < CMP_STRIDE and 2 * nq == SUBLANES
    ab_k3 = ab_k.reshape(-1, _CHUNKS_PER_PAGE, groups * LANES)
    ab_v3 = ab_v.reshape(-1, _CHUNKS_PER_PAGE, groups * LANES)
    full = lambda a: pl.BlockSpec(a.shape, lambda s, pt: (0,) * a.ndim)
    grid_spec = pltpu.PrefetchScalarGridSpec(
        num_scalar_prefetch=1,
        grid=(b,),
        in_specs=[pl.BlockSpec((1, rows, LANES), lambda s, pt: (s, 0, 0)), full(slope_rows),
                  pl.BlockSpec(memory_space=pl.ANY), pl.BlockSpec(memory_space=pl.ANY),
                  full(wjk), full(wjv), full(pek), full(pev), full(w2k), full(w2v), full(cover)],
        out_specs=[pl.BlockSpec((1, rows, LANES), lambda s, pt: (s, 0, 0)),
                   pl.BlockSpec((1, groups, 2 * nq, cover.shape[1]), lambda s, pt: (s, 0, 0, 0))],
        scratch_shapes=[pltpu.VMEM((c + SUBLANES, groups * LANES), F32),
                        pltpu.VMEM((c + SUBLANES, groups * LANES), F32),
                        pltpu.SemaphoreType.DMA((2,))],
    )
    return pl.pallas_call(
        functools.partial(_nsa_decode_cmp_body, n_pages, past),
        grid_spec=grid_spec,
        out_shape=[jax.ShapeDtypeStruct((b, rows, LANES), F32),
                   jax.ShapeDtypeStruct((b, groups, 2 * nq, cover.shape[1]), F32)],
        compiler_params=_cparams(("arbitrary",)),
        name="nsa_decode_cmp",
    )(page_table, q_rows, slope_rows, ab_k3, ab_v3, wjk, wjv, pek, pev, w2k, w2v, cover)


_SEL_PAGES_PER_STEP = 8


def _nsa_decode_sel_body(n_steps, past, wbuf, pt_ref, q_ref, slope_ref, pen_ref, oc_ref, gt_ref, ksn_ref, vsn_ref,
                         kwn_ref, vwn_ref, wk_ref, wv_ref, *refs):
    del pt_ref
    pp = _SEL_PAGES_PER_STEP
    k_refs, v_refs = refs[:pp], refs[pp:2 * pp]
    o_ref, m_s, l_s, acc_s = refs[2 * pp:]
    step = pl.program_id(1)
    rows, width = q_ref.shape[1], q_ref.shape[2]
    groups = pen_ref.shape[1]
    rpg = rows // groups
    nq = pen_ref.shape[2] // 2
    n_sel = pen_ref.shape[3]
    own = _own_head_block(rows, width, rpg)
    qb = jnp.where(own, q_ref[0] * HEAD_DIM ** -0.5, 0.0).astype(BF16)
    slope = slope_ref[...]
    blocks_per_page = PAGE_SIZE // SEL_BLOCK
    pen_rows = jnp.concatenate([pen_ref[0, g] for g in range(groups) for _ in range(rpg // (2 * nq))],
                               axis=0).astype(BF16)

    @pl.when(step == 0)
    def _():
        m_s[...] = jnp.full(m_s.shape, -jnp.inf, F32)
        l_s[...] = jnp.zeros(l_s.shape, F32)
        acc_s[...] = jnp.zeros(acc_s.shape, F32)

    def attend(carry, kt, vt, bias, mask=None):
        m, l, acc = carry
        s = jnp.dot(qb, kt, preferred_element_type=F32) + bias
        if mask is not None:
            s = jnp.where(mask, s, NEG_BIG)
        m_new = jnp.maximum(m, jnp.max(s, axis=-1, keepdims=True))
        p = jnp.exp(s - m_new)
        alpha = jnp.exp(m - m_new)
        return m_new, alpha * l + jnp.sum(p, axis=-1, keepdims=True), alpha * acc + _dot_nt(p.astype(BF16), vt)

    col = lax.broadcasted_iota(jnp.int32, (rows, PAGE_SIZE), 1)
    col_f = col.astype(F32)
    span = pp * PAGE_SIZE
    e_row = lax.broadcasted_iota(jnp.int32, (n_sel, span), 0)
    e_col = lax.broadcasted_iota(jnp.int32, (n_sel, span), 1)
    expand = jnp.where(e_row == step * (pp * blocks_per_page) + e_col // SEL_BLOCK, 1.0, 0.0).astype(BF16)
    pos = lax.broadcasted_iota(jnp.int32, (rows, span), 1).astype(F32) + (step * span - past).astype(F32)
    bias = jnp.dot(pen_rows, expand, preferred_element_type=F32) + slope * pos
    kt = jnp.concatenate([r[0, 0].reshape(width, PAGE_SIZE) for r in k_refs], axis=1).astype(BF16)
    vt = jnp.concatenate([r[0, 0].reshape(width, PAGE_SIZE) for r in v_refs], axis=1).astype(BF16)
    m_s[...], l_s[...], acc_s[...] = attend((m_s[...], l_s[...], acc_s[...]), kt, vt, bias)

    @pl.when(step == n_steps - 1)
    def _():
        gates = jax.nn.sigmoid(gt_ref[0])
        qi = lax.broadcasted_iota(jnp.int32, (rows, PAGE_SIZE), 0) % nq
        ok_n = (col <= qi) & (col < nq)
        pen_new = pen_rows[:, past // SEL_BLOCK:past // SEL_BLOCK + 1].astype(F32)
        _, l_sel, acc_sel = attend((m_s[...], l_s[...], acc_s[...]), ksn_ref[0].astype(BF16),
                                   vsn_ref[0].astype(BF16), slope * col_f + pen_new, ok_n)
        o_sel = _fold_heads(acc_sel, own) / l_sel
        wqi = lax.broadcasted_iota(jnp.int32, (rows, wbuf), 0) % nq
        wc = lax.broadcasted_iota(jnp.int32, (rows, wbuf), 1)
        dist_w = wqi + wbuf - wc
        ok_w = (dist_w >= 0) & (dist_w < WINDOW) & (past - wbuf + wc >= 0)
        init = (jnp.full((rows, 1), -jnp.inf, F32), jnp.zeros((rows, 1), F32), jnp.zeros((rows, width), F32))
        carry_w = attend(init, wk_ref[0, 0].reshape(width, wbuf).astype(BF16),
                         wv_ref[0, 0].reshape(width, wbuf).astype(BF16), slope * (wc - wbuf).astype(F32), ok_w)
        _, l_win, acc_win = attend(carry_w, kwn_ref[0].astype(BF16), vwn_ref[0].astype(BF16), slope * col_f, ok_n)
        o_win = _fold_heads(acc_win, own) / l_win
        o_ref[0] = (gates[:, 0:1] * oc_ref[0, :, 0:HEAD_DIM] + gates[:, 1:2] * o_sel + gates[:, 2:3] * o_win)


def nsa_decode_sel(q_tiled, slope_rows, pen, o_c, gt_rows, ks_new, vs_new, kw_new, vw_new, win_k, win_v,
                   cache_k, cache_v, page_table, layer):
    b, rows, _ = q_tiled.shape
    groups = cache_k.shape[2]
    n_pages = page_table.shape[1]
    past = n_pages * PAGE_SIZE
    pp = _SEL_PAGES_PER_STEP
    wbuf = win_k.shape[4]
    assert n_pages % pp == 0 and past % SEL_BLOCK == 0
    n_steps = n_pages // pp
    per_seq = lambda a: pl.BlockSpec((1,) + a.shape[1:], lambda s, j, pt: (s,) + (0,) * (a.ndim - 1))
    page = lambda u: pl.BlockSpec((1, 1, groups, HEAD_DIM, PAGE_SIZE),
                                  lambda s, j, pt: (layer, pt[s, pp * j + u], 0, 0, 0))
    win = pl.BlockSpec((1, 1, groups, HEAD_DIM, wbuf), lambda s, j, pt: (layer, s, 0, 0, 0))
    grid_spec = pltpu.PrefetchScalarGridSpec(
        num_scalar_prefetch=1,
        grid=(b, n_steps),
        in_specs=[per_seq(q_tiled), pl.BlockSpec(slope_rows.shape, lambda s, j, pt: (0, 0)),
                  per_seq(pen), per_seq(o_c), per_seq(gt_rows),
                  per_seq(ks_new), per_seq(vs_new), per_seq(kw_new), per_seq(vw_new), win, win]
        + [page(u) for u in range(pp)] * 2,
        out_specs=pl.BlockSpec((1, rows, HEAD_DIM), lambda s, j, pt: (s, 0, 0)),
        scratch_shapes=[pltpu.VMEM((rows, 1), F32), pltpu.VMEM((rows, 1), F32),
                        pltpu.VMEM((rows, groups * HEAD_DIM), F32)],
    )
    return pl.pallas_call(
        functools.partial(_nsa_decode_sel_body, n_steps, past, wbuf),
        grid_spec=grid_spec,
        out_shape=jax.ShapeDtypeStruct((b, rows, HEAD_DIM), F32),
        compiler_params=_cparams(("parallel", "arbitrary")),
        name="nsa_decode_sel",
    )(page_table, q_tiled, slope_rows, pen, o_c, gt_rows, ks_new, vs_new, kw_new, vw_new, win_k, win_v,
      *([cache_k] * pp), *([cache_v] * pp))


def _heads_to_rows(x, n, t, heads):
    return x.reshape(n, t, heads, HEAD_DIM).transpose(0, 2, 1, 3).reshape(n, heads * t, HEAD_DIM)


def _rows_to_heads(x, n, t, heads):
    return x.reshape(n, heads, t, HEAD_DIM).transpose(0, 2, 1, 3).reshape(n * t, heads * HEAD_DIM)


def _new_kv_cols(x, n, t):
    return jnp.pad(x.reshape(n, t, x.shape[-1]).transpose(0, 2, 1), ((0, 0), (0, 0), (0, PAGE_SIZE - t)))


def _heads_last(a, heads):
    n, _, t = a.shape
    return a.reshape(n, heads, HEAD_DIM, t).transpose(0, 3, 1, 2)


def _tokens_minor(a):
    return a.transpose(0, 1, 3, 4, 2)


def _layer_ab(layer, l, yp, ys, dims, p):
    bp, tp, bs, ts = dims
    d_rnn = p["conv_w"].shape[2]
    heads = (p["w_in_ab"].shape[2] - 2 * d_rnn) // (3 * HEAD_DIM)
    hw = heads * HEAD_DIM
    w_in = p["w_in_ab"][l].astype(BF16)
    cuts = [0, d_rnn, 2 * d_rnn, 2 * d_rnn + hw, 2 * d_rnn + 2 * hw, 2 * d_rnn + 3 * hw]
    ws = [w_in[:, a:b] for a, b in zip(cuts[:-1], cuts[1:])]
    w_out = p["w_out_ab"][l].astype(BF16)
    w_out = [w_out[:d_rnn], w_out[d_rnn:]]
    rg = (p["conv_w"][l], p["conv_b"][l], _block_diag(p["w_rg_a"][l]).astype(BF16), p["b_rg_a"][l],
          _block_diag(p["w_rg_i"][l]).astype(BF16), p["b_rg_i"][l], p["rg_lambda"][l])
    slopes = _alibi_slopes(heads)
    g = p["norm_mix"][layer]
    f32x5 = [(i, F32) for i in range(5)]

    xr, gr, q, k, k_fm, v_fm, kb, vb = norm_proj(
        yp, g, ws, f32x5[:4] + [(3, F32, True), (4, F32, True), (3, BF16), (4, BF16)], seq_len=tp)
    seq = lambda a: a.reshape(bp, tp, a.shape[-1])
    r_p, conv_p, h_p = rglru(seq(xr), seq(gr), jnp.zeros((bp, CONV_W - 1, d_rnn), F32),
                             jnp.zeros((bp, d_rnn), F32), *rg)
    att = moba_prompt(seq(q), seq(kb), seq(vb), block_mean(seq(k), MOBA_BLOCK), slopes)
    yp = proj_residual([r_p.reshape(bp * tp, d_rnn), att.reshape(bp * tp, hw)], w_out, yp)
    out_p = (_heads_last(k_fm, heads), _heads_last(v_fm, heads), conv_p, h_p)

    xr, gr, q, k, v = norm_proj(ys, g, ws, f32x5)
    seq = lambda a: a.reshape(bs, ts, a.shape[-1])
    r_s, conv_s, h_s = rglru(seq(xr), seq(gr), p["state_conv"][l], p["state_rglru"][l], *rg)
    att = moba_decode(jnp.tile(_heads_to_rows(q, bs, ts, heads), (1, 1, heads)), _new_kv_cols(k, bs, ts),
                      _new_kv_cols(v, bs, ts), _tokens_minor(p["cache_moba_k"]), _tokens_minor(p["cache_moba_v"]),
                      p["page_table"], l, slopes)
    ys = proj_residual([r_s.reshape(bs * ts, d_rnn), _rows_to_heads(att, bs, ts, heads)], w_out, ys)
    out_s = (k.reshape(bs, ts, heads, HEAD_DIM), v.reshape(bs, ts, heads, HEAD_DIM), conv_s, h_s)
    return yp, ys, out_p, out_s


def _layer_c(layer, l, yp, ys, dims, p):
    bp, tp, bs, ts = dims
    groups = p["cache_cmp_k"].shape[3]
    gw = groups * HEAD_DIM
    w_in = p["w_in_c"][l]
    heads = (w_in.shape[1] - 6 * gw) // (HEAD_DIM + 3)
    qw = heads * HEAD_DIM
    cuts = [0, qw] + [qw + (i + 1) * gw for i in range(6)]
    w_f32 = [w_in[:, a:b] for a, b in zip(cuts[:-1], cuts[1:])]
    w_gt = w_in[:, cuts[-1]:]
    slopes = _alibi_slopes(heads)
    g = p["norm_mix"][layer]
    w_out = [p["w_out_c"][l].astype(BF16)]
    cmp_k = (p["w_cmp_k1"][l], p["w_cmp_k2"][l], p["pe_cmp_k"][l])
    cmp_v = (p["w_cmp_v1"][l], p["w_cmp_v2"][l], p["pe_cmp_v"][l])
    kv4 = lambda a, n, t: a.reshape(n, t, groups, HEAD_DIM)

    ws = ([w.astype(BF16) for w in w_f32] + [_pad_group_cols(w_gt, 3 * heads // groups).astype(BF16)]
          + [_pad_group_cols(w, HEAD_DIM).astype(BF16) for w in w_f32[3:7]])
    outs = ([(0, F32), (1, F32), (2, F32), (7, F32)] + [(i, F32, True) for i in range(1, 7)]
            + [(i, BF16) for i in range(8, 12)])
    q, kc, vc, gt, *kv_fm, ksp, vsp, kwp, vwp = norm_proj(yp, g, ws, outs, seq_len=tp)
    seq = lambda a: a.reshape(bp, tp, a.shape[-1])
    ck = nsa_compress(seq(kc), *_compress_weights(*cmp_k, groups), True)
    cv = nsa_compress(seq(vc), *_compress_weights(*cmp_v, groups), False)
    cover = _nsa_cover(tp // CMP_STRIDE, max(-(-tp // SEL_BLOCK), SEL_TOPK))
    att = nsa_prompt(seq(q), seq(gt), ck, cv, cover, seq(ksp), seq(vsp), seq(kwp), seq(vwp), slopes)
    yp = proj_residual([att.reshape(bp * tp, qw)], w_out, yp)
    w_keep = min(WINDOW, tp)
    out_p = (tuple(_heads_last(a, groups) for a in kv_fm[:4])
             + tuple(_heads_last(a[:, :, tp - w_keep:], groups) for a in kv_fm[4:]))

    ws = [w.astype(BF16) for w in w_f32] + [jnp.pad(w_gt, ((0, 0), (0, LANES - w_gt.shape[1]))).astype(BF16)]
    q, kc, vc, ks, vs, kw, vw, gt = norm_proj(ys, g, ws, [(i, F32) for i in range(8)])
    q_rows = _heads_to_rows(q, bs, ts, heads)
    q_tiled = jnp.tile(q_rows, (1, 1, groups))
    q_rows = jnp.pad(q_rows, ((0, 0), (0, 0), (0, LANES - HEAD_DIM)))
    slope_rows = jnp.asarray(np.repeat(slopes, ts).reshape(heads * ts, 1))
    gt_rows = gt[:, :3 * heads].reshape(bs, ts, heads, 3).transpose(0, 2, 1, 3).reshape(bs, heads * ts, 3)
    n_pages = p["page_table"].shape[1]
    past = n_pages * PAGE_SIZE
    cover = _nsa_cover(n_pages * _CHUNKS_PER_PAGE, max(-(-(past + ts) // SEL_BLOCK), SEL_TOPK))
    dec_w = []
    for w1, w2, pe in (cmp_k, cmp_v):
        w1j = w1.reshape(CMP_LEN, HEAD_DIM, HEAD_DIM)
        dec_w.append((jnp.concatenate([w1j[:CMP_STRIDE], w1j[CMP_STRIDE:]], axis=2).astype(BF16), pe,
                      jnp.pad(w2, ((0, LANES - HEAD_DIM), (0, LANES - HEAD_DIM))).astype(BF16)))
    (wjk, pek, w2k), (wjv, pev, w2v) = dec_w
    ab_k = nsa_page_ab(_tokens_minor(p["cache_cmp_k"]), l, wjk)
    ab_v = nsa_page_ab(_tokens_minor(p["cache_cmp_v"]), l, wjv)
    o_c, pen = nsa_decode_cmp(q_rows, slope_rows, ab_k, ab_v, p["page_table"], wjk, wjv, pek, pev, w2k, w2v,
                              cover, ts)
    att = nsa_decode_sel(q_tiled, slope_rows, pen, o_c, gt_rows, _new_kv_cols(ks, bs, ts),
                         _new_kv_cols(vs, bs, ts), _new_kv_cols(kw, bs, ts), _new_kv_cols(vw, bs, ts),
                         _tokens_minor(p["state_win_k"]), _tokens_minor(p["state_win_v"]),
                         _tokens_minor(p["cache_sel_k"]), _tokens_minor(p["cache_sel_v"]), p["page_table"], l)
    ys = proj_residual([_rows_to_heads(att, bs, ts, heads)], w_out, ys)
    w_buf = p["state_win_k"].shape[2]
    win = lambda state, new: jnp.concatenate([state[l], kv4(new, bs, ts)], axis=1)[:, -w_buf:]
    out_s = tuple(kv4(a, bs, ts) for a in (kc, vc, ks, vs)) + (win(p["state_win_k"], kw), win(p["state_win_v"], vw))
    return yp, ys, out_p, out_s


def kernel(x_prompt, x_sample, cache_moba_k, cache_moba_v, cache_cmp_k, cache_cmp_v, cache_sel_k, cache_sel_v,
           state_win_k, state_win_v, state_conv, state_rglru, page_table, norm_mix, norm_ffn, norm_final,
           w_in_ab, w_out_ab, conv_w, conv_b, w_rg_a, b_rg_a, w_rg_i, b_rg_i, rg_lambda, w_in_c, w_out_c,
           w_cmp_k1, w_cmp_k2, pe_cmp_k, w_cmp_v1, w_cmp_v2, pe_cmp_v, w_route_group, b_route_group,
           w_route_expert, b_route_expert, w_exp_gate, w_exp_up, w_exp_down):
    p = dict(cache_moba_k=cache_moba_k, cache_moba_v=cache_moba_v, cache_cmp_k=cache_cmp_k,
             cache_cmp_v=cache_cmp_v, cache_sel_k=cache_sel_k, cache_sel_v=cache_sel_v,
             state_win_k=state_win_k, state_win_v=state_win_v, state_conv=state_conv, state_rglru=state_rglru,
             page_table=page_table, norm_mix=norm_mix, w_in_ab=w_in_ab, w_out_ab=w_out_ab, conv_w=conv_w,
             conv_b=conv_b, w_rg_a=w_rg_a, b_rg_a=b_rg_a, w_rg_i=w_rg_i, b_rg_i=b_rg_i, rg_lambda=rg_lambda,
             w_in_c=w_in_c, w_out_c=w_out_c, w_cmp_k1=w_cmp_k1, w_cmp_k2=w_cmp_k2, pe_cmp_k=pe_cmp_k,
             w_cmp_v1=w_cmp_v1, w_cmp_v2=w_cmp_v2, pe_cmp_v=pe_cmp_v)
    bp, tp, d = x_prompt.shape
    bs, ts, _ = x_sample.shape
    dims = (bp, tp, bs, ts)
    yp = x_prompt.reshape(bp * tp, d)
    ys = x_sample.reshape(bs * ts, d)
    ab_p, ab_s, c_p, c_s = [], [], [], []
    for layer in range(norm_mix.shape[0]):
        l = layer // 2
        if layer % 2 == 0:
            yp, ys, o_p, o_s = _layer_ab(layer, l, yp, ys, dims, p)
            ab_p.append(o_p)
            ab_s.append(o_s)
        else:
            yp, ys, o_p, o_s = _layer_c(layer, l, yp, ys, dims, p)
            c_p.append(o_p)
            c_s.append(o_s)
        wr, br = _router_weights(w_route_group[layer], b_route_group[layer], w_route_expert[layer],
                                 b_route_expert[layer])
        moe_w = (w_exp_gate[layer].astype(BF16), w_exp_up[layer].astype(BF16), w_exp_down[layer].astype(BF16))
        yp = hier_moe_residual(yp, norm_ffn[layer], wr, br, *moe_w)
        ys = hier_moe_residual(ys, norm_ffn[layer], wr, br, *moe_w)
    y_prompt = rmsnorm(yp, norm_final).reshape(bp, tp, d)
    y_sample = rmsnorm(ys, norm_final).reshape(bs, ts, d)
    stack = lambda outs, i: jnp.stack([o[i] for o in outs])
    res = [y_prompt, y_sample]
    for i in range(4):
        res += [stack(ab_p, i), stack(ab_s, i)]
    for i in range(6):
        res += [stack(c_p, i), stack(c_s, i)]
    return tuple(res)
```

```python
import functools

import numpy as np
import jax
import jax.numpy as jnp
from jax import lax
from jax.experimental import pallas as pl
from jax.experimental.pallas import tpu as pltpu

F32 = jnp.float32
BF16 = jnp.bfloat16
HIGHEST = lax.Precision.HIGHEST

HEAD_DIM = 64
LANES = 128
SUBLANES = 8
CONV_W = 4
RG_C = 8.0
MOBA_BLOCK = 256
MOBA_TOPK = 3
CMP_STRIDE = 16
CMP_LEN = 32
SEL_BLOCK = 64
SEL_TOPK = 16
WINDOW = 512
N_GROUPS = 4
EXPERTS_PER_GROUP = 8
N_EXPERTS = N_GROUPS * EXPERTS_PER_GROUP
PAGE_SIZE = 128
RMS_EPS = 1e-6
NEG_BIG = -1e30
VMEM_LIMIT = 56 * 1024 * 1024
MOE_BLOCK_ROWS = 256


def _cparams(sem):
    return pltpu.CompilerParams(dimension_semantics=sem, vmem_limit_bytes=VMEM_LIMIT)


def _alibi_slopes(n):
    return np.asarray(2.0 ** (-8.0 * np.arange(1, n + 1) / n), dtype=np.float32)


def _row_tile(m, pref=512):
    return pref if m % pref == 0 else m


def _dot_nt(a, b, precision=None):
    return lax.dot_general(a, b, (((1,), (1,)), ((), ())), precision=precision,
                           preferred_element_type=F32)


def _norm_proj_body(out_map, x_ref, g_ref, *refs):
    n_w = max(wi for wi, _ in out_map) + 1
    w_refs, o_refs = refs[:n_w], refs[n_w:]
    x = x_ref[...]
    y = x * lax.rsqrt(jnp.mean(x * x, axis=-1, keepdims=True) + RMS_EPS) * g_ref[...]
    yb = y.astype(BF16)
    prods = [jnp.dot(yb, w_ref[...], preferred_element_type=F32) for w_ref in w_refs]
    for (wi, feature_major), o_ref in zip(out_map, o_refs):
        if feature_major:
            o_ref[0] = prods[wi].T.astype(o_ref.dtype)
        else:
            o_ref[...] = prods[wi].astype(o_ref.dtype)


def norm_proj(x, g, ws, outs, seq_len=None):
    m, d = x.shape
    tm = _row_tile(m)
    outs = [o if len(o) == 3 else (o[0], o[1], False) for o in outs]
    out_map = tuple((wi, fm) for wi, _, fm in outs)
    tiles = (seq_len or tm) // tm
    out_specs, out_shape = [], []
    for wi, dt, fm in outs:
        cols = ws[wi].shape[1]
        if fm:
            assert seq_len % tm == 0
            out_specs.append(pl.BlockSpec((1, cols, tm), lambda i: (i // tiles, 0, i % tiles)))
            out_shape.append(jax.ShapeDtypeStruct((m // seq_len, cols, seq_len), dt))
        else:
            out_specs.append(pl.BlockSpec((tm, cols), lambda i: (i, 0)))
            out_shape.append(jax.ShapeDtypeStruct((m, cols), dt))
    return pl.pallas_call(
        functools.partial(_norm_proj_body, out_map),
        grid=(m // tm,),
        in_specs=[pl.BlockSpec((tm, d), lambda i: (i, 0)), pl.BlockSpec((1, d), lambda i: (0, 0))]
        + [pl.BlockSpec(w.shape, lambda i: (0, 0)) for w in ws],
        out_specs=out_specs,
        out_shape=out_shape,
        compiler_params=_cparams(("parallel",)),
        name="norm_proj",
    )(x, g.reshape(1, d), *ws)


def _proj_res_body(n_in, *refs):
    x_refs, w_refs = refs[:n_in], refs[n_in:2 * n_in]
    res_ref, o_ref = refs[2 * n_in], refs[2 * n_in + 1]
    acc = jnp.dot(x_refs[0][...].astype(BF16), w_refs[0][...], preferred_element_type=F32)
    for x_ref, w_ref in zip(x_refs[1:], w_refs[1:]):
        acc = acc + jnp.dot(x_ref[...].astype(BF16), w_ref[...], preferred_element_type=F32)
    o_ref[...] = res_ref[...] + acc


def proj_residual(xs, ws, res):
    m, d = res.shape
    tm = _row_tile(m)
    return pl.pallas_call(
        functools.partial(_proj_res_body, len(xs)),
        grid=(m // tm,),
        in_specs=[pl.BlockSpec((tm, x.shape[1]), lambda i: (i, 0)) for x in xs]
        + [pl.BlockSpec(w.shape, lambda i: (0, 0)) for w in ws]
        + [pl.BlockSpec((tm, d), lambda i: (i, 0))],
        out_specs=pl.BlockSpec((tm, d), lambda i: (i, 0)),
        out_shape=jax.ShapeDtypeStruct((m, d), F32),
        compiler_params=_cparams(("parallel",)),
        name="proj_residual",
    )(*xs, *ws, res)


def _rmsnorm_body(x_ref, g_ref, o_ref):
    x = x_ref[...]
    o_ref[...] = x * lax.rsqrt(jnp.mean(x * x, axis=-1, keepdims=True) + RMS_EPS) * g_ref[...]


def rmsnorm(x, g):
    m, d = x.shape
    tm = _row_tile(m)
    return pl.pallas_call(
        _rmsnorm_body,
        grid=(m // tm,),
        in_specs=[pl.BlockSpec((tm, d), lambda i: (i, 0)), pl.BlockSpec((1, d), lambda i: (0, 0))],
        out_specs=pl.BlockSpec((tm, d), lambda i: (i, 0)),
        out_shape=jax.ShapeDtypeStruct((m, d), F32),
        compiler_params=_cparams(("parallel",)),
        name="rmsnorm",
    )(x, g.reshape(1, d))


_XIN_PAD = 8


def _rglru_body(xr_ref, gr_ref, cb_ref, h0_ref, cw_ref, cbias_ref, wa_ref, ba_ref, wi_ref, bi_ref,
                lam_ref, y_ref, tail_ref, hl_ref, xin_s, a_s, b_s, h_s):
    tt = xr_ref.shape[1]
    keep = CONV_W - 1

    @pl.when(pl.program_id(1) == 0)
    def _():
        xin_s[_XIN_PAD - keep:_XIN_PAD, :] = cb_ref[0]
        h_s[...] = h0_ref[0]

    xin_s[_XIN_PAD:_XIN_PAD + tt, :] = xr_ref[0]
    xc = cbias_ref[...]
    for k in range(CONV_W):
        lo = _XIN_PAD - keep + k
        xc = xc + xin_s[lo:lo + tt, :] * cw_ref[k:k + 1, :]
    new_tail = xin_s[_XIN_PAD + tt - keep:_XIN_PAD + tt, :]
    xin_s[_XIN_PAD - keep:_XIN_PAD, :] = new_tail
    tail_ref[0] = new_tail

    xcb = xc.astype(BF16)
    r = jax.nn.sigmoid(jnp.dot(xcb, wa_ref[...], preferred_element_type=F32) + ba_ref[...])
    gi = jax.nn.sigmoid(jnp.dot(xcb, wi_ref[...], preferred_element_type=F32) + bi_ref[...])
    nl = -lam_ref[...]
    softplus = jnp.maximum(nl, 0.0) + jnp.log1p(jnp.exp(-jnp.abs(nl)))
    log_a = (-RG_C) * r * softplus
    a = jnp.exp(log_a)
    a_s[...] = a
    b_s[...] = jnp.sqrt(1.0 - a * a) * (gi * xc)

    def step(t, h):
        h = a_s[pl.ds(t, 1), :] * h + b_s[pl.ds(t, 1), :]
        b_s[pl.ds(t, 1), :] = h
        return h

    h = lax.fori_loop(0, tt, step, h_s[...], unroll=min(tt, 8))
    h_s[...] = h
    hl_ref[0] = h
    y_ref[0] = b_s[...] * jax.nn.gelu(gr_ref[0])


def rglru(xr, gr, conv_buf, h0, conv_w, conv_b, wa_bd, b_a, wi_bd, b_i, lam):
    n, t, w = xr.shape
    tt = min(t, 512)
    assert t % tt == 0 and tt >= CONV_W - 1
    row = lambda v: v.reshape(1, w)
    full = lambda shp: pl.BlockSpec(shp, lambda b, s: (0,) * len(shp))
    y, tail, hl = pl.pallas_call(
        _rglru_body,
        grid=(n, t // tt),
        in_specs=[pl.BlockSpec((1, tt, w), lambda b, s: (b, s, 0)),
                  pl.BlockSpec((1, tt, w), lambda b, s: (b, s, 0)),
                  pl.BlockSpec((1, CONV_W - 1, w), lambda b, s: (b, 0, 0)),
                  pl.BlockSpec((1, 1, w), lambda b, s: (b, 0, 0)),
                  full((CONV_W, w)), full((1, w)), full((w, w)), full((1, w)), full((w, w)),
                  full((1, w)), full((1, w))],
        out_specs=[pl.BlockSpec((1, tt, w), lambda b, s: (b, s, 0)),
                   pl.BlockSpec((1, CONV_W - 1, w), lambda b, s: (b, 0, 0)),
                   pl.BlockSpec((1, 1, w), lambda b, s: (b, 0, 0))],
        out_shape=[jax.ShapeDtypeStruct((n, t, w), F32),
                   jax.ShapeDtypeStruct((n, CONV_W - 1, w), F32),
                   jax.ShapeDtypeStruct((n, 1, w), F32)],
        scratch_shapes=[pltpu.VMEM((_XIN_PAD + tt, w), F32), pltpu.VMEM((tt, w), F32),
                        pltpu.VMEM((tt, w), F32), pltpu.VMEM((1, w), F32)],
        compiler_params=_cparams(("parallel", "arbitrary")),
        name="rglru",
    )(xr, gr, conv_buf, h0.reshape(n, 1, w), conv_w, row(conv_b), wa_bd, row(b_a), wi_bd, row(b_i), row(lam))
    return y, tail, hl.reshape(n, w)


def _block_diag(w):
    h, d, _ = w.shape
    eye = jnp.eye(h, dtype=w.dtype)
    return (eye[:, None, :, None] * w[:, :, None, :]).reshape(h * d, h * d)


def _block_mean_body(blk, k_ref, o_ref):
    x = k_ref[0]
    nb = x.shape[0] // blk
    o_ref[0] = jnp.mean(x.reshape(nb, blk, x.shape[1]), axis=1)


def block_mean(k, blk):
    n, t, w = k.shape
    nb = t // blk
    g = SUBLANES if nb % SUBLANES == 0 else nb
    return pl.pallas_call(
        functools.partial(_block_mean_body, blk),
        grid=(n, nb // g),
        in_specs=[pl.BlockSpec((1, g * blk, w), lambda b, s: (b, s, 0))],
        out_specs=pl.BlockSpec((1, g, w), lambda b, s: (b, s, 0)),
        out_shape=jax.ShapeDtypeStruct((n, nb, w), F32),
        compiler_params=_cparams(("parallel", "parallel")),
        name="block_mean",
    )(k)


def _first_argmax(g, idx_f):
    m = jnp.max(g, axis=-1, keepdims=True)
    return jnp.min(jnp.where(g == m, idx_f, jnp.inf), axis=-1, keepdims=True)


def _split_hi_lo(x):
    hi = x.astype(BF16)
    lo = (x - hi.astype(F32)).astype(BF16)
    return hi, lo


LOG2E = 1.4426950408889634
_QSCALE = HEAD_DIM ** -0.5 * LOG2E
_MOBA_PAIRS_PER_STEP = 2


def _moba_prompt_body(nbk, slopes_ref, q_ref, km_ref, kb_ref, vt_ref, o_ref):
    pg = pl.program_id(1)
    i = pl.program_id(2)
    tq = q_ref.shape[1]
    pairs = q_ref.shape[2] // LANES
    qlane = lax.broadcasted_iota(jnp.int32, (tq, LANES), 1)
    blk = lax.broadcasted_iota(jnp.int32, (LANES, tq), 0)
    blk_f = blk.astype(F32)
    qts = []
    for pp in range(pairs):
        q = q_ref[0, :, pp * LANES:(pp + 1) * LANES]
        km = km_ref[0, :, pp * LANES:(pp + 1) * LANES]
        q_cols = []
        for hh in range(2):
            in_head = (qlane >= hh * HEAD_DIM) & (qlane < (hh + 1) * HEAD_DIM)
            qh = jnp.where(in_head, q, 0.0)
            gate = _dot_nt(km, qh, HIGHEST)
            g = jnp.where(blk < i, gate, -jnp.inf)
            sel = blk == i
            for _ in range(MOBA_TOPK):
                m = jnp.max(g, axis=0, keepdims=True)
                first = jnp.min(jnp.where(g == m, blk_f, jnp.inf), axis=0, keepdims=True)
                pick = blk_f == first
                sel = sel | (pick & (blk < i))
                g = jnp.where(pick, -jnp.inf, g)
            slope = slopes_ref[2 * (pg * pairs + pp) + hh]
            pb = jnp.where(sel, 0.0, NEG_BIG) + slope * (blk_f * MOBA_BLOCK)
            pb_hi, pb_lo = _split_hi_lo(pb)
            pb_lo = pltpu.roll(pb_lo.astype(F32), nbk, 0)
            s_hi, s_lo = _split_hi_lo(jnp.full((LANES, tq), slope, F32))
            aug = jnp.where(blk < nbk, pb_hi.astype(F32),
                            jnp.where(blk < 2 * nbk, pb_lo,
                                      jnp.where(blk == 2 * nbk, s_hi.astype(F32),
                                                jnp.where(blk == 2 * nbk + 1, s_lo.astype(F32), 0.0))))
            q_cols.append(jnp.concatenate([(qh * _QSCALE).T.astype(BF16), aug.astype(BF16)], axis=0))
        qts.append(jnp.concatenate(q_cols, axis=1))

    klane = lax.broadcasted_iota(jnp.int32, (MOBA_BLOCK, LANES), 1)
    krow = lax.broadcasted_iota(jnp.int32, (MOBA_BLOCK, LANES), 0).astype(F32)
    vrow = lax.broadcasted_iota(jnp.int32, (LANES, MOBA_BLOCK), 0)
    ones_row = (HEAD_DIM, 0)
    causal = (lax.broadcasted_iota(jnp.int32, (MOBA_BLOCK, tq), 0)
              <= lax.broadcasted_iota(jnp.int32, (MOBA_BLOCK, tq), 1))

    def scores(j):
        off = pl.multiple_of(j * MOBA_BLOCK, MOBA_BLOCK)
        kaug = jnp.where((klane == j) | (klane == nbk + j), 1.0,
                         jnp.where((klane == 2 * nbk) | (klane == 2 * nbk + 1), krow, 0.0)).astype(BF16)
        return [jnp.dot(jnp.concatenate([kb_ref[0, pl.ds(off, MOBA_BLOCK), pp * LANES:(pp + 1) * LANES], kaug],
                                        axis=1), qts[pp], preferred_element_type=F32)
                for pp in range(pairs)]

    def update(j, s, carry, diag):
        off = pl.multiple_of(j * MOBA_BLOCK, MOBA_BLOCK)
        new = []
        for pp in range(pairs):
            vt = vt_ref[0, pp * LANES:(pp + 1) * LANES, pl.ds(off, MOBA_BLOCK)]
            for hh in range(2):
                m, acc = carry[2 * pp + hh]
                sh = s[pp][:, hh * tq:(hh + 1) * tq]
                if diag:
                    sh = jnp.where(causal, sh, NEG_BIG)
                m_new = jnp.maximum(m, jnp.max(sh, axis=0, keepdims=True))
                p = jnp.exp2(sh - m_new).astype(BF16)
                in_head = (vrow >= hh * HEAD_DIM) & (vrow < (hh + 1) * HEAD_DIM)
                v_h = jnp.where(in_head, vt, jnp.where(vrow == ones_row[hh], 1.0, 0.0).astype(BF16))
                new.append((m_new, jnp.exp2(m - m_new) * acc + jnp.dot(v_h, p, preferred_element_type=F32)))
        return new

    def body(j, c):
        s, carry = c
        return scores(j + 1), update(j, s, carry, False)

    init = [(jnp.full((1, tq), -jnp.inf, F32), jnp.zeros((LANES, tq), F32)) for _ in range(2 * pairs)]
    s, carry = lax.fori_loop(0, i, body, (scores(0), init))
    carry = update(i, s, carry, True)
    orow = lax.broadcasted_iota(jnp.int32, (LANES, tq), 0)
    for pp in range(pairs):
        outs = [acc / acc[ones_row[hh]:ones_row[hh] + 1, :] for hh, (_, acc) in enumerate(carry[2 * pp:2 * pp + 2])]
        o_ref[0, :, pp * LANES:(pp + 1) * LANES] = jnp.where(orow < HEAD_DIM, outs[0], outs[1]).T


def moba_prompt(q, kb, vt, kmean, slopes):
    n, t, w = q.shape
    nbk = t // MOBA_BLOCK
    pw = _MOBA_PAIRS_PER_STEP * LANES
    assert t % MOBA_BLOCK == 0 and nbk >= MOBA_TOPK and 2 * nbk + 2 <= LANES and nbk % SUBLANES == 0 and w % pw == 0
    km = jnp.pad(kmean, ((0, 0), (0, LANES - nbk), (0, 0)))
    grid_spec = pltpu.PrefetchScalarGridSpec(
        num_scalar_prefetch=0,
        grid=(n, w // pw, nbk),
        in_specs=[pl.BlockSpec(memory_space=pltpu.SMEM),
                  pl.BlockSpec((1, MOBA_BLOCK, pw), lambda b, p, i: (b, i, p)),
                  pl.BlockSpec((1, LANES, pw), lambda b, p, i: (b, 0, p)),
                  pl.BlockSpec((1, t, pw), lambda b, p, i: (b, 0, p)),
                  pl.BlockSpec((1, pw, t), lambda b, p, i: (b, p, 0))],
        out_specs=pl.BlockSpec((1, MOBA_BLOCK, pw), lambda b, p, i: (b, i, p)),
    )
    return pl.pallas_call(
        functools.partial(_moba_prompt_body, nbk),
        grid_spec=grid_spec,
        out_shape=jax.ShapeDtypeStruct((n, t, w), F32),
        compiler_params=_cparams(("parallel", "parallel", "arbitrary")),
        name="moba_prompt",
    )(jnp.asarray(slopes * np.float32(LOG2E)), q, km, kb, vt)


_ROUTE_E, _ROUTE_W, _ROUTE_RANK = 0, 2, 4


def _router_body(x_ref, g_ref, wr_ref, br_ref, xn_ref, route_ref, cnt_ref, carry_s):
    tm = x_ref.shape[0]

    @pl.when(pl.program_id(0) == 0)
    def _():
        carry_s[...] = jnp.zeros_like(carry_s)

    x = x_ref[...]
    xn = x * lax.rsqrt(jnp.mean(x * x, axis=-1, keepdims=True) + RMS_EPS) * g_ref[...]
    xn_ref[...] = xn
    logits = jnp.dot(xn, wr_ref[...], precision=HIGHEST, preferred_element_type=F32) + br_ref[...]
    lane = lax.broadcasted_iota(jnp.int32, (tm, LANES), 1)
    is_g = lane < N_GROUPS
    gl = jnp.where(is_g, logits, -jnp.inf)
    gmax = jnp.max(gl, axis=-1, keepdims=True)
    g_idx = jnp.min(jnp.where(gl == gmax, lane, LANES), axis=-1, keepdims=True)
    g_w = 1.0 / jnp.sum(jnp.where(is_g, jnp.exp(gl - gmax), 0.0), axis=-1, keepdims=True)
    e_lo = N_GROUPS + g_idx * EXPERTS_PER_GROUP
    el = jnp.where((lane >= e_lo) & (lane < e_lo + EXPERTS_PER_GROUP), logits, -jnp.inf)
    v1 = jnp.max(el, axis=-1, keepdims=True)
    i1 = jnp.min(jnp.where(el == v1, lane, LANES), axis=-1, keepdims=True)
    el2 = jnp.where(lane == i1, -jnp.inf, el)
    v2 = jnp.max(el2, axis=-1, keepdims=True)
    i2 = jnp.min(jnp.where(el2 == v2, lane, LANES), axis=-1, keepdims=True)
    e21 = jnp.exp(v2 - v1)
    p1 = 1.0 / (1.0 + e21)
    w1 = g_w * p1
    w2 = g_w * (e21 * p1)
    hot1 = lane == i1
    hot2 = lane == i2
    cnt = jnp.where(hot1 | hot2, 1.0, 0.0)
    r = lax.broadcasted_iota(jnp.int32, (tm, tm), 0)
    c = lax.broadcasted_iota(jnp.int32, (tm, tm), 1)
    tri = jnp.where(c < r, 1.0, 0.0).astype(BF16)
    before = jnp.dot(tri, cnt.astype(BF16), preferred_element_type=F32) + carry_s[...]
    r1 = jnp.sum(jnp.where(hot1, before, 0.0), axis=-1, keepdims=True)
    r2 = jnp.sum(jnp.where(hot2, before, 0.0), axis=-1, keepdims=True)
    carry_s[...] = carry_s[...] + jnp.sum(cnt, axis=0, keepdims=True)
    cnt_ref[...] = carry_s[...]
    route = jnp.zeros((tm, LANES), F32)
    for pos, val in ((_ROUTE_E, (i1 - N_GROUPS).astype(F32)), (_ROUTE_E + 1, (i2 - N_GROUPS).astype(F32)),
                     (_ROUTE_W, w1), (_ROUTE_W + 1, w2), (_ROUTE_RANK, r1), (_ROUTE_RANK + 1, r2)):
        route = jnp.where(lane == pos, val, route)
    route_ref[...] = route


def moe_router(y, g, wr, br):
    m, d = y.shape
    tm = _row_tile(m, 256)
    return pl.pallas_call(
        _router_body,
        grid=(m // tm,),
        in_specs=[pl.BlockSpec((tm, d), lambda i: (i, 0)), pl.BlockSpec((1, d), lambda i: (0, 0)),
                  pl.BlockSpec((d, LANES), lambda i: (0, 0)), pl.BlockSpec((1, LANES), lambda i: (0, 0))],
        out_specs=[pl.BlockSpec((tm, d), lambda i: (i, 0)), pl.BlockSpec((tm, LANES), lambda i: (i, 0)),
                   pl.BlockSpec((1, LANES), lambda i: (0, 0))],
        out_shape=[jax.ShapeDtypeStruct((m, d), F32), jax.ShapeDtypeStruct((m, LANES), F32),
                   jax.ShapeDtypeStruct((1, LANES), F32)],
        scratch_shapes=[pltpu.VMEM((1, LANES), F32)],
        compiler_params=_cparams(("arbitrary",)),
        name="moe_router",
    )(y, g.reshape(1, d), wr, br)


def _row_copy(src, s_row, dst, d_row, sem):
    return pltpu.make_async_copy(src.at[pl.ds(s_row, 1), :], dst.at[pl.ds(d_row, 1), :], sem)


def _scatter_body(dest_ref, xn_ref, xs_in_ref, xs_ref, sem):
    del xs_in_ref
    tm = xn_ref.shape[0]
    base = pl.program_id(0) * tm

    def issue(r, _):
        for s in range(2):
            _row_copy(xn_ref, r, xs_ref, dest_ref[2 * (base + r) + s], sem).start()
        return 0

    lax.fori_loop(0, tm, issue, 0, unroll=4)
    for _ in range(2):
        pltpu.make_async_copy(xn_ref, xs_ref.at[pl.ds(0, tm), :], sem).wait()


def moe_scatter(dest, xn, rows):
    m, d = xn.shape
    tm = _row_tile(m, 256)
    grid_spec = pltpu.PrefetchScalarGridSpec(
        num_scalar_prefetch=1,
        grid=(m // tm,),
        in_specs=[pl.BlockSpec((tm, d), lambda i, dest: (i, 0)), pl.BlockSpec(memory_space=pl.ANY)],
        out_specs=pl.BlockSpec(memory_space=pl.ANY),
        scratch_shapes=[pltpu.SemaphoreType.DMA(())],
    )
    return pl.pallas_call(
        _scatter_body,
        grid_spec=grid_spec,
        out_shape=jax.ShapeDtypeStruct((rows, d), F32),
        input_output_aliases={2: 0},
        compiler_params=_cparams(("arbitrary",)),
        name="moe_scatter",
    )(dest, xn, jnp.zeros((rows, d), F32))


def _expert_body(blk_e_ref, nused_ref, xs_ref, w1_ref, w3_ref, w2_ref, ys_ref):
    del blk_e_ref
    used = pl.program_id(0) < nused_ref[0]

    @pl.when(used)
    def _():
        x = xs_ref[...].astype(BF16)
        h = jax.nn.silu(jnp.dot(x, w1_ref[0], preferred_element_type=F32))
        h = h * jnp.dot(x, w3_ref[0], preferred_element_type=F32)
        ys_ref[...] = jnp.dot(h.astype(BF16), w2_ref[0], preferred_element_type=F32)

    @pl.when(jnp.logical_not(used))
    def _():
        ys_ref[...] = jnp.zeros_like(ys_ref)


def moe_experts(blk_e, nused, xs, w1, w3, w2):
    rows, d = xs.shape
    te = MOE_BLOCK_ROWS
    de = w1.shape[2]
    grid_spec = pltpu.PrefetchScalarGridSpec(
        num_scalar_prefetch=2,
        grid=(rows // te,),
        in_specs=[pl.BlockSpec((te, d), lambda i, be, nu: (i, 0)),
                  pl.BlockSpec((1, d, de), lambda i, be, nu: (be[i], 0, 0)),
                  pl.BlockSpec((1, d, de), lambda i, be, nu: (be[i], 0, 0)),
                  pl.BlockSpec((1, de, d), lambda i, be, nu: (be[i], 0, 0))],
        out_specs=pl.BlockSpec((te, d), lambda i, be, nu: (i, 0)),
    )
    return pl.pallas_call(
        _expert_body,
        grid_spec=grid_spec,
        out_shape=jax.ShapeDtypeStruct((rows, d), F32),
        compiler_params=_cparams(("arbitrary",)),
        name="moe_experts",
    )(blk_e, nused, xs, w1, w3, w2)


def _combine_body(dest_ref, route_ref, res_ref, ys_ref, o_ref, buf, sem):
    tm = res_ref.shape[0]
    base = pl.program_id(0) * tm

    def issue(r, _):
        for s in range(2):
            _row_copy(ys_ref, dest_ref[2 * (base + r) + s], buf.at[s], r, sem).start()
        return 0

    lax.fori_loop(0, tm, issue, 0, unroll=4)
    for s in range(2):
        pltpu.make_async_copy(ys_ref.at[pl.ds(0, tm), :], buf.at[s], sem).wait()
    route = route_ref[...]
    w1 = route[:, _ROUTE_W:_ROUTE_W + 1]
    w2 = route[:, _ROUTE_W + 1:_ROUTE_W + 2]
    o_ref[...] = res_ref[...] + (w1 * buf[0] + w2 * buf[1])


def moe_combine(dest, route, res, ys):
    m, d = res.shape
    tm = _row_tile(m, 256)
    grid_spec = pltpu.PrefetchScalarGridSpec(
        num_scalar_prefetch=1,
        grid=(m // tm,),
        in_specs=[pl.BlockSpec((tm, LANES), lambda i, dest: (i, 0)),
                  pl.BlockSpec((tm, d), lambda i, dest: (i, 0)),
                  pl.BlockSpec(memory_space=pl.ANY)],
        out_specs=pl.BlockSpec((tm, d), lambda i, dest: (i, 0)),
        scratch_shapes=[pltpu.VMEM((2, tm, d), F32), pltpu.SemaphoreType.DMA(())],
    )
    return pl.pallas_call(
        _combine_body,
        grid_spec=grid_spec,
        out_shape=jax.ShapeDtypeStruct((m, d), F32),
        compiler_params=_cparams(("arbitrary",)),
        name="moe_combine",
    )(dest, route, res, ys)


def hier_moe_residual(y, g, wr, br, w1, w3, w2):
    m, d = y.shape
    te = MOE_BLOCK_ROWS
    xn, route, cnt = moe_router(y, g, wr, br)
    counts = cnt[0, N_GROUPS:N_GROUPS + N_EXPERTS].astype(jnp.int32)
    padded = (counts + te - 1) // te * te
    pend = jnp.cumsum(padded)
    pstart = pend - padded
    e = route[:, _ROUTE_E:_ROUTE_E + 2].astype(jnp.int32)
    rank = route[:, _ROUTE_RANK:_ROUTE_RANK + 2].astype(jnp.int32)
    dest = (pstart[e] + rank).reshape(-1)
    n_blk = -(-(2 * m + N_EXPERTS * (te - 1)) // te)
    blk_start = jnp.arange(n_blk, dtype=jnp.int32) * te
    blk_e = jnp.minimum(jnp.sum((pend[None, :] <= blk_start[:, None]).astype(jnp.int32), axis=1), N_EXPERTS - 1)
    nused = (pend[-1:] // te).astype(jnp.int32)
    xs = moe_scatter(dest, xn, n_blk * te)
    ys = moe_experts(blk_e, nused, xs, w1, w3, w2)
    return moe_combine(dest, route, y, ys)


def _router_weights(w_rg, b_rg, w_re, b_re):
    d = w_rg.shape[0]
    pad = LANES - N_GROUPS - N_EXPERTS
    wr = jnp.concatenate([w_rg, w_re, jnp.zeros((d, pad), F32)], axis=1)
    br = jnp.concatenate([b_rg, b_re, jnp.zeros((pad,), F32)]).reshape(1, LANES)
    return wr, br


_POS_LANE = HEAD_DIM
_TILE_BIAS_LANE = _POS_LANE + 4


def _compress_tail(ab, bias, w2_ref, add_pos):
    c = ab.shape[0] - SUBLANES
    gw = ab.shape[1] // 2
    h = ab[0:c, 0:gw] + ab[1:c + 1, gw:2 * gw] + bias
    out = jnp.dot(jax.nn.gelu(h).astype(BF16), w2_ref[...], preferred_element_type=F32)
    if add_pos:
        lane = lax.broadcasted_iota(jnp.int32, out.shape, 1) % LANES
        cend = lax.broadcasted_iota(jnp.int32, out.shape, 0) * CMP_STRIDE + (CMP_LEN - 1)
        hi = ((cend // 256) * 256).astype(F32)
        lo = (cend % 256).astype(F32)
        out = jnp.where((lane == _POS_LANE) | (lane == _POS_LANE + 1), hi,
                        jnp.where((lane == _POS_LANE + 2) | (lane == _POS_LANE + 3), lo, out))
    return out


def _compress_body(add_pos, n_k, *refs):
    k_refs = refs[:n_k]
    wab_ref, pe_ref, w2_ref, o_ref, ab_s = refs[n_k:]
    c = k_refs[0].shape[1] // CMP_STRIDE
    gw = n_k * LANES
    acc = jnp.zeros((c + SUBLANES, 2 * gw), F32)
    for j in range(CMP_STRIDE):
        xk = jnp.concatenate([r[0, pl.ds(j, c, stride=CMP_STRIDE), :] for r in k_refs], axis=1)
        xj = jnp.concatenate([xk, pe_ref[j:j + 1, :], pe_ref[CMP_STRIDE + j:CMP_STRIDE + j + 1, :],
                              jnp.zeros((SUBLANES - 2, gw), F32)], axis=0).astype(BF16)
        acc = acc + jnp.dot(xj, wab_ref[j], preferred_element_type=F32)
    bias = acc[c:c + 1, 0:gw] + acc[c + 1:c + 2, gw:2 * gw]
    ab_s[0:c, :] = acc[0:c]
    ab_s[c:c + SUBLANES, :] = jnp.zeros((SUBLANES, 2 * gw), F32)
    out = _compress_tail(ab_s, bias, w2_ref, add_pos)
    o_ref[0] = (out if add_pos else out.T).astype(o_ref.dtype)


def _compress_weights(w1, w2, pe, groups):
    d = w2.shape[0]
    w1j = w1.reshape(CMP_LEN, d, d)
    eye = jnp.eye(groups, dtype=F32)
    bd = (eye[None, :, None, :, None] * w1j[:, None, :, None, :]).reshape(CMP_LEN, groups * d, groups * d)
    wab = jnp.concatenate([bd[:CMP_STRIDE], bd[CMP_STRIDE:]], axis=2).astype(BF16)
    pe_t = jnp.tile(pe, (1, groups))
    w2p = jnp.concatenate([w2, jnp.zeros((d, LANES - d), F32)], axis=1)
    w2bd = (eye[:, None, :, None] * w2p[None, :, None, :]).reshape(groups * d, groups * LANES).astype(BF16)
    return wab, pe_t, w2bd


def nsa_compress(k, wab, pe_t, w2bd, add_pos):
    n, t, gw = k.shape
    c = t // CMP_STRIDE
    n_k = gw // LANES
    full = lambda shp: pl.BlockSpec(shp, lambda b: (0,) * len(shp))
    out_dims = (c, w2bd.shape[1]) if add_pos else (w2bd.shape[1], c)
    return pl.pallas_call(
        functools.partial(_compress_body, add_pos, n_k),
        grid=(n,),
        in_specs=[pl.BlockSpec((1, t, LANES), functools.partial(lambda u, b: (b, 0, u), u)) for u in range(n_k)]
        + [full(wab.shape), full(pe_t.shape), full(w2bd.shape)],
        out_specs=pl.BlockSpec((1,) + out_dims, lambda b: (b, 0, 0)),
        out_shape=jax.ShapeDtypeStruct((n,) + out_dims, BF16),
        scratch_shapes=[pltpu.VMEM((c + SUBLANES, 2 * gw), F32)],
        compiler_params=_cparams(("parallel",)),
        name="nsa_compress",
    )(*([k] * n_k), wab, pe_t, w2bd)


def _select_blocks_t(imp, qb, blk):
    vis = blk <= qb
    forced = (blk == 0) | (blk == qb) | (blk == qb - 1)
    score = jnp.where(vis & jnp.logical_not(forced), imp, -jnp.inf)
    blk_f = blk.astype(F32)
    sel = forced
    for _ in range(SEL_TOPK - 3):
        m = jnp.max(score, axis=0, keepdims=True)
        first = jnp.min(jnp.where(score == m, blk_f, jnp.inf), axis=0, keepdims=True)
        pick = blk_f == first
        sel = sel | pick
        score = jnp.where(pick, -jnp.inf, score)
    return sel & vis


def _select_blocks(imp, qb, lane):
    vis = lane <= qb
    forced = (lane == 0) | (lane == qb) | (lane == qb - 1)
    score = jnp.where(vis & jnp.logical_not(forced), imp, -jnp.inf)
    lane_f = lane.astype(F32)
    sel = forced
    for _ in range(SEL_TOPK - 3):
        pick = lane_f == _first_argmax(score, lane_f)
        sel = sel | pick
        score = jnp.where(pick, -jnp.inf, score)
    return sel & vis


def _nsa_prompt_body(slopes_ref, q_ref, gt_ref, ck_ref, cvt_ref, covert_ref, ks_ref, vst_ref, kw_ref, vwt_ref,
                     o_ref):
    g = pl.program_id(1)
    i = pl.program_id(2)
    tq = q_ref.shape[1]
    r_heads = q_ref.shape[2] // HEAD_DIM
    cols = r_heads * tq
    t0 = i * tq
    frow = lax.broadcasted_iota(jnp.int32, (LANES, tq), 0)
    qt_all = q_ref[0].T
    q_cols = []
    for r in range(r_heads):
        slope = slopes_ref[r_heads * g + r]
        s_hi, s_lo = _split_hi_lo(jnp.full((LANES, tq), slope, F32))
        feat = jnp.where(((frow - _POS_LANE) % 2) == 1, s_lo.astype(F32), s_hi.astype(F32))
        qr = jnp.concatenate([qt_all[r * HEAD_DIM:(r + 1) * HEAD_DIM] * _QSCALE,
                              jnp.zeros((LANES - HEAD_DIM, tq), F32)], axis=0)
        q_cols.append(jnp.where(frow < HEAD_DIM, qr,
                                jnp.where(frow < _POS_LANE + 4, feat,
                                          jnp.where(frow == _TILE_BIAS_LANE, 1.0, 0.0))).astype(BF16))
    qt = jnp.concatenate(q_cols, axis=1)
    qoff = lax.broadcasted_iota(jnp.int32, (1, cols), 1) % tq

    ck = ck_ref[0]
    n_cmp = ck.shape[0]
    s_c = jnp.dot(ck, qt, preferred_element_type=F32)
    cend = lax.broadcasted_iota(jnp.int32, (n_cmp, cols), 0) * CMP_STRIDE + (CMP_LEN - 1)
    ok_c = cend <= t0 + qoff
    s_c = jnp.where(ok_c, s_c, NEG_BIG)
    e_c = jnp.where(ok_c, jnp.exp2(s_c - jnp.max(s_c, axis=0, keepdims=True)), 0.0)
    p_c = e_c / jnp.maximum(jnp.sum(e_c, axis=0, keepdims=True), 1e-30)
    o_c = jnp.dot(cvt_ref[0], p_c.astype(BF16), preferred_element_type=F32)

    psum = p_c[:, 0:tq]
    for r in range(1, r_heads):
        psum = psum + p_c[:, r * tq:(r + 1) * tq]
    imp = jnp.dot(covert_ref[...], psum, precision=HIGHEST, preferred_element_type=F32)
    qb = (t0 + lax.broadcasted_iota(jnp.int32, (LANES, tq), 1)) // SEL_BLOCK
    pen = jnp.where(_select_blocks_t(imp, qb, frow), 0.0, NEG_BIG).astype(BF16)
    qs = jnp.concatenate([qt, jnp.concatenate([pen] * r_heads, axis=1)], axis=0)

    tk = tq
    klane = lax.broadcasted_iota(jnp.int32, (tk, LANES), 1)
    krow = lax.broadcasted_iota(jnp.int32, (tk, LANES), 0)
    krow_b = krow.astype(F32).astype(BF16)
    is_pos = (klane == _POS_LANE) | (klane == _POS_LANE + 1)
    is_off = (klane == _POS_LANE + 2) | (klane == _POS_LANE + 3)
    blocks_per_tile = tk // SEL_BLOCK
    vrow = lax.broadcasted_iota(jnp.int32, (LANES, tk), 0)
    one_b = jnp.ones((LANES, tk), BF16)
    key_i = lax.broadcasted_iota(jnp.int32, (tk, cols), 0)

    def keys_with_pos(k_ref, j):
        kt = k_ref[0, pl.ds(pl.multiple_of(j * tk, tk), tk), :]
        off_b = jnp.full((tk, LANES), ((j - i) * tk).astype(F32), F32).astype(BF16)
        return jnp.where(is_pos, krow_b, jnp.where(is_off, off_b, kt))

    def values_with_ones(vt_ref, j):
        return jnp.where(vrow == _POS_LANE, one_b, vt_ref[0, :, pl.ds(pl.multiple_of(j * tk, tk), tk)])

    def sel_scores(j):
        onehot = jnp.where(klane == j * blocks_per_tile + krow // SEL_BLOCK, 1.0, 0.0).astype(BF16)
        return jnp.dot(jnp.concatenate([keys_with_pos(ks_ref, j), onehot], axis=1), qs,
                       preferred_element_type=F32)

    def update(s, v, carry, mask=None):
        m, acc = carry
        if mask is not None:
            s = jnp.where(mask, s, NEG_BIG)
        m_new = jnp.maximum(m, jnp.max(s, axis=0, keepdims=True))
        p = jnp.exp2(s - m_new).astype(BF16)
        return m_new, jnp.exp2(m - m_new) * acc + jnp.dot(v, p, preferred_element_type=F32)

    def sel_body(j, c):
        s, carry = c
        return sel_scores(j + 1), update(s, values_with_ones(vst_ref, j), carry)

    def finish(acc):
        return acc / acc[_POS_LANE:_POS_LANE + 1, :]

    init = (jnp.full((1, cols), -jnp.inf, F32), jnp.zeros((LANES, cols), F32))
    s, carry = lax.fori_loop(0, i, sel_body, (sel_scores(0), init))
    o_s = finish(update(s, values_with_ones(vst_ref, i), carry, key_i <= qoff)[1])

    n_win = WINDOW // tk + 1
    kws, vws = [], []
    for w in range(n_win):
        j_true = i - (n_win - 1) + w
        j = jnp.maximum(j_true, 0)
        tile_bias = jnp.full((tk, LANES), jnp.where(j_true >= 0, 0.0, NEG_BIG), F32).astype(BF16)
        kws.append(jnp.where(klane == _TILE_BIAS_LANE, tile_bias, keys_with_pos(kw_ref, j)))
        vws.append(values_with_ones(vwt_ref, j))
    s_w = jnp.dot(jnp.concatenate(kws, axis=0), qt, preferred_element_type=F32)
    wk = lax.broadcasted_iota(jnp.int32, (n_win * tk, cols), 0)
    newest = (n_win - 1) * tk
    ok_w = ((wk < tk) & (wk > qoff)) | ((wk >= tk) & (wk < newest)) | ((wk >= newest) & (wk - newest <= qoff))
    s_w = jnp.where(ok_w, s_w, NEG_BIG)
    p_w = jnp.exp2(s_w - jnp.max(s_w, axis=0, keepdims=True)).astype(BF16)
    o_w = finish(jnp.dot(jnp.concatenate(vws, axis=1), p_w, preferred_element_type=F32))

    gates = jax.nn.sigmoid(gt_ref[0]).T
    outs = []
    for r in range(r_heads):
        cs = slice(r * tq, (r + 1) * tq)
        gr = [gates[3 * r + b:3 * r + b + 1, :] for b in range(3)]
        outs.append((gr[0] * o_c[:, cs] + gr[1] * o_s[:, cs] + gr[2] * o_w[:, cs])[0:HEAD_DIM])
    o_ref[0] = jnp.concatenate(outs, axis=0).T


def nsa_prompt(q, gt, ck, cvt, cover_t, ks, vst, kw, vwt, slopes):
    n, t, w = q.shape
    groups = ks.shape[2] // LANES
    qw = w // groups
    tq = 256
    c = ck.shape[1]
    assert t % tq == 0 and -(-t // SEL_BLOCK) <= LANES and t // SEL_BLOCK >= SEL_TOPK and WINDOW % tq == 0
    rows_spec = lambda rows: pl.BlockSpec((1, rows, LANES), lambda b, g, i: (b, 0, g))
    cols_spec = lambda cols: pl.BlockSpec((1, LANES, cols), lambda b, g, i: (b, g, 0))
    grid_spec = pltpu.PrefetchScalarGridSpec(
        num_scalar_prefetch=0,
        grid=(n, groups, t // tq),
        in_specs=[pl.BlockSpec(memory_space=pltpu.SMEM),
                  pl.BlockSpec((1, tq, qw), lambda b, g, i: (b, i, g)),
                  pl.BlockSpec((1, tq, LANES), lambda b, g, i: (b, i, g)),
                  rows_spec(c), cols_spec(c),
                  pl.BlockSpec(cover_t.shape, lambda b, g, i: (0, 0)),
                  rows_spec(t), cols_spec(t), rows_spec(t), cols_spec(t)],
        out_specs=pl.BlockSpec((1, tq, qw), lambda b, g, i: (b, i, g)),
    )
    return pl.pallas_call(
        _nsa_prompt_body,
        grid_spec=grid_spec,
        out_shape=jax.ShapeDtypeStruct((n, t, w), F32),
        compiler_params=_cparams(("parallel", "parallel", "arbitrary")),
        name="nsa_prompt",
    )(jnp.asarray(slopes * np.float32(LOG2E)), q, gt, ck, cvt, cover_t, ks, vst, kw, vwt)


def _nsa_cover(n_cmp_rows, n_sel):
    i = np.arange(n_cmp_rows)[:, None]
    j = np.arange(-(-n_sel // LANES) * LANES)[None, :]
    cov = (CMP_STRIDE * i < SEL_BLOCK * (j + 1)) & (CMP_STRIDE * i + CMP_LEN > SEL_BLOCK * j) & (j < n_sel)
    return jnp.asarray(cov.astype(np.float32))


def _pad_group_cols(w, width):
    d, gw = w.shape
    g = gw // width
    return jnp.pad(w.reshape(d, g, width), ((0, 0), (0, 0), (0, LANES - width))).reshape(d, g * LANES)


def _own_head_block(rows, width, nq):
    r = lax.broadcasted_iota(jnp.int32, (rows, width), 0)
    c = lax.broadcasted_iota(jnp.int32, (rows, width), 1)
    return (c // HEAD_DIM) == (r // nq)


def _fold_heads(x, own):
    width = x.shape[1]
    fold = jnp.where(lax.broadcasted_iota(jnp.int32, (width, HEAD_DIM), 0) % HEAD_DIM
                     == lax.broadcasted_iota(jnp.int32, (width, HEAD_DIM), 1), 1.0, 0.0)
    return jnp.dot(jnp.where(own, x, 0.0), fold, precision=HIGHEST, preferred_element_type=F32)


_MOBA_BLOCKS_PER_STEP = 4


def _moba_decode_body(nblk, past, pt_ref, q_ref, slope_ref, kn_ref, vn_ref, *refs):
    del pt_ref
    n_page_refs = _MOBA_BLOCKS_PER_STEP * (MOBA_BLOCK // PAGE_SIZE)
    k_refs, v_refs = refs[:n_page_refs], refs[n_page_refs:2 * n_page_refs]
    o_ref, m_s, l_s, g_s, acc_s = refs[2 * n_page_refs:]
    j = pl.program_id(1)
    rows, width = q_ref.shape[1], q_ref.shape[2]
    nq = rows // (width // HEAD_DIM)
    own = _own_head_block(rows, width, nq)
    q_hi, q_lo = _split_hi_lo(jnp.where(own, q_ref[0] * HEAD_DIM ** -0.5, 0.0))
    slope = slope_ref[...]
    lane = lax.broadcasted_iota(jnp.int32, (rows, LANES), 1)
    lane_f = lane.astype(F32)

    @pl.when(j == 0)
    def _():
        m_s[...] = jnp.full(m_s.shape, NEG_BIG, F32)
        l_s[...] = jnp.zeros(l_s.shape, F32)
        g_s[...] = jnp.zeros(g_s.shape, F32)

    lhs = jnp.concatenate([q_hi, q_lo], axis=0)
    pages_per_blk = MOBA_BLOCK // PAGE_SIZE
    span = jnp.concatenate([lane_f + pg * PAGE_SIZE for pg in range(pages_per_blk)], axis=1)
    for bk in range(_MOBA_BLOCKS_PER_STEP):
        jb = j * _MOBA_BLOCKS_PER_STEP + bk
        pages = slice(bk * pages_per_blk, (bk + 1) * pages_per_blk)
        kt = jnp.concatenate([r[0, 0].reshape(width, PAGE_SIZE) for r in k_refs[pages]], axis=1).astype(BF16)
        vt = jnp.concatenate([r[0, 0].reshape(width, PAGE_SIZE) for r in v_refs[pages]], axis=1).astype(BF16)
        s2 = jnp.dot(lhs, kt, preferred_element_type=F32)
        gate = jnp.sum(s2[:rows] + s2[rows:], axis=-1, keepdims=True)
        s = s2[:rows] + slope * (span + (jb * MOBA_BLOCK - past).astype(F32))
        m_j = jnp.max(s, axis=-1, keepdims=True)
        p = jnp.exp(s - m_j)
        m_s[...] = jnp.where(lane == jb, m_j, m_s[...])
        l_s[...] = jnp.where(lane == jb, jnp.sum(p, axis=-1, keepdims=True), l_s[...])
        g_s[...] = jnp.where(lane == jb, gate, g_s[...])
        acc_s[jb] = _dot_nt(p.astype(BF16), vt)

    @pl.when(j == nblk // _MOBA_BLOCKS_PER_STEP - 1)
    def _():
        g = jnp.where(lane < nblk, g_s[...], -jnp.inf)
        sel = jnp.zeros(g.shape, jnp.bool_)
        for _ in range(MOBA_TOPK):
            pick = lane_f == _first_argmax(g, lane_f)
            sel = sel | pick
            g = jnp.where(pick, -jnp.inf, g)
        sel = sel & (lane < nblk)
        ok_o = (lane < nq) & (lane <= lax.broadcasted_iota(jnp.int32, (rows, LANES), 0) % nq)
        s_o = jnp.dot(q_hi, kn_ref[0].astype(BF16), preferred_element_type=F32) + slope * lane_f
        s_o = jnp.where(ok_o, s_o, NEG_BIG)
        m_o = jnp.max(s_o, axis=-1, keepdims=True)
        p_o = jnp.exp(s_o - m_o)
        l_o = jnp.sum(p_o, axis=-1, keepdims=True)
        acc_o = _dot_nt(p_o.astype(BF16), vn_ref[0].astype(BF16))
        m_all = jnp.maximum(jnp.max(jnp.where(sel, m_s[...], -jnp.inf), axis=-1, keepdims=True), m_o)
        w = jnp.where(sel, jnp.exp(m_s[...] - m_all), 0.0)
        w_o = jnp.exp(m_o - m_all)
        den = jnp.sum(w * l_s[...], axis=-1, keepdims=True) + w_o * l_o
        num = w_o * acc_o
        for jj in range(nblk):
            num = num + w[:, jj:jj + 1] * acc_s[jj]
        o_ref[0] = _fold_heads(num, own) / den


def moba_decode(q_tiled, kn_t, vn_t, cache_k, cache_v, page_table, layer, slopes):
    b, rows, width = q_tiled.shape
    heads = cache_k.shape[2]
    nq = rows // heads
    n_pages = page_table.shape[1]
    past = n_pages * PAGE_SIZE
    pages_per_blk = MOBA_BLOCK // PAGE_SIZE
    nblk = past // MOBA_BLOCK
    pages_per_step = _MOBA_BLOCKS_PER_STEP * pages_per_blk
    assert past % MOBA_BLOCK == 0 and MOBA_TOPK <= nblk <= LANES and nblk % _MOBA_BLOCKS_PER_STEP == 0
    slope_rows = jnp.asarray(np.repeat(slopes, nq).reshape(rows, 1))
    page = lambda pg: pl.BlockSpec((1, 1, heads, HEAD_DIM, PAGE_SIZE),
                                   lambda s, j, pt: (layer, pt[s, pages_per_step * j + pg], 0, 0, 0))
    pages = [page(pg) for pg in range(pages_per_step)]
    per_seq = lambda a: pl.BlockSpec((1,) + a.shape[1:], lambda s, j, pt: (s, 0, 0))
    grid_spec = pltpu.PrefetchScalarGridSpec(
        num_scalar_prefetch=1,
        grid=(b, nblk // _MOBA_BLOCKS_PER_STEP),
        in_specs=[per_seq(q_tiled), pl.BlockSpec((rows, 1), lambda s, j, pt: (0, 0)), per_seq(kn_t), per_seq(vn_t)]
        + pages + pages,
        out_specs=pl.BlockSpec((1, rows, HEAD_DIM), lambda s, j, pt: (s, 0, 0)),
        scratch_shapes=[pltpu.VMEM((rows, LANES), F32), pltpu.VMEM((rows, LANES), F32),
                        pltpu.VMEM((rows, LANES), F32), pltpu.VMEM((nblk, rows, width), F32)],
    )
    return pl.pallas_call(
        functools.partial(_moba_decode_body, nblk, past),
        grid_spec=grid_spec,
        out_shape=jax.ShapeDtypeStruct((b, rows, HEAD_DIM), F32),
        compiler_params=_cparams(("parallel", "arbitrary")),
        name="moba_decode",
    )(page_table, q_tiled, slope_rows, kn_t, vn_t, *([cache_k] * pages_per_step), *([cache_v] * pages_per_step))


_CHUNKS_PER_PAGE = PAGE_SIZE // CMP_STRIDE


def _page_ab_body(k_ref, wj_ref, o_ref, xs):
    pages, groups = k_ref.shape[1], k_ref.shape[2]
    for g in range(groups):
        xs[...] = jnp.swapaxes(k_ref[0, :, g], 1, 2)
        acc = jnp.zeros((pages * _CHUNKS_PER_PAGE, LANES), F32)
        for j in range(CMP_STRIDE):
            x = xs[:, pl.ds(j, _CHUNKS_PER_PAGE, stride=CMP_STRIDE), :]
            x = x.reshape(pages * _CHUNKS_PER_PAGE, HEAD_DIM).astype(BF16)
            acc = acc + jnp.dot(x, wj_ref[j], preferred_element_type=F32)
        o_ref[:, g * LANES:(g + 1) * LANES] = acc


def nsa_page_ab(cache, layer, wj):
    _, n_phys, groups, _, _ = cache.shape
    pages = next(p for p in (32, 16, 8, 4, 2, 1) if n_phys % p == 0)
    return pl.pallas_call(
        _page_ab_body,
        grid=(n_phys // pages,),
        in_specs=[pl.BlockSpec((1, pages, groups, HEAD_DIM, PAGE_SIZE), lambda i: (layer, i, 0, 0, 0)),
                  pl.BlockSpec(wj.shape, lambda i: (0, 0, 0))],
        out_specs=pl.BlockSpec((pages * _CHUNKS_PER_PAGE, groups * LANES), lambda i: (i, 0)),
        out_shape=jax.ShapeDtypeStruct((n_phys * _CHUNKS_PER_PAGE, groups * LANES), F32),
        scratch_shapes=[pltpu.VMEM((pages, PAGE_SIZE, HEAD_DIM), F32)],
        compiler_params=_cparams(("parallel",)),
        name="nsa_page_ab",
    )(cache, wj)


def _page_ab_copy(ab_hbm, pt_ref, s, p, buf, sem):
    return pltpu.make_async_copy(ab_hbm.at[pt_ref[s, p]], buf.at[pl.ds(p * _CHUNKS_PER_PAGE, _CHUNKS_PER_PAGE), :], sem)


def _decode_queries(q_ref, slope_ref):
    q = q_ref[0]
    lane = lax.broadcasted_iota(jnp.int32, q.shape, 1)
    s_hi, s_lo = _split_hi_lo(jnp.broadcast_to(slope_ref[...], q.shape))
    feat = jnp.where(((lane - _POS_LANE) % 2) == 1, s_lo.astype(F32), s_hi.astype(F32))
    return jnp.where(lane < HEAD_DIM, q * HEAD_DIM ** -0.5,
                     jnp.where(lane < _POS_LANE + 4, feat, 0.0)).astype(BF16)


def _nsa_decode_cmp_body(n_pages, past, pt_ref, q_ref, slope_ref, abk_ref, abv_ref, wjk_ref, wjv_ref, pek_ref,
                         pev_ref, w2k_ref, w2v_ref, cover_ref, oc_ref, pen_ref, bufk, bufv, sem):
    s = pl.program_id(0)
    c = n_pages * _CHUNKS_PER_PAGE
    rows = q_ref.shape[1]
    groups = abk_ref.shape[2] // LANES
    rpg = rows // groups
    nq = pen_ref.shape[2] // 2

    def issue(p, _):
        _page_ab_copy(abk_ref, pt_ref, s, p, bufk, sem.at[0]).start()
        _page_ab_copy(abv_ref, pt_ref, s, p, bufv, sem.at[1]).start()
        return 0

    def drain(p, _):
        _page_ab_copy(abk_ref, pt_ref, s, p, bufk, sem.at[0]).wait()
        _page_ab_copy(abv_ref, pt_ref, s, p, bufv, sem.at[1]).wait()
        return 0

    lax.fori_loop(0, n_pages, issue, 0)
    pad = jnp.zeros((SUBLANES, bufk.shape[1]), F32)
    bufk[c:c + SUBLANES, :] = pad
    bufv[c:c + SUBLANES, :] = pad
    lax.fori_loop(0, n_pages, drain, 0)

    def tokens(buf, wj_ref, pe_ref, w2_ref, g, add_pos):
        acc = jnp.zeros((SUBLANES, LANES), F32)
        for j in range(CMP_STRIDE):
            xj = jnp.concatenate([pe_ref[j:j + 1, :], pe_ref[CMP_STRIDE + j:CMP_STRIDE + j + 1, :],
                                  jnp.zeros((SUBLANES - 2, HEAD_DIM), F32)], axis=0).astype(BF16)
            acc = acc + jnp.dot(xj, wj_ref[j], preferred_element_type=F32)
        bias = acc[0:1] + pltpu.roll(acc, HEAD_DIM, 1)[1:2]
        slab = slice(g * LANES, (g + 1) * LANES)
        h = buf[0:c, slab] + pltpu.roll(buf[1:c + 1, slab], HEAD_DIM, 1) + bias
        out = jnp.dot(jax.nn.gelu(h).astype(BF16), w2_ref[...], preferred_element_type=F32)
        if add_pos:
            lane = lax.broadcasted_iota(jnp.int32, out.shape, 1)
            cend = lax.broadcasted_iota(jnp.int32, out.shape, 0) * CMP_STRIDE + (CMP_LEN - 1)
            hi = ((cend // 256) * 256).astype(F32)
            lo = (cend % 256).astype(F32)
            out = jnp.where((lane == _POS_LANE) | (lane == _POS_LANE + 1), hi,
                            jnp.where((lane == _POS_LANE + 2) | (lane == _POS_LANE + 3), lo, out))
        return out.astype(BF16)

    qa = _decode_queries(q_ref, slope_ref)
    for g in range(groups):
        ck = tokens(bufk, wjk_ref, pek_ref, w2k_ref, g, True)
        cv = tokens(bufv, wjv_ref, pev_ref, w2v_ref, g, False)
        s_c = _dot_nt(qa[g * rpg:(g + 1) * rpg], ck)
        cend = lax.broadcasted_iota(jnp.int32, s_c.shape, 1) * CMP_STRIDE + (CMP_LEN - 1)
        t_q = past + lax.broadcasted_iota(jnp.int32, s_c.shape, 0) % nq
        ok = cend <= t_q
        s_c = jnp.where(ok, s_c, NEG_BIG)
        e_c = jnp.where(ok, jnp.exp(s_c - jnp.max(s_c, axis=-1, keepdims=True)), 0.0)
        p_c = e_c / jnp.maximum(jnp.sum(e_c, axis=-1, keepdims=True), 1e-30)
        oc_ref[0, g * rpg:(g + 1) * rpg, :] = jnp.dot(p_c.astype(BF16), cv, preferred_element_type=F32)
        half = p_c[0:2 * nq]
        for u in range(1, rpg // (2 * nq)):
            half = half + p_c[2 * nq * u:2 * nq * (u + 1)]
        psum = half + pltpu.roll(half, nq, 0)
        imp = jnp.dot(psum, cover_ref[...], precision=HIGHEST, preferred_element_type=F32)
        lane = lax.broadcasted_iota(jnp.int32, imp.shape, 1)
        qb = (past + lax.broadcasted_iota(jnp.int32, imp.shape, 0) % nq) // SEL_BLOCK
        pen_ref[0, g] = jnp.where(_select_blocks(imp, qb, lane), 0.0, NEG_BIG)


def nsa_decode_cmp(q_rows, slope_rows, ab_k, ab_v, page_table, wjk, wjv, pek, pev, w2k, w2v, cover, nq):
    b, rows, _ = q_rows.shape
    n_pages = page_table.shape[1]
    past = n_pages * PAGE_SIZE
    groups = ab_k.shape[1] // LANES
    c = n_pages * _CHUNKS_PER_PAGE
    assert past % CMP_STRIDE == 0 and nq < CMP_STRIDE and 2 * nq == SUBLANES
    ab_k3 = ab_k.reshape(-1, _CHUNKS_PER_PAGE, groups * LANES)
    ab_v3 = ab_v.reshape(-1, _CHUNKS_PER_PAGE, groups * LANES)
    full = lambda a: pl.BlockSpec(a.shape, lambda s, pt: (0,) * a.ndim)
    grid_spec = pltpu.PrefetchScalarGridSpec(
        num_scalar_prefetch=1,
        grid=(b,),
        in_specs=[pl.BlockSpec((1, rows, LANES), lambda s, pt: (s, 0, 0)), full(slope_rows),
                  pl.BlockSpec(memory_space=pl.ANY), pl.BlockSpec(memory_space=pl.ANY),
                  full(wjk), full(wjv), full(pek), full(pev), full(w2k), full(w2v), full(cover)],
        out_specs=[pl.BlockSpec((1, rows, LANES), lambda s, pt: (s, 0, 0)),
                   pl.BlockSpec((1, groups, 2 * nq, cover.shape[1]), lambda s, pt: (s, 0, 0, 0))],
        scratch_shapes=[pltpu.VMEM((c + SUBLANES, groups * LANES), F32),
                        pltpu.VMEM((c + SUBLANES, groups * LANES), F32),
                        pltpu.SemaphoreType.DMA((2,))],
    )
    return pl.pallas_call(
        functools.partial(_nsa_decode_cmp_body, n_pages, past),
        grid_spec=grid_spec,
        out_shape=[jax.ShapeDtypeStruct((b, rows, LANES), F32),
                   jax.ShapeDtypeStruct((b, groups, 2 * nq, cover.shape[1]), F32)],
        compiler_params=_cparams(("arbitrary",)),
        name="nsa_decode_cmp",
    )(page_table, q_rows, slope_rows, ab_k3, ab_v3, wjk, wjv, pek, pev, w2k, w2v, cover)


_SEL_PAGES_PER_STEP = 8


def _nsa_decode_sel_body(n_steps, past, wbuf, pt_ref, q_ref, slope_ref, pen_ref, oc_ref, gt_ref, ksn_ref, vsn_ref,
                         kwn_ref, vwn_ref, wk_ref, wv_ref, *refs):
    del pt_ref
    pp = _SEL_PAGES_PER_STEP
    k_refs, v_refs = refs[:pp], refs[pp:2 * pp]
    o_ref, m_s, l_s, acc_s = refs[2 * pp:]
    step = pl.program_id(1)
    rows, width = q_ref.shape[1], q_ref.shape[2]
    groups = pen_ref.shape[1]
    rpg = rows // groups
    nq = pen_ref.shape[2] // 2
    n_sel = pen_ref.shape[3]
    own = _own_head_block(rows, width, rpg)
    qb = jnp.where(own, q_ref[0] * HEAD_DIM ** -0.5, 0.0).astype(BF16)
    slope = slope_ref[...]
    blocks_per_page = PAGE_SIZE // SEL_BLOCK
    pen_rows = jnp.concatenate([pen_ref[0, g] for g in range(groups) for _ in range(rpg // (2 * nq))],
                               axis=0).astype(BF16)

    @pl.when(step == 0)
    def _():
        m_s[...] = jnp.full(m_s.shape, -jnp.inf, F32)
        l_s[...] = jnp.zeros(l_s.shape, F32)
        acc_s[...] = jnp.zeros(acc_s.shape, F32)

    def attend(carry, kt, vt, bias, mask=None):
        m, l, acc = carry
        s = jnp.dot(qb, kt, preferred_element_type=F32) + bias
        if mask is not None:
            s = jnp.where(mask, s, NEG_BIG)
        m_new = jnp.maximum(m, jnp.max(s, axis=-1, keepdims=True))
        p = jnp.exp(s - m_new)
        alpha = jnp.exp(m - m_new)
        return m_new, alpha * l + jnp.sum(p, axis=-1, keepdims=True), alpha * acc + _dot_nt(p.astype(BF16), vt)

    col = lax.broadcasted_iota(jnp.int32, (rows, PAGE_SIZE), 1)
    col_f = col.astype(F32)
    span = pp * PAGE_SIZE
    e_row = lax.broadcasted_iota(jnp.int32, (n_sel, span), 0)
    e_col = lax.broadcasted_iota(jnp.int32, (n_sel, span), 1)
    expand = jnp.where(e_row == step * (pp * blocks_per_page) + e_col // SEL_BLOCK, 1.0, 0.0).astype(BF16)
    pos = lax.broadcasted_iota(jnp.int32, (rows, span), 1).astype(F32) + (step * span - past).astype(F32)
    bias = jnp.dot(pen_rows, expand, preferred_element_type=F32) + slope * pos
    kt = jnp.concatenate([r[0, 0].reshape(width, PAGE_SIZE) for r in k_refs], axis=1).astype(BF16)
    vt = jnp.concatenate([r[0, 0].reshape(width, PAGE_SIZE) for r in v_refs], axis=1).astype(BF16)
    m_s[...], l_s[...], acc_s[...] = attend((m_s[...], l_s[...], acc_s[...]), kt, vt, bias)

    @pl.when(step == n_steps - 1)
    def _():
        gates = jax.nn.sigmoid(gt_ref[0])
        qi = lax.broadcasted_iota(jnp.int32, (rows, PAGE_SIZE), 0) % nq
        ok_n = (col <= qi) & (col < nq)
        pen_new = pen_rows[:, past // SEL_BLOCK:past // SEL_BLOCK + 1].astype(F32)
        _, l_sel, acc_sel = attend((m_s[...], l_s[...], acc_s[...]), ksn_ref[0].astype(BF16),
                                   vsn_ref[0].astype(BF16), slope * col_f + pen_new, ok_n)
        o_sel = _fold_heads(acc_sel, own) / l_sel
        wqi = lax.broadcasted_iota(jnp.int32, (rows, wbuf), 0) % nq
        wc = lax.broadcasted_iota(jnp.int32, (rows, wbuf), 1)
        dist_w = wqi + wbuf - wc
        ok_w = (dist_w >= 0) & (dist_w < WINDOW) & (past - wbuf + wc >= 0)
        init = (jnp.full((rows, 1), -jnp.inf, F32), jnp.zeros((rows, 1), F32), jnp.zeros((rows, width), F32))
        carry_w = attend(init, wk_ref[0, 0].reshape(width, wbuf).astype(BF16),
                         wv_ref[0, 0].reshape(width, wbuf).astype(BF16), slope * (wc - wbuf).astype(F32), ok_w)
        _, l_win, acc_win = attend(carry_w, kwn_ref[0].astype(BF16), vwn_ref[0].astype(BF16), slope * col_f, ok_n)
        o_win = _fold_heads(acc_win, own) / l_win
        o_ref[0] = (gates[:, 0:1] * oc_ref[0, :, 0:HEAD_DIM] + gates[:, 1:2] * o_sel + gates[:, 2:3] * o_win)


def nsa_decode_sel(q_tiled, slope_rows, pen, o_c, gt_rows, ks_new, vs_new, kw_new, vw_new, win_k, win_v,
                   cache_k, cache_v, page_table, layer):
    b, rows, _ = q_tiled.shape
    groups = cache_k.shape[2]
    n_pages = page_table.shape[1]
    past = n_pages * PAGE_SIZE
    pp = _SEL_PAGES_PER_STEP
    wbuf = win_k.shape[4]
    assert n_pages % pp == 0 and past % SEL_BLOCK == 0
    n_steps = n_pages // pp
    per_seq = lambda a: pl.BlockSpec((1,) + a.shape[1:], lambda s, j, pt: (s,) + (0,) * (a.ndim - 1))
    page = lambda u: pl.BlockSpec((1, 1, groups, HEAD_DIM, PAGE_SIZE),
                                  lambda s, j, pt: (layer, pt[s, pp * j + u], 0, 0, 0))
    win = pl.BlockSpec((1, 1, groups, HEAD_DIM, wbuf), lambda s, j, pt: (layer, s, 0, 0, 0))
    grid_spec = pltpu.PrefetchScalarGridSpec(
        num_scalar_prefetch=1,
        grid=(b, n_steps),
        in_specs=[per_seq(q_tiled), pl.BlockSpec(slope_rows.shape, lambda s, j, pt: (0, 0)),
                  per_seq(pen), per_seq(o_c), per_seq(gt_rows),
                  per_seq(ks_new), per_seq(vs_new), per_seq(kw_new), per_seq(vw_new), win, win]
        + [page(u) for u in range(pp)] * 2,
        out_specs=pl.BlockSpec((1, rows, HEAD_DIM), lambda s, j, pt: (s, 0, 0)),
        scratch_shapes=[pltpu.VMEM((rows, 1), F32), pltpu.VMEM((rows, 1), F32),
                        pltpu.VMEM((rows, groups * HEAD_DIM), F32)],
    )
    return pl.pallas_call(
        functools.partial(_nsa_decode_sel_body, n_steps, past, wbuf),
        grid_spec=grid_spec,
        out_shape=jax.ShapeDtypeStruct((b, rows, HEAD_DIM), F32),
        compiler_params=_cparams(("parallel", "arbitrary")),
        name="nsa_decode_sel",
    )(page_table, q_tiled, slope_rows, pen, o_c, gt_rows, ks_new, vs_new, kw_new, vw_new, win_k, win_v,
      *([cache_k] * pp), *([cache_v] * pp))


def _heads_to_rows(x, n, t, heads):
    return x.reshape(n, t, heads, HEAD_DIM).transpose(0, 2, 1, 3).reshape(n, heads * t, HEAD_DIM)


def _rows_to_heads(x, n, t, heads):
    return x.reshape(n, heads, t, HEAD_DIM).transpose(0, 2, 1, 3).reshape(n * t, heads * HEAD_DIM)


def _new_kv_cols(x, n, t):
    return jnp.pad(x.reshape(n, t, x.shape[-1]).transpose(0, 2, 1), ((0, 0), (0, 0), (0, PAGE_SIZE - t)))


def _heads_last(a, heads):
    n, _, t = a.shape
    return a.reshape(n, heads, HEAD_DIM, t).transpose(0, 3, 1, 2)


def _tokens_minor(a):
    return a.transpose(0, 1, 3, 4, 2)


def _layer_ab(layer, l, yp, ys, dims, p):
    bp, tp, bs, ts = dims
    d_rnn = p["conv_w"].shape[2]
    heads = (p["w_in_ab"].shape[2] - 2 * d_rnn) // (3 * HEAD_DIM)
    hw = heads * HEAD_DIM
    w_in = p["w_in_ab"][l].astype(BF16)
    cuts = [0, d_rnn, 2 * d_rnn, 2 * d_rnn + hw, 2 * d_rnn + 2 * hw, 2 * d_rnn + 3 * hw]
    ws = [w_in[:, a:b] for a, b in zip(cuts[:-1], cuts[1:])]
    w_out = p["w_out_ab"][l].astype(BF16)
    w_out = [w_out[:d_rnn], w_out[d_rnn:]]
    rg = (p["conv_w"][l], p["conv_b"][l], _block_diag(p["w_rg_a"][l]).astype(BF16), p["b_rg_a"][l],
          _block_diag(p["w_rg_i"][l]).astype(BF16), p["b_rg_i"][l], p["rg_lambda"][l])
    slopes = _alibi_slopes(heads)
    g = p["norm_mix"][layer]
    f32x5 = [(i, F32) for i in range(5)]

    xr, gr, q, k, k_fm, v_fm, kb, vt = norm_proj(
        yp, g, ws, f32x5[:4] + [(3, F32, True), (4, F32, True), (3, BF16), (4, BF16, True)], seq_len=tp)
    seq = lambda a: a.reshape(bp, tp, a.shape[-1])
    r_p, conv_p, h_p = rglru(seq(xr), seq(gr), jnp.zeros((bp, CONV_W - 1, d_rnn), F32),
                             jnp.zeros((bp, d_rnn), F32), *rg)
    att = moba_prompt(seq(q), seq(kb), vt, block_mean(seq(k), MOBA_BLOCK), slopes)
    yp = proj_residual([r_p.reshape(bp * tp, d_rnn), att.reshape(bp * tp, hw)], w_out, yp)
    out_p = (_heads_last(k_fm, heads), _heads_last(v_fm, heads), conv_p, h_p)

    xr, gr, q, k, v = norm_proj(ys, g, ws, f32x5)
    seq = lambda a: a.reshape(bs, ts, a.shape[-1])
    r_s, conv_s, h_s = rglru(seq(xr), seq(gr), p["state_conv"][l], p["state_rglru"][l], *rg)
    att = moba_decode(jnp.tile(_heads_to_rows(q, bs, ts, heads), (1, 1, heads)), _new_kv_cols(k, bs, ts),
                      _new_kv_cols(v, bs, ts), _tokens_minor(p["cache_moba_k"]), _tokens_minor(p["cache_moba_v"]),
                      p["page_table"], l, slopes)
    ys = proj_residual([r_s.reshape(bs * ts, d_rnn), _rows_to_heads(att, bs, ts, heads)], w_out, ys)
    out_s = (k.reshape(bs, ts, heads, HEAD_DIM), v.reshape(bs, ts, heads, HEAD_DIM), conv_s, h_s)
    return yp, ys, out_p, out_s


def _layer_c(layer, l, yp, ys, dims, p):
    bp, tp, bs, ts = dims
    groups = p["cache_cmp_k"].shape[3]
    gw = groups * HEAD_DIM
    w_in = p["w_in_c"][l]
    heads = (w_in.shape[1] - 6 * gw) // (HEAD_DIM + 3)
    qw = heads * HEAD_DIM
    cuts = [0, qw] + [qw + (i + 1) * gw for i in range(6)]
    w_f32 = [w_in[:, a:b] for a, b in zip(cuts[:-1], cuts[1:])]
    w_gt = w_in[:, cuts[-1]:]
    slopes = _alibi_slopes(heads)
    g = p["norm_mix"][layer]
    w_out = [p["w_out_c"][l].astype(BF16)]
    cmp_k = (p["w_cmp_k1"][l], p["w_cmp_k2"][l], p["pe_cmp_k"][l])
    cmp_v = (p["w_cmp_v1"][l], p["w_cmp_v2"][l], p["pe_cmp_v"][l])
    kv4 = lambda a, n, t: a.reshape(n, t, groups, HEAD_DIM)

    ws = ([w.astype(BF16) for w in w_f32] + [_pad_group_cols(w_gt, 3 * heads // groups).astype(BF16)]
          + [_pad_group_cols(w, HEAD_DIM).astype(BF16) for w in w_f32[3:7]])
    outs = ([(0, F32), (1, F32), (2, F32), (7, F32)] + [(i, F32, True) for i in range(1, 7)]
            + [(8, BF16), (9, BF16, True), (10, BF16), (11, BF16, True)])
    q, kc, vc, gt, *kv_fm, ksp, vst, kwp, vwt = norm_proj(yp, g, ws, outs, seq_len=tp)
    seq = lambda a: a.reshape(bp, tp, a.shape[-1])
    ck = nsa_compress(seq(kc), *_compress_weights(*cmp_k, groups), True)
    cvt = nsa_compress(seq(vc), *_compress_weights(*cmp_v, groups), False)
    cover_t = _nsa_cover(tp // CMP_STRIDE, max(-(-tp // SEL_BLOCK), SEL_TOPK)).T
    att = nsa_prompt(seq(q), seq(gt), ck, cvt, cover_t, seq(ksp), vst, seq(kwp), vwt, slopes)
    yp = proj_residual([att.reshape(bp * tp, qw)], w_out, yp)
    w_keep = min(WINDOW, tp)
    out_p = (tuple(_heads_last(a, groups) for a in kv_fm[:4])
             + tuple(_heads_last(a[:, :, tp - w_keep:], groups) for a in kv_fm[4:]))

    ws = [w.astype(BF16) for w in w_f32] + [jnp.pad(w_gt, ((0, 0), (0, LANES - w_gt.shape[1]))).astype(BF16)]
    q, kc, vc, ks, vs, kw, vw, gt = norm_proj(ys, g, ws, [(i, F32) for i in range(8)])
    q_rows = _heads_to_rows(q, bs, ts, heads)
    q_tiled = jnp.tile(q_rows, (1, 1, groups))
    q_rows = jnp.pad(q_rows, ((0, 0), (0, 0), (0, LANES - HEAD_DIM)))
    slope_rows = jnp.asarray(np.repeat(slopes, ts).reshape(heads * ts, 1))
    gt_rows = gt[:, :3 * heads].reshape(bs, ts, heads, 3).transpose(0, 2, 1, 3).reshape(bs, heads * ts, 3)
    n_pages = p["page_table"].shape[1]
    past = n_pages * PAGE_SIZE
    cover = _nsa_cover(n_pages * _CHUNKS_PER_PAGE, max(-(-(past + ts) // SEL_BLOCK), SEL_TOPK))
    dec_w = []
    for w1, w2, pe in (cmp_k, cmp_v):
        w1j = w1.reshape(CMP_LEN, HEAD_DIM, HEAD_DIM)
        dec_w.append((jnp.concatenate([w1j[:CMP_STRIDE], w1j[CMP_STRIDE:]], axis=2).astype(BF16), pe,
                      jnp.pad(w2, ((0, LANES - HEAD_DIM), (0, LANES - HEAD_DIM))).astype(BF16)))
    (wjk, pek, w2k), (wjv, pev, w2v) = dec_w
    ab_k = nsa_page_ab(_tokens_minor(p["cache_cmp_k"]), l, wjk)
    ab_v = nsa_page_ab(_tokens_minor(p["cache_cmp_v"]), l, wjv)
    o_c, pen = nsa_decode_cmp(q_rows, slope_rows, ab_k, ab_v, p["page_table"], wjk, wjv, pek, pev, w2k, w2v,
                              cover, ts)
    att = nsa_decode_sel(q_tiled, slope_rows, pen, o_c, gt_rows, _new_kv_cols(ks, bs, ts),
                         _new_kv_cols(vs, bs, ts), _new_kv_cols(kw, bs, ts), _new_kv_cols(vw, bs, ts),
                         _tokens_minor(p["state_win_k"]), _tokens_minor(p["state_win_v"]),
                         _tokens_minor(p["cache_sel_k"]), _tokens_minor(p["cache_sel_v"]), p["page_table"], l)
    ys = proj_residual([_rows_to_heads(att, bs, ts, heads)], w_out, ys)
    w_buf = p["state_win_k"].shape[2]
    win = lambda state, new: jnp.concatenate([state[l], kv4(new, bs, ts)], axis=1)[:, -w_buf:]
    out_s = tuple(kv4(a, bs, ts) for a in (kc, vc, ks, vs)) + (win(p["state_win_k"], kw), win(p["state_win_v"], vw))
    return yp, ys, out_p, out_s


def kernel(x_prompt, x_sample, cache_moba_k, cache_moba_v, cache_cmp_k, cache_cmp_v, cache_sel_k, cache_sel_v,
           state_win_k, state_win_v, state_conv, state_rglru, page_table, norm_mix, norm_ffn, norm_final,
           w_in_ab, w_out_ab, conv_w, conv_b, w_rg_a, b_rg_a, w_rg_i, b_rg_i, rg_lambda, w_in_c, w_out_c,
           w_cmp_k1, w_cmp_k2, pe_cmp_k, w_cmp_v1, w_cmp_v2, pe_cmp_v, w_route_group, b_route_group,
           w_route_expert, b_route_expert, w_exp_gate, w_exp_up, w_exp_down):
    p = dict(cache_moba_k=cache_moba_k, cache_moba_v=cache_moba_v, cache_cmp_k=cache_cmp_k,
             cache_cmp_v=cache_cmp_v, cache_sel_k=cache_sel_k, cache_sel_v=cache_sel_v,
             state_win_k=state_win_k, state_win_v=state_win_v, state_conv=state_conv, state_rglru=state_rglru,
             page_table=page_table, norm_mix=norm_mix, w_in_ab=w_in_ab, w_out_ab=w_out_ab, conv_w=conv_w,
             conv_b=conv_b, w_rg_a=w_rg_a, b_rg_a=b_rg_a, w_rg_i=w_rg_i, b_rg_i=b_rg_i, rg_lambda=rg_lambda,
             w_in_c=w_in_c, w_out_c=w_out_c, w_cmp_k1=w_cmp_k1, w_cmp_k2=w_cmp_k2, pe_cmp_k=pe_cmp_k,
             w_cmp_v1=w_cmp_v1, w_cmp_v2=w_cmp_v2, pe_cmp_v=pe_cmp_v)
    bp, tp, d = x_prompt.shape
    bs, ts, _ = x_sample.shape
    dims = (bp, tp, bs, ts)
    yp = x_prompt.reshape(bp * tp, d)
    ys = x_sample.reshape(bs * ts, d)
    ab_p, ab_s, c_p, c_s = [], [], [], []
    for layer in range(norm_mix.shape[0]):
        l = layer // 2
        if layer % 2 == 0:
            yp, ys, o_p, o_s = _layer_ab(layer, l, yp, ys, dims, p)
            ab_p.append(o_p)
            ab_s.append(o_s)
        else:
            yp, ys, o_p, o_s = _layer_c(layer, l, yp, ys, dims, p)
            c_p.append(o_p)
            c_s.append(o_s)
        wr, br = _router_weights(w_route_group[layer], b_route_group[layer], w_route_expert[layer],
                                 b_route_expert[layer])
        moe_w = (w_exp_gate[layer].astype(BF16), w_exp_up[layer].astype(BF16), w_exp_down[layer].astype(BF16))
        yp = hier_moe_residual(yp, norm_ffn[layer], wr, br, *moe_w)
        ys = hier_moe_residual(ys, norm_ffn[layer], wr, br, *moe_w)
    y_prompt = rmsnorm(yp, norm_final).reshape(bp, tp, d)
    y_sample = rmsnorm(ys, norm_final).reshape(bs, ts, d)
    stack = lambda outs, i: jnp.stack([o[i] for o in outs])
    res = [y_prompt, y_sample]
    for i in range(4):
        res += [stack(ab_p, i), stack(ab_s, i)]
    for i in range(6):
        res += [stack(c_p, i), stack(c_s, i)]
    return tuple(res)
```

```python
import functools

import numpy as np
import jax
import jax.numpy as jnp
from jax import lax
from jax.experimental import pallas as pl
from jax.experimental.pallas import tpu as pltpu

F32 = jnp.float32
BF16 = jnp.bfloat16
HIGHEST = lax.Precision.HIGHEST

HEAD_DIM = 64
LANES = 128
SUBLANES = 8
CONV_W = 4
RG_C = 8.0
MOBA_BLOCK = 256
MOBA_TOPK = 3
CMP_STRIDE = 16
CMP_LEN = 32
SEL_BLOCK = 64
SEL_TOPK = 16
WINDOW = 512
N_GROUPS = 4
EXPERTS_PER_GROUP = 8
N_EXPERTS = N_GROUPS * EXPERTS_PER_GROUP
PAGE_SIZE = 128
RMS_EPS = 1e-6
NEG_BIG = -1e30
VMEM_LIMIT = 56 * 1024 * 1024
MOE_BLOCK_ROWS = 256


def _cparams(sem):
    return pltpu.CompilerParams(dimension_semantics=sem, vmem_limit_bytes=VMEM_LIMIT)


def _alibi_slopes(n):
    return np.asarray(2.0 ** (-8.0 * np.arange(1, n + 1) / n), dtype=np.float32)


def _row_tile(m, pref=512):
    return pref if m % pref == 0 else m


def _dot_nt(a, b, precision=None):
    return lax.dot_general(a, b, (((1,), (1,)), ((), ())), precision=precision,
                           preferred_element_type=F32)


def _norm_proj_body(out_map, x_ref, g_ref, *refs):
    n_w = max(wi for wi, _ in out_map) + 1
    w_refs, o_refs = refs[:n_w], refs[n_w:]
    x = x_ref[...]
    y = x * lax.rsqrt(jnp.mean(x * x, axis=-1, keepdims=True) + RMS_EPS) * g_ref[...]
    yb = y.astype(BF16)
    prods = [jnp.dot(yb, w_ref[...], preferred_element_type=F32) for w_ref in w_refs]
    for (wi, feature_major), o_ref in zip(out_map, o_refs):
        if feature_major:
            o_ref[0] = prods[wi].T.astype(o_ref.dtype)
        else:
            o_ref[...] = prods[wi].astype(o_ref.dtype)


def norm_proj(x, g, ws, outs, seq_len=None):
    m, d = x.shape
    tm = _row_tile(m)
    outs = [o if len(o) == 3 else (o[0], o[1], False) for o in outs]
    out_map = tuple((wi, fm) for wi, _, fm in outs)
    tiles = (seq_len or tm) // tm
    out_specs, out_shape = [], []
    for wi, dt, fm in outs:
        cols = ws[wi].shape[1]
        if fm:
            assert seq_len % tm == 0
            out_specs.append(pl.BlockSpec((1, cols, tm), lambda i: (i // tiles, 0, i % tiles)))
            out_shape.append(jax.ShapeDtypeStruct((m // seq_len, cols, seq_len), dt))
        else:
            out_specs.append(pl.BlockSpec((tm, cols), lambda i: (i, 0)))
            out_shape.append(jax.ShapeDtypeStruct((m, cols), dt))
    return pl.pallas_call(
        functools.partial(_norm_proj_body, out_map),
        grid=(m // tm,),
        in_specs=[pl.BlockSpec((tm, d), lambda i: (i, 0)), pl.BlockSpec((1, d), lambda i: (0, 0))]
        + [pl.BlockSpec(w.shape, lambda i: (0, 0)) for w in ws],
        out_specs=out_specs,
        out_shape=out_shape,
        compiler_params=_cparams(("parallel",)),
        name="norm_proj",
    )(x, g.reshape(1, d), *ws)


def _proj_res_body(n_in, *refs):
    x_refs, w_refs = refs[:n_in], refs[n_in:2 * n_in]
    res_ref, o_ref = refs[2 * n_in], refs[2 * n_in + 1]
    acc = jnp.dot(x_refs[0][...].astype(BF16), w_refs[0][...], preferred_element_type=F32)
    for x_ref, w_ref in zip(x_refs[1:], w_refs[1:]):
        acc = acc + jnp.dot(x_ref[...].astype(BF16), w_ref[...], preferred_element_type=F32)
    o_ref[...] = res_ref[...] + acc


def proj_residual(xs, ws, res):
    m, d = res.shape
    tm = _row_tile(m)
    return pl.pallas_call(
        functools.partial(_proj_res_body, len(xs)),
        grid=(m // tm,),
        in_specs=[pl.BlockSpec((tm, x.shape[1]), lambda i: (i, 0)) for x in xs]
        + [pl.BlockSpec(w.shape, lambda i: (0, 0)) for w in ws]
        + [pl.BlockSpec((tm, d), lambda i: (i, 0))],
        out_specs=pl.BlockSpec((tm, d), lambda i: (i, 0)),
        out_shape=jax.ShapeDtypeStruct((m, d), F32),
        compiler_params=_cparams(("parallel",)),
        name="proj_residual",
    )(*xs, *ws, res)


def _rmsnorm_body(x_ref, g_ref, o_ref):
    x = x_ref[...]
    o_ref[...] = x * lax.rsqrt(jnp.mean(x * x, axis=-1, keepdims=True) + RMS_EPS) * g_ref[...]


def rmsnorm(x, g):
    m, d = x.shape
    tm = _row_tile(m)
    return pl.pallas_call(
        _rmsnorm_body,
        grid=(m // tm,),
        in_specs=[pl.BlockSpec((tm, d), lambda i: (i, 0)), pl.BlockSpec((1, d), lambda i: (0, 0))],
        out_specs=pl.BlockSpec((tm, d), lambda i: (i, 0)),
        out_shape=jax.ShapeDtypeStruct((m, d), F32),
        compiler_params=_cparams(("parallel",)),
        name="rmsnorm",
    )(x, g.reshape(1, d))


_XIN_PAD = 8


def _rglru_body(xr_ref, gr_ref, cb_ref, h0_ref, cw_ref, cbias_ref, wa_ref, ba_ref, wi_ref, bi_ref,
                lam_ref, y_ref, tail_ref, hl_ref, xin_s, a_s, b_s, h_s):
    tt = xr_ref.shape[1]
    keep = CONV_W - 1

    @pl.when(pl.program_id(1) == 0)
    def _():
        xin_s[_XIN_PAD - keep:_XIN_PAD, :] = cb_ref[0]
        h_s[...] = h0_ref[0]

    xin_s[_XIN_PAD:_XIN_PAD + tt, :] = xr_ref[0]
    xc = cbias_ref[...]
    for k in range(CONV_W):
        lo = _XIN_PAD - keep + k
        xc = xc + xin_s[lo:lo + tt, :] * cw_ref[k:k + 1, :]
    new_tail = xin_s[_XIN_PAD + tt - keep:_XIN_PAD + tt, :]
    xin_s[_XIN_PAD - keep:_XIN_PAD, :] = new_tail
    tail_ref[0] = new_tail

    xcb = xc.astype(BF16)
    r = jax.nn.sigmoid(jnp.dot(xcb, wa_ref[...], preferred_element_type=F32) + ba_ref[...])
    gi = jax.nn.sigmoid(jnp.dot(xcb, wi_ref[...], preferred_element_type=F32) + bi_ref[...])
    nl = -lam_ref[...]
    softplus = jnp.maximum(nl, 0.0) + jnp.log1p(jnp.exp(-jnp.abs(nl)))
    log_a = (-RG_C) * r * softplus
    a = jnp.exp(log_a)
    a_s[...] = a
    b_s[...] = jnp.sqrt(1.0 - a * a) * (gi * xc)

    def step(t, h):
        h = a_s[pl.ds(t, 1), :] * h + b_s[pl.ds(t, 1), :]
        b_s[pl.ds(t, 1), :] = h
        return h

    h = lax.fori_loop(0, tt, step, h_s[...], unroll=min(tt, 8))
    h_s[...] = h
    hl_ref[0] = h
    y_ref[0] = b_s[...] * jax.nn.gelu(gr_ref[0])


def rglru(xr, gr, conv_buf, h0, conv_w, conv_b, wa_bd, b_a, wi_bd, b_i, lam):
    n, t, w = xr.shape
    tt = min(t, 512)
    assert t % tt == 0 and tt >= CONV_W - 1
    row = lambda v: v.reshape(1, w)
    full = lambda shp: pl.BlockSpec(shp, lambda b, s: (0,) * len(shp))
    y, tail, hl = pl.pallas_call(
        _rglru_body,
        grid=(n, t // tt),
        in_specs=[pl.BlockSpec((1, tt, w), lambda b, s: (b, s, 0)),
                  pl.BlockSpec((1, tt, w), lambda b, s: (b, s, 0)),
                  pl.BlockSpec((1, CONV_W - 1, w), lambda b, s: (b, 0, 0)),
                  pl.BlockSpec((1, 1, w), lambda b, s: (b, 0, 0)),
                  full((CONV_W, w)), full((1, w)), full((w, w)), full((1, w)), full((w, w)),
                  full((1, w)), full((1, w))],
        out_specs=[pl.BlockSpec((1, tt, w), lambda b, s: (b, s, 0)),
                   pl.BlockSpec((1, CONV_W - 1, w), lambda b, s: (b, 0, 0)),
                   pl.BlockSpec((1, 1, w), lambda b, s: (b, 0, 0))],
        out_shape=[jax.ShapeDtypeStruct((n, t, w), F32),
                   jax.ShapeDtypeStruct((n, CONV_W - 1, w), F32),
                   jax.ShapeDtypeStruct((n, 1, w), F32)],
        scratch_shapes=[pltpu.VMEM((_XIN_PAD + tt, w), F32), pltpu.VMEM((tt, w), F32),
                        pltpu.VMEM((tt, w), F32), pltpu.VMEM((1, w), F32)],
        compiler_params=_cparams(("parallel", "arbitrary")),
        name="rglru",
    )(xr, gr, conv_buf, h0.reshape(n, 1, w), conv_w, row(conv_b), wa_bd, row(b_a), wi_bd, row(b_i), row(lam))
    return y, tail, hl.reshape(n, w)


def _block_diag(w):
    h, d, _ = w.shape
    eye = jnp.eye(h, dtype=w.dtype)
    return (eye[:, None, :, None] * w[:, :, None, :]).reshape(h * d, h * d)


def _block_mean_body(blk, k_ref, o_ref):
    x = k_ref[0]
    nb = x.shape[0] // blk
    o_ref[0] = jnp.mean(x.reshape(nb, blk, x.shape[1]), axis=1)


def block_mean(k, blk):
    n, t, w = k.shape
    nb = t // blk
    g = SUBLANES if nb % SUBLANES == 0 else nb
    return pl.pallas_call(
        functools.partial(_block_mean_body, blk),
        grid=(n, nb // g),
        in_specs=[pl.BlockSpec((1, g * blk, w), lambda b, s: (b, s, 0))],
        out_specs=pl.BlockSpec((1, g, w), lambda b, s: (b, s, 0)),
        out_shape=jax.ShapeDtypeStruct((n, nb, w), F32),
        compiler_params=_cparams(("parallel", "parallel")),
        name="block_mean",
    )(k)


def _first_argmax(g, idx_f):
    m = jnp.max(g, axis=-1, keepdims=True)
    return jnp.min(jnp.where(g == m, idx_f, jnp.inf), axis=-1, keepdims=True)


def _split_hi_lo(x):
    hi = x.astype(BF16)
    lo = (x - hi.astype(F32)).astype(BF16)
    return hi, lo


LOG2E = 1.4426950408889634
_QSCALE = HEAD_DIM ** -0.5 * LOG2E
_MOBA_PAIRS_PER_STEP = 4


def _moba_prompt_body(nbk, slopes_ref, q_ref, km_ref, kb_ref, vt_ref, o_ref):
    pg = pl.program_id(1)
    i = pl.program_id(2)
    tq = q_ref.shape[1]
    pairs = q_ref.shape[2] // LANES
    qlane = lax.broadcasted_iota(jnp.int32, (tq, LANES), 1)
    blk = lax.broadcasted_iota(jnp.int32, (LANES, tq), 0)
    blk_f = blk.astype(F32)
    qts = []
    for pp in range(pairs):
        q = q_ref[0, :, pp * LANES:(pp + 1) * LANES]
        km = km_ref[0, :, pp * LANES:(pp + 1) * LANES]
        q_cols = []
        for hh in range(2):
            in_head = (qlane >= hh * HEAD_DIM) & (qlane < (hh + 1) * HEAD_DIM)
            qh = jnp.where(in_head, q, 0.0)
            gate = _dot_nt(km, qh, HIGHEST)
            g = jnp.where(blk < i, gate, -jnp.inf)
            sel = blk == i
            for _ in range(MOBA_TOPK):
                m = jnp.max(g, axis=0, keepdims=True)
                first = jnp.min(jnp.where(g == m, blk_f, jnp.inf), axis=0, keepdims=True)
                pick = blk_f == first
                sel = sel | (pick & (blk < i))
                g = jnp.where(pick, -jnp.inf, g)
            slope = slopes_ref[2 * (pg * pairs + pp) + hh]
            pb = jnp.where(sel, 0.0, NEG_BIG) + slope * (blk_f * MOBA_BLOCK)
            pb_hi, pb_lo = _split_hi_lo(pb)
            pb_lo = pltpu.roll(pb_lo.astype(F32), nbk, 0)
            s_hi, s_lo = _split_hi_lo(jnp.full((LANES, tq), slope, F32))
            aug = jnp.where(blk < nbk, pb_hi.astype(F32),
                            jnp.where(blk < 2 * nbk, pb_lo,
                                      jnp.where(blk == 2 * nbk, s_hi.astype(F32),
                                                jnp.where(blk == 2 * nbk + 1, s_lo.astype(F32), 0.0))))
            q_cols.append(jnp.concatenate([(qh * _QSCALE).T.astype(BF16), aug.astype(BF16)], axis=0))
        qts.append(jnp.concatenate(q_cols, axis=1))

    klane = lax.broadcasted_iota(jnp.int32, (MOBA_BLOCK, LANES), 1)
    krow = lax.broadcasted_iota(jnp.int32, (MOBA_BLOCK, LANES), 0).astype(F32)
    vrow = lax.broadcasted_iota(jnp.int32, (LANES, MOBA_BLOCK), 0)
    ones_row = (HEAD_DIM, 0)
    causal = (lax.broadcasted_iota(jnp.int32, (MOBA_BLOCK, tq), 0)
              <= lax.broadcasted_iota(jnp.int32, (MOBA_BLOCK, tq), 1))

    def scores(j):
        off = pl.multiple_of(j * MOBA_BLOCK, MOBA_BLOCK)
        kaug = jnp.where((klane == j) | (klane == nbk + j), 1.0,
                         jnp.where((klane == 2 * nbk) | (klane == 2 * nbk + 1), krow, 0.0)).astype(BF16)
        return [jnp.dot(jnp.concatenate([kb_ref[0, pl.ds(off, MOBA_BLOCK), pp * LANES:(pp + 1) * LANES], kaug],
                                        axis=1), qts[pp], preferred_element_type=F32)
                for pp in range(pairs)]

    def update(j, s, carry, diag):
        off = pl.multiple_of(j * MOBA_BLOCK, MOBA_BLOCK)
        new = []
        for pp in range(pairs):
            vt = vt_ref[0, pp * LANES:(pp + 1) * LANES, pl.ds(off, MOBA_BLOCK)]
            for hh in range(2):
                m, acc = carry[2 * pp + hh]
                sh = s[pp][:, hh * tq:(hh + 1) * tq]
                if diag:
                    sh = jnp.where(causal, sh, NEG_BIG)
                m_new = jnp.maximum(m, jnp.max(sh, axis=0, keepdims=True))
                p = jnp.exp2(sh - m_new).astype(BF16)
                in_head = (vrow >= hh * HEAD_DIM) & (vrow < (hh + 1) * HEAD_DIM)
                v_h = jnp.where(in_head, vt, jnp.where(vrow == ones_row[hh], 1.0, 0.0).astype(BF16))
                new.append((m_new, jnp.exp2(m - m_new) * acc + jnp.dot(v_h, p, preferred_element_type=F32)))
        return new

    def body(j, c):
        s, carry = c
        return scores(j + 1), update(j, s, carry, False)

    init = [(jnp.full((1, tq), -jnp.inf, F32), jnp.zeros((LANES, tq), F32)) for _ in range(2 * pairs)]
    s, carry = lax.fori_loop(0, i, body, (scores(0), init))
    carry = update(i, s, carry, True)
    orow = lax.broadcasted_iota(jnp.int32, (LANES, tq), 0)
    for pp in range(pairs):
        outs = [acc / acc[ones_row[hh]:ones_row[hh] + 1, :] for hh, (_, acc) in enumerate(carry[2 * pp:2 * pp + 2])]
        o_ref[0, :, pp * LANES:(pp + 1) * LANES] = jnp.where(orow < HEAD_DIM, outs[0], outs[1]).T


def moba_prompt(q, kb, vt, kmean, slopes):
    n, t, w = q.shape
    nbk = t // MOBA_BLOCK
    pw = _MOBA_PAIRS_PER_STEP * LANES
    assert t % MOBA_BLOCK == 0 and nbk >= MOBA_TOPK and 2 * nbk + 2 <= LANES and nbk % SUBLANES == 0 and w % pw == 0
    km = jnp.pad(kmean, ((0, 0), (0, LANES - nbk), (0, 0)))
    grid_spec = pltpu.PrefetchScalarGridSpec(
        num_scalar_prefetch=0,
        grid=(n, w // pw, nbk),
        in_specs=[pl.BlockSpec(memory_space=pltpu.SMEM),
                  pl.BlockSpec((1, MOBA_BLOCK, pw), lambda b, p, i: (b, i, p)),
                  pl.BlockSpec((1, LANES, pw), lambda b, p, i: (b, 0, p)),
                  pl.BlockSpec((1, t, pw), lambda b, p, i: (b, 0, p)),
                  pl.BlockSpec((1, pw, t), lambda b, p, i: (b, p, 0))],
        out_specs=pl.BlockSpec((1, MOBA_BLOCK, pw), lambda b, p, i: (b, i, p)),
    )
    return pl.pallas_call(
        functools.partial(_moba_prompt_body, nbk),
        grid_spec=grid_spec,
        out_shape=jax.ShapeDtypeStruct((n, t, w), F32),
        compiler_params=_cparams(("parallel", "parallel", "arbitrary")),
        name="moba_prompt",
    )(jnp.asarray(slopes * np.float32(LOG2E)), q, km, kb, vt)


_ROUTE_E, _ROUTE_W, _ROUTE_RANK = 0, 2, 4


def _router_body(x_ref, g_ref, wr_ref, br_ref, xn_ref, route_ref, cnt_ref, carry_s):
    tm = x_ref.shape[0]

    @pl.when(pl.program_id(0) == 0)
    def _():
        carry_s[...] = jnp.zeros_like(carry_s)

    x = x_ref[...]
    xn = x * lax.rsqrt(jnp.mean(x * x, axis=-1, keepdims=True) + RMS_EPS) * g_ref[...]
    xn_ref[...] = xn
    logits = jnp.dot(xn, wr_ref[...], precision=HIGHEST, preferred_element_type=F32) + br_ref[...]
    lane = lax.broadcasted_iota(jnp.int32, (tm, LANES), 1)
    is_g = lane < N_GROUPS
    gl = jnp.where(is_g, logits, -jnp.inf)
    gmax = jnp.max(gl, axis=-1, keepdims=True)
    g_idx = jnp.min(jnp.where(gl == gmax, lane, LANES), axis=-1, keepdims=True)
    g_w = 1.0 / jnp.sum(jnp.where(is_g, jnp.exp(gl - gmax), 0.0), axis=-1, keepdims=True)
    e_lo = N_GROUPS + g_idx * EXPERTS_PER_GROUP
    el = jnp.where((lane >= e_lo) & (lane < e_lo + EXPERTS_PER_GROUP), logits, -jnp.inf)
    v1 = jnp.max(el, axis=-1, keepdims=True)
    i1 = jnp.min(jnp.where(el == v1, lane, LANES), axis=-1, keepdims=True)
    el2 = jnp.where(lane == i1, -jnp.inf, el)
    v2 = jnp.max(el2, axis=-1, keepdims=True)
    i2 = jnp.min(jnp.where(el2 == v2, lane, LANES), axis=-1, keepdims=True)
    e21 = jnp.exp(v2 - v1)
    p1 = 1.0 / (1.0 + e21)
    w1 = g_w * p1
    w2 = g_w * (e21 * p1)
    hot1 = lane == i1
    hot2 = lane == i2
    cnt = jnp.where(hot1 | hot2, 1.0, 0.0)
    r = lax.broadcasted_iota(jnp.int32, (tm, tm), 0)
    c = lax.broadcasted_iota(jnp.int32, (tm, tm), 1)
    tri = jnp.where(c < r, 1.0, 0.0).astype(BF16)
    before = jnp.dot(tri, cnt.astype(BF16), preferred_element_type=F32) + carry_s[...]
    r1 = jnp.sum(jnp.where(hot1, before, 0.0), axis=-1, keepdims=True)
    r2 = jnp.sum(jnp.where(hot2, before, 0.0), axis=-1, keepdims=True)
    carry_s[...] = carry_s[...] + jnp.sum(cnt, axis=0, keepdims=True)
    cnt_ref[...] = carry_s[...]
    route = jnp.zeros((tm, LANES), F32)
    for pos, val in ((_ROUTE_E, (i1 - N_GROUPS).astype(F32)), (_ROUTE_E + 1, (i2 - N_GROUPS).astype(F32)),
                     (_ROUTE_W, w1), (_ROUTE_W + 1, w2), (_ROUTE_RANK, r1), (_ROUTE_RANK + 1, r2)):
        route = jnp.where(lane == pos, val, route)
    route_ref[...] = route


def moe_router(y, g, wr, br):
    m, d = y.shape
    tm = _row_tile(m, 256)
    return pl.pallas_call(
        _router_body,
        grid=(m // tm,),
        in_specs=[pl.BlockSpec((tm, d), lambda i: (i, 0)), pl.BlockSpec((1, d), lambda i: (0, 0)),
                  pl.BlockSpec((d, LANES), lambda i: (0, 0)), pl.BlockSpec((1, LANES), lambda i: (0, 0))],
        out_specs=[pl.BlockSpec((tm, d), lambda i: (i, 0)), pl.BlockSpec((tm, LANES), lambda i: (i, 0)),
                   pl.BlockSpec((1, LANES), lambda i: (0, 0))],
        out_shape=[jax.ShapeDtypeStruct((m, d), F32), jax.ShapeDtypeStruct((m, LANES), F32),
                   jax.ShapeDtypeStruct((1, LANES), F32)],
        scratch_shapes=[pltpu.VMEM((1, LANES), F32)],
        compiler_params=_cparams(("arbitrary",)),
        name="moe_router",
    )(y, g.reshape(1, d), wr, br)


def _row_copy(src, s_row, dst, d_row, sem):
    return pltpu.make_async_copy(src.at[pl.ds(s_row, 1), :], dst.at[pl.ds(d_row, 1), :], sem)


def _scatter_body(dest_ref, xn_ref, xs_in_ref, xs_ref, sem):
    del xs_in_ref
    tm = xn_ref.shape[0]
    base = pl.program_id(0) * tm

    def issue(r, _):
        for s in range(2):
            _row_copy(xn_ref, r, xs_ref, dest_ref[2 * (base + r) + s], sem).start()
        return 0

    lax.fori_loop(0, tm, issue, 0, unroll=4)
    for _ in range(2):
        pltpu.make_async_copy(xn_ref, xs_ref.at[pl.ds(0, tm), :], sem).wait()


def moe_scatter(dest, xn, rows):
    m, d = xn.shape
    tm = _row_tile(m, 256)
    grid_spec = pltpu.PrefetchScalarGridSpec(
        num_scalar_prefetch=1,
        grid=(m // tm,),
        in_specs=[pl.BlockSpec((tm, d), lambda i, dest: (i, 0)), pl.BlockSpec(memory_space=pl.ANY)],
        out_specs=pl.BlockSpec(memory_space=pl.ANY),
        scratch_shapes=[pltpu.SemaphoreType.DMA(())],
    )
    return pl.pallas_call(
        _scatter_body,
        grid_spec=grid_spec,
        out_shape=jax.ShapeDtypeStruct((rows, d), F32),
        input_output_aliases={2: 0},
        compiler_params=_cparams(("arbitrary",)),
        name="moe_scatter",
    )(dest, xn, jnp.zeros((rows, d), F32))


def _expert_body(blk_e_ref, nused_ref, xs_ref, w1_ref, w3_ref, w2_ref, ys_ref):
    del blk_e_ref
    used = pl.program_id(0) < nused_ref[0]

    @pl.when(used)
    def _():
        x = xs_ref[...].astype(BF16)
        h = jax.nn.silu(jnp.dot(x, w1_ref[0], preferred_element_type=F32))
        h = h * jnp.dot(x, w3_ref[0], preferred_element_type=F32)
        ys_ref[...] = jnp.dot(h.astype(BF16), w2_ref[0], preferred_element_type=F32)

    @pl.when(jnp.logical_not(used))
    def _():
        ys_ref[...] = jnp.zeros_like(ys_ref)


def moe_experts(blk_e, nused, xs, w1, w3, w2):
    rows, d = xs.shape
    te = MOE_BLOCK_ROWS
    de = w1.shape[2]
    grid_spec = pltpu.PrefetchScalarGridSpec(
        num_scalar_prefetch=2,
        grid=(rows // te,),
        in_specs=[pl.BlockSpec((te, d), lambda i, be, nu: (i, 0)),
                  pl.BlockSpec((1, d, de), lambda i, be, nu: (be[i], 0, 0)),
                  pl.BlockSpec((1, d, de), lambda i, be, nu: (be[i], 0, 0)),
                  pl.BlockSpec((1, de, d), lambda i, be, nu: (be[i], 0, 0))],
        out_specs=pl.BlockSpec((te, d), lambda i, be, nu: (i, 0)),
    )
    return pl.pallas_call(
        _expert_body,
        grid_spec=grid_spec,
        out_shape=jax.ShapeDtypeStruct((rows, d), F32),
        compiler_params=_cparams(("arbitrary",)),
        name="moe_experts",
    )(blk_e, nused, xs, w1, w3, w2)


def _combine_body(dest_ref, route_ref, res_ref, ys_ref, o_ref, buf, sem):
    tm = res_ref.shape[0]
    base = pl.program_id(0) * tm

    def issue(r, _):
        for s in range(2):
            _row_copy(ys_ref, dest_ref[2 * (base + r) + s], buf.at[s], r, sem).start()
        return 0

    lax.fori_loop(0, tm, issue, 0, unroll=4)
    for s in range(2):
        pltpu.make_async_copy(ys_ref.at[pl.ds(0, tm), :], buf.at[s], sem).wait()
    route = route_ref[...]
    w1 = route[:, _ROUTE_W:_ROUTE_W + 1]
    w2 = route[:, _ROUTE_W + 1:_ROUTE_W + 2]
    o_ref[...] = res_ref[...] + (w1 * buf[0] + w2 * buf[1])


def moe_combine(dest, route, res, ys):
    m, d = res.shape
    tm = _row_tile(m, 256)
    grid_spec = pltpu.PrefetchScalarGridSpec(
        num_scalar_prefetch=1,
        grid=(m // tm,),
        in_specs=[pl.BlockSpec((tm, LANES), lambda i, dest: (i, 0)),
                  pl.BlockSpec((tm, d), lambda i, dest: (i, 0)),
                  pl.BlockSpec(memory_space=pl.ANY)],
        out_specs=pl.BlockSpec((tm, d), lambda i, dest: (i, 0)),
        scratch_shapes=[pltpu.VMEM((2, tm, d), F32), pltpu.SemaphoreType.DMA(())],
    )
    return pl.pallas_call(
        _combine_body,
        grid_spec=grid_spec,
        out_shape=jax.ShapeDtypeStruct((m, d), F32),
        compiler_params=_cparams(("arbitrary",)),
        name="moe_combine",
    )(dest, route, res, ys)


def hier_moe_residual(y, g, wr, br, w1, w3, w2):
    m, d = y.shape
    te = MOE_BLOCK_ROWS
    xn, route, cnt = moe_router(y, g, wr, br)
    counts = cnt[0, N_GROUPS:N_GROUPS + N_EXPERTS].astype(jnp.int32)
    padded = (counts + te - 1) // te * te
    pend = jnp.cumsum(padded)
    pstart = pend - padded
    e = route[:, _ROUTE_E:_ROUTE_E + 2].astype(jnp.int32)
    rank = route[:, _ROUTE_RANK:_ROUTE_RANK + 2].astype(jnp.int32)
    dest = (pstart[e] + rank).reshape(-1)
    n_blk = -(-(2 * m + N_EXPERTS * (te - 1)) // te)
    blk_start = jnp.arange(n_blk, dtype=jnp.int32) * te
    blk_e = jnp.minimum(jnp.sum((pend[None, :] <= blk_start[:, None]).astype(jnp.int32), axis=1), N_EXPERTS - 1)
    nused = (pend[-1:] // te).astype(jnp.int32)
    xs = moe_scatter(dest, xn, n_blk * te)
    ys = moe_experts(blk_e, nused, xs, w1, w3, w2)
    return moe_combine(dest, route, y, ys)


def _router_weights(w_rg, b_rg, w_re, b_re):
    d = w_rg.shape[0]
    pad = LANES - N_GROUPS - N_EXPERTS
    wr = jnp.concatenate([w_rg, w_re, jnp.zeros((d, pad), F32)], axis=1)
    br = jnp.concatenate([b_rg, b_re, jnp.zeros((pad,), F32)]).reshape(1, LANES)
    return wr, br


_POS_LANE = HEAD_DIM
_TILE_BIAS_LANE = _POS_LANE + 4


def _compress_tail(ab, bias, w2_ref, add_pos):
    c = ab.shape[0] - SUBLANES
    gw = ab.shape[1] // 2
    h = ab[0:c, 0:gw] + ab[1:c + 1, gw:2 * gw] + bias
    out = jnp.dot(jax.nn.gelu(h).astype(BF16), w2_ref[...], preferred_element_type=F32)
    if add_pos:
        lane = lax.broadcasted_iota(jnp.int32, out.shape, 1) % LANES
        cend = lax.broadcasted_iota(jnp.int32, out.shape, 0) * CMP_STRIDE + (CMP_LEN - 1)
        hi = ((cend // 256) * 256).astype(F32)
        lo = (cend % 256).astype(F32)
        out = jnp.where((lane == _POS_LANE) | (lane == _POS_LANE + 1), hi,
                        jnp.where((lane == _POS_LANE + 2) | (lane == _POS_LANE + 3), lo, out))
    return out


def _compress_body(add_pos, n_k, *refs):
    k_refs = refs[:n_k]
    wab_ref, pe_ref, w2_ref, o_ref, ab_s = refs[n_k:]
    c = k_refs[0].shape[1] // CMP_STRIDE
    gw = n_k * LANES
    acc = jnp.zeros((c + SUBLANES, 2 * gw), F32)
    for j in range(CMP_STRIDE):
        xk = jnp.concatenate([r[0, pl.ds(j, c, stride=CMP_STRIDE), :] for r in k_refs], axis=1)
        xj = jnp.concatenate([xk, pe_ref[j:j + 1, :], pe_ref[CMP_STRIDE + j:CMP_STRIDE + j + 1, :],
                              jnp.zeros((SUBLANES - 2, gw), F32)], axis=0).astype(BF16)
        acc = acc + jnp.dot(xj, wab_ref[j], preferred_element_type=F32)
    bias = acc[c:c + 1, 0:gw] + acc[c + 1:c + 2, gw:2 * gw]
    ab_s[0:c, :] = acc[0:c]
    ab_s[c:c + SUBLANES, :] = jnp.zeros((SUBLANES, 2 * gw), F32)
    out = _compress_tail(ab_s, bias, w2_ref, add_pos)
    o_ref[0] = (out if add_pos else out.T).astype(o_ref.dtype)


def _compress_weights(w1, w2, pe, groups):
    d = w2.shape[0]
    w1j = w1.reshape(CMP_LEN, d, d)
    eye = jnp.eye(groups, dtype=F32)
    bd = (eye[None, :, None, :, None] * w1j[:, None, :, None, :]).reshape(CMP_LEN, groups * d, groups * d)
    wab = jnp.concatenate([bd[:CMP_STRIDE], bd[CMP_STRIDE:]], axis=2).astype(BF16)
    pe_t = jnp.tile(pe, (1, groups))
    w2p = jnp.concatenate([w2, jnp.zeros((d, LANES - d), F32)], axis=1)
    w2bd = (eye[:, None, :, None] * w2p[None, :, None, :]).reshape(groups * d, groups * LANES).astype(BF16)
    return wab, pe_t, w2bd


def nsa_compress(k, wab, pe_t, w2bd, add_pos):
    n, t, gw = k.shape
    c = t // CMP_STRIDE
    n_k = gw // LANES
    full = lambda shp: pl.BlockSpec(shp, lambda b: (0,) * len(shp))
    out_dims = (c, w2bd.shape[1]) if add_pos else (w2bd.shape[1], c)
    return pl.pallas_call(
        functools.partial(_compress_body, add_pos, n_k),
        grid=(n,),
        in_specs=[pl.BlockSpec((1, t, LANES), functools.partial(lambda u, b: (b, 0, u), u)) for u in range(n_k)]
        + [full(wab.shape), full(pe_t.shape), full(w2bd.shape)],
        out_specs=pl.BlockSpec((1,) + out_dims, lambda b: (b, 0, 0)),
        out_shape=jax.ShapeDtypeStruct((n,) + out_dims, BF16),
        scratch_shapes=[pltpu.VMEM((c + SUBLANES, 2 * gw), F32)],
        compiler_params=_cparams(("parallel",)),
        name="nsa_compress",
    )(*([k] * n_k), wab, pe_t, w2bd)


def _select_blocks_t(imp, qb, blk):
    vis = blk <= qb
    forced = (blk == 0) | (blk == qb) | (blk == qb - 1)
    score = jnp.where(vis & jnp.logical_not(forced), imp, -jnp.inf)
    blk_f = blk.astype(F32)
    sel = forced
    for _ in range(SEL_TOPK - 3):
        m = jnp.max(score, axis=0, keepdims=True)
        first = jnp.min(jnp.where(score == m, blk_f, jnp.inf), axis=0, keepdims=True)
        pick = blk_f == first
        sel = sel | pick
        score = jnp.where(pick, -jnp.inf, score)
    return sel & vis


def _select_blocks(imp, qb, lane):
    vis = lane <= qb
    forced = (lane == 0) | (lane == qb) | (lane == qb - 1)
    score = jnp.where(vis & jnp.logical_not(forced), imp, -jnp.inf)
    lane_f = lane.astype(F32)
    sel = forced
    for _ in range(SEL_TOPK - 3):
        pick = lane_f == _first_argmax(score, lane_f)
        sel = sel | pick
        score = jnp.where(pick, -jnp.inf, score)
    return sel & vis


def _nsa_prompt_body(slopes_ref, q_ref, gt_ref, ck_ref, cvt_ref, covert_ref, ks_ref, vst_ref, kw_ref, vwt_ref,
                     o_ref):
    g = pl.program_id(1)
    i = pl.program_id(2)
    tq = q_ref.shape[1]
    r_heads = q_ref.shape[2] // HEAD_DIM
    cols = r_heads * tq
    t0 = i * tq
    frow = lax.broadcasted_iota(jnp.int32, (LANES, tq), 0)
    qt_all = q_ref[0].T
    q_cols = []
    for r in range(r_heads):
        slope = slopes_ref[r_heads * g + r]
        s_hi, s_lo = _split_hi_lo(jnp.full((LANES, tq), slope, F32))
        feat = jnp.where(((frow - _POS_LANE) % 2) == 1, s_lo.astype(F32), s_hi.astype(F32))
        qr = jnp.concatenate([qt_all[r * HEAD_DIM:(r + 1) * HEAD_DIM] * _QSCALE,
                              jnp.zeros((LANES - HEAD_DIM, tq), F32)], axis=0)
        q_cols.append(jnp.where(frow < HEAD_DIM, qr,
                                jnp.where(frow < _POS_LANE + 4, feat,
                                          jnp.where(frow == _TILE_BIAS_LANE, 1.0, 0.0))).astype(BF16))
    qt = jnp.concatenate(q_cols, axis=1)
    qoff = lax.broadcasted_iota(jnp.int32, (1, cols), 1) % tq

    ck = ck_ref[0]
    n_cmp = ck.shape[0]
    s_c = jnp.dot(ck, qt, preferred_element_type=F32)
    cend = lax.broadcasted_iota(jnp.int32, (n_cmp, cols), 0) * CMP_STRIDE + (CMP_LEN - 1)
    ok_c = cend <= t0 + qoff
    s_c = jnp.where(ok_c, s_c, NEG_BIG)
    e_c = jnp.where(ok_c, jnp.exp2(s_c - jnp.max(s_c, axis=0, keepdims=True)), 0.0)
    p_c = e_c / jnp.maximum(jnp.sum(e_c, axis=0, keepdims=True), 1e-30)
    o_c = jnp.dot(cvt_ref[0], p_c.astype(BF16), preferred_element_type=F32)

    psum = p_c[:, 0:tq]
    for r in range(1, r_heads):
        psum = psum + p_c[:, r * tq:(r + 1) * tq]
    imp = jnp.dot(covert_ref[...], psum, precision=HIGHEST, preferred_element_type=F32)
    qb = (t0 + lax.broadcasted_iota(jnp.int32, (LANES, tq), 1)) // SEL_BLOCK
    pen = jnp.where(_select_blocks_t(imp, qb, frow), 0.0, NEG_BIG).astype(BF16)
    qs = jnp.concatenate([qt, jnp.concatenate([pen] * r_heads, axis=1)], axis=0)

    tk = tq
    klane = lax.broadcasted_iota(jnp.int32, (tk, LANES), 1)
    krow = lax.broadcasted_iota(jnp.int32, (tk, LANES), 0)
    krow_b = krow.astype(F32).astype(BF16)
    is_pos = (klane == _POS_LANE) | (klane == _POS_LANE + 1)
    is_off = (klane == _POS_LANE + 2) | (klane == _POS_LANE + 3)
    blocks_per_tile = tk // SEL_BLOCK
    vrow = lax.broadcasted_iota(jnp.int32, (LANES, tk), 0)
    one_b = jnp.ones((LANES, tk), BF16)
    key_i = lax.broadcasted_iota(jnp.int32, (tk, cols), 0)

    def keys_with_pos(k_ref, j):
        kt = k_ref[0, pl.ds(pl.multiple_of(j * tk, tk), tk), :]
        off_b = jnp.full((tk, LANES), ((j - i) * tk).astype(F32), F32).astype(BF16)
        return jnp.where(is_pos, krow_b, jnp.where(is_off, off_b, kt))

    def values_with_ones(vt_ref, j):
        return jnp.where(vrow == _POS_LANE, one_b, vt_ref[0, :, pl.ds(pl.multiple_of(j * tk, tk), tk)])

    def sel_scores(j):
        onehot = jnp.where(klane == j * blocks_per_tile + krow // SEL_BLOCK, 1.0, 0.0).astype(BF16)
        return jnp.dot(jnp.concatenate([keys_with_pos(ks_ref, j), onehot], axis=1), qs,
                       preferred_element_type=F32)

    def update(s, v, carry, mask=None):
        m, acc = carry
        if mask is not None:
            s = jnp.where(mask, s, NEG_BIG)
        m_new = jnp.maximum(m, jnp.max(s, axis=0, keepdims=True))
        p = jnp.exp2(s - m_new).astype(BF16)
        return m_new, jnp.exp2(m - m_new) * acc + jnp.dot(v, p, preferred_element_type=F32)

    def sel_body(j, c):
        s, carry = c
        return sel_scores(j + 1), update(s, values_with_ones(vst_ref, j), carry)

    def finish(acc):
        return acc / acc[_POS_LANE:_POS_LANE + 1, :]

    init = (jnp.full((1, cols), -jnp.inf, F32), jnp.zeros((LANES, cols), F32))
    s, carry = lax.fori_loop(0, i, sel_body, (sel_scores(0), init))
    o_s = finish(update(s, values_with_ones(vst_ref, i), carry, key_i <= qoff)[1])

    n_win = WINDOW // tk + 1
    kws, vws = [], []
    for w in range(n_win):
        j_true = i - (n_win - 1) + w
        j = jnp.maximum(j_true, 0)
        tile_bias = jnp.full((tk, LANES), jnp.where(j_true >= 0, 0.0, NEG_BIG), F32).astype(BF16)
        kws.append(jnp.where(klane == _TILE_BIAS_LANE, tile_bias, keys_with_pos(kw_ref, j)))
        vws.append(values_with_ones(vwt_ref, j))
    s_w = jnp.dot(jnp.concatenate(kws, axis=0), qt, preferred_element_type=F32)
    newest = (n_win - 1) * tk
    s_w = jnp.concatenate([jnp.where(key_i > qoff, s_w[0:tk], NEG_BIG), s_w[tk:newest],
                           jnp.where(key_i <= qoff, s_w[newest:], NEG_BIG)], axis=0)
    p_w = jnp.exp2(s_w - jnp.max(s_w, axis=0, keepdims=True)).astype(BF16)
    o_w = finish(jnp.dot(jnp.concatenate(vws, axis=1), p_w, preferred_element_type=F32))

    gates = jax.nn.sigmoid(gt_ref[0]).T
    outs = []
    for r in range(r_heads):
        cs = slice(r * tq, (r + 1) * tq)
        gr = [gates[3 * r + b:3 * r + b + 1, :] for b in range(3)]
        outs.append((gr[0] * o_c[:, cs] + gr[1] * o_s[:, cs] + gr[2] * o_w[:, cs])[0:HEAD_DIM])
    o_ref[0] = jnp.concatenate(outs, axis=0).T


def nsa_prompt(q, gt, ck, cvt, cover_t, ks, vst, kw, vwt, slopes):
    n, t, w = q.shape
    groups = ks.shape[2] // LANES
    qw = w // groups
    tq = 256
    c = ck.shape[1]
    assert t % tq == 0 and -(-t // SEL_BLOCK) <= LANES and t // SEL_BLOCK >= SEL_TOPK and WINDOW % tq == 0
    rows_spec = lambda rows: pl.BlockSpec((1, rows, LANES), lambda b, g, i: (b, 0, g))
    cols_spec = lambda cols: pl.BlockSpec((1, LANES, cols), lambda b, g, i: (b, g, 0))
    grid_spec = pltpu.PrefetchScalarGridSpec(
        num_scalar_prefetch=0,
        grid=(n, groups, t // tq),
        in_specs=[pl.BlockSpec(memory_space=pltpu.SMEM),
                  pl.BlockSpec((1, tq, qw), lambda b, g, i: (b, i, g)),
                  pl.BlockSpec((1, tq, LANES), lambda b, g, i: (b, i, g)),
                  rows_spec(c), cols_spec(c),
                  pl.BlockSpec(cover_t.shape, lambda b, g, i: (0, 0)),
                  rows_spec(t), cols_spec(t), rows_spec(t), cols_spec(t)],
        out_specs=pl.BlockSpec((1, tq, qw), lambda b, g, i: (b, i, g)),
    )
    return pl.pallas_call(
        _nsa_prompt_body,
        grid_spec=grid_spec,
        out_shape=jax.ShapeDtypeStruct((n, t, w), F32),
        compiler_params=_cparams(("parallel", "parallel", "arbitrary")),
        name="nsa_prompt",
    )(jnp.asarray(slopes * np.float32(LOG2E)), q, gt, ck, cvt, cover_t, ks, vst, kw, vwt)


def _nsa_cover(n_cmp_rows, n_sel):
    i = np.arange(n_cmp_rows)[:, None]
    j = np.arange(-(-n_sel // LANES) * LANES)[None, :]
    cov = (CMP_STRIDE * i < SEL_BLOCK * (j + 1)) & (CMP_STRIDE * i + CMP_LEN > SEL_BLOCK * j) & (j < n_sel)
    return jnp.asarray(cov.astype(np.float32))


def _pad_group_cols(w, width):
    d, gw = w.shape
    g = gw // width
    return jnp.pad(w.reshape(d, g, width), ((0, 0), (0, 0), (0, LANES - width))).reshape(d, g * LANES)


def _own_head_block(rows, width, nq):
    r = lax.broadcasted_iota(jnp.int32, (rows, width), 0)
    c = lax.broadcasted_iota(jnp.int32, (rows, width), 1)
    return (c // HEAD_DIM) == (r // nq)


def _fold_heads(x, own):
    width = x.shape[1]
    fold = jnp.where(lax.broadcasted_iota(jnp.int32, (width, HEAD_DIM), 0) % HEAD_DIM
                     == lax.broadcasted_iota(jnp.int32, (width, HEAD_DIM), 1), 1.0, 0.0)
    return jnp.dot(jnp.where(own, x, 0.0), fold, precision=HIGHEST, preferred_element_type=F32)


_MOBA_BLOCKS_PER_STEP = 4


def _moba_decode_body(nblk, past, pt_ref, q_ref, slope_ref, kn_ref, vn_ref, *refs):
    del pt_ref
    n_page_refs = _MOBA_BLOCKS_PER_STEP * (MOBA_BLOCK // PAGE_SIZE)
    k_refs, v_refs = refs[:n_page_refs], refs[n_page_refs:2 * n_page_refs]
    o_ref, m_s, l_s, g_s, acc_s = refs[2 * n_page_refs:]
    j = pl.program_id(1)
    rows, width = q_ref.shape[1], q_ref.shape[2]
    nq = rows // (width // HEAD_DIM)
    own = _own_head_block(rows, width, nq)
    q_hi, q_lo = _split_hi_lo(jnp.where(own, q_ref[0] * HEAD_DIM ** -0.5, 0.0))
    slope = slope_ref[...]
    lane = lax.broadcasted_iota(jnp.int32, (rows, LANES), 1)
    lane_f = lane.astype(F32)

    @pl.when(j == 0)
    def _():
        m_s[...] = jnp.full(m_s.shape, NEG_BIG, F32)
        l_s[...] = jnp.zeros(l_s.shape, F32)
        g_s[...] = jnp.zeros(g_s.shape, F32)

    lhs = jnp.concatenate([q_hi, q_lo], axis=0)
    pages_per_blk = MOBA_BLOCK // PAGE_SIZE
    span = jnp.concatenate([lane_f + pg * PAGE_SIZE for pg in range(pages_per_blk)], axis=1)
    for bk in range(_MOBA_BLOCKS_PER_STEP):
        jb = j * _MOBA_BLOCKS_PER_STEP + bk
        pages = slice(bk * pages_per_blk, (bk + 1) * pages_per_blk)
        kt = jnp.concatenate([r[0, 0].reshape(width, PAGE_SIZE) for r in k_refs[pages]], axis=1).astype(BF16)
        vt = jnp.concatenate([r[0, 0].reshape(width, PAGE_SIZE) for r in v_refs[pages]], axis=1).astype(BF16)
        s2 = jnp.dot(lhs, kt, preferred_element_type=F32)
        gate = jnp.sum(s2[:rows] + s2[rows:], axis=-1, keepdims=True)
        s = s2[:rows] + slope * (span + (jb * MOBA_BLOCK - past).astype(F32))
        m_j = jnp.max(s, axis=-1, keepdims=True)
        p = jnp.exp(s - m_j)
        m_s[...] = jnp.where(lane == jb, m_j, m_s[...])
        l_s[...] = jnp.where(lane == jb, jnp.sum(p, axis=-1, keepdims=True), l_s[...])
        g_s[...] = jnp.where(lane == jb, gate, g_s[...])
        acc_s[jb] = _dot_nt(p.astype(BF16), vt)

    @pl.when(j == nblk // _MOBA_BLOCKS_PER_STEP - 1)
    def _():
        g = jnp.where(lane < nblk, g_s[...], -jnp.inf)
        sel = jnp.zeros(g.shape, jnp.bool_)
        for _ in range(MOBA_TOPK):
            pick = lane_f == _first_argmax(g, lane_f)
            sel = sel | pick
            g = jnp.where(pick, -jnp.inf, g)
        sel = sel & (lane < nblk)
        ok_o = (lane < nq) & (lane <= lax.broadcasted_iota(jnp.int32, (rows, LANES), 0) % nq)
        s_o = jnp.dot(q_hi, kn_ref[0].astype(BF16), preferred_element_type=F32) + slope * lane_f
        s_o = jnp.where(ok_o, s_o, NEG_BIG)
        m_o = jnp.max(s_o, axis=-1, keepdims=True)
        p_o = jnp.exp(s_o - m_o)
        l_o = jnp.sum(p_o, axis=-1, keepdims=True)
        acc_o = _dot_nt(p_o.astype(BF16), vn_ref[0].astype(BF16))
        m_all = jnp.maximum(jnp.max(jnp.where(sel, m_s[...], -jnp.inf), axis=-1, keepdims=True), m_o)
        w = jnp.where(sel, jnp.exp(m_s[...] - m_all), 0.0)
        w_o = jnp.exp(m_o - m_all)
        den = jnp.sum(w * l_s[...], axis=-1, keepdims=True) + w_o * l_o
        num = w_o * acc_o
        for jj in range(nblk):
            num = num + w[:, jj:jj + 1] * acc_s[jj]
        o_ref[0] = _fold_heads(num, own) / den


def moba_decode(q_tiled, kn_t, vn_t, cache_k, cache_v, page_table, layer, slopes):
    b, rows, width = q_tiled.shape
    heads = cache_k.shape[2]
    nq = rows // heads
    n_pages = page_table.shape[1]
    past = n_pages * PAGE_SIZE
    pages_per_blk = MOBA_BLOCK // PAGE_SIZE
    nblk = past // MOBA_BLOCK
    pages_per_step = _MOBA_BLOCKS_PER_STEP * pages_per_blk
    assert past % MOBA_BLOCK == 0 and MOBA_TOPK <= nblk <= LANES and nblk % _MOBA_BLOCKS_PER_STEP == 0
    slope_rows = jnp.asarray(np.repeat(slopes, nq).reshape(rows, 1))
    page = lambda pg: pl.BlockSpec((1, 1, heads, HEAD_DIM, PAGE_SIZE),
                                   lambda s, j, pt: (layer, pt[s, pages_per_step * j + pg], 0, 0, 0))
    pages = [page(pg) for pg in range(pages_per_step)]
    per_seq = lambda a: pl.BlockSpec((1,) + a.shape[1:], lambda s, j, pt: (s, 0, 0))
    grid_spec = pltpu.PrefetchScalarGridSpec(
        num_scalar_prefetch=1,
        grid=(b, nblk // _MOBA_BLOCKS_PER_STEP),
        in_specs=[per_seq(q_tiled), pl.BlockSpec((rows, 1), lambda s, j, pt: (0, 0)), per_seq(kn_t), per_seq(vn_t)]
        + pages + pages,
        out_specs=pl.BlockSpec((1, rows, HEAD_DIM), lambda s, j, pt: (s, 0, 0)),
        scratch_shapes=[pltpu.VMEM((rows, LANES), F32), pltpu.VMEM((rows, LANES), F32),
                        pltpu.VMEM((rows, LANES), F32), pltpu.VMEM((nblk, rows, width), F32)],
    )
    return pl.pallas_call(
        functools.partial(_moba_decode_body, nblk, past),
        grid_spec=grid_spec,
        out_shape=jax.ShapeDtypeStruct((b, rows, HEAD_DIM), F32),
        compiler_params=_cparams(("parallel", "arbitrary")),
        name="moba_decode",
    )(page_table, q_tiled, slope_rows, kn_t, vn_t, *([cache_k] * pages_per_step), *([cache_v] * pages_per_step))


_CHUNKS_PER_PAGE = PAGE_SIZE // CMP_STRIDE


def _page_ab_body(k_ref, wj_ref, o_ref, xs):
    pages, groups = k_ref.shape[1], k_ref.shape[2]
    for gp in range(groups // 2):
        both = jnp.concatenate([k_ref[0, :, 2 * gp], k_ref[0, :, 2 * gp + 1]], axis=1)
        xs[...] = jnp.swapaxes(both, 1, 2)
        acc = jnp.zeros((pages * _CHUNKS_PER_PAGE, 2 * LANES), F32)
        for j in range(CMP_STRIDE):
            x = xs[:, pl.ds(j, _CHUNKS_PER_PAGE, stride=CMP_STRIDE), :]
            x = x.reshape(pages * _CHUNKS_PER_PAGE, 2 * HEAD_DIM).astype(BF16)
            acc = acc + jnp.dot(x, wj_ref[j], preferred_element_type=F32)
        o_ref[:, 2 * gp * LANES:2 * (gp + 1) * LANES] = acc


def nsa_page_ab(cache, layer, wj):
    _, n_phys, groups, _, _ = cache.shape
    assert groups % 2 == 0
    pages = next(p for p in (32, 16, 8, 4, 2, 1) if n_phys % p == 0)
    zero = jnp.zeros_like(wj)
    wj = jnp.concatenate([jnp.concatenate([wj, zero], axis=2), jnp.concatenate([zero, wj], axis=2)], axis=1)
    return pl.pallas_call(
        _page_ab_body,
        grid=(n_phys // pages,),
        in_specs=[pl.BlockSpec((1, pages, groups, HEAD_DIM, PAGE_SIZE), lambda i: (layer, i, 0, 0, 0)),
                  pl.BlockSpec(wj.shape, lambda i: (0, 0, 0))],
        out_specs=pl.BlockSpec((pages * _CHUNKS_PER_PAGE, groups * LANES), lambda i: (i, 0)),
        out_shape=jax.ShapeDtypeStruct((n_phys * _CHUNKS_PER_PAGE, groups * LANES), F32),
        scratch_shapes=[pltpu.VMEM((pages, PAGE_SIZE, 2 * HEAD_DIM), F32)],
        compiler_params=_cparams(("parallel",)),
        name="nsa_page_ab",
    )(cache, wj)


def _page_ab_copy(ab_hbm, pt_ref, s, p, buf, sem):
    return pltpu.make_async_copy(ab_hbm.at[pt_ref[s, p]], buf.at[pl.ds(p * _CHUNKS_PER_PAGE, _CHUNKS_PER_PAGE), :], sem)


def _decode_queries(q_ref, slope_ref):
    q = q_ref[0]
    lane = lax.broadcasted_iota(jnp.int32, q.shape, 1)
    s_hi, s_lo = _split_hi_lo(jnp.broadcast_to(slope_ref[...], q.shape))
    feat = jnp.where(((lane - _POS_LANE) % 2) == 1, s_lo.astype(F32), s_hi.astype(F32))
    return jnp.where(lane < HEAD_DIM, q * HEAD_DIM ** -0.5,
                     jnp.where(lane < _POS_LANE + 4, feat, 0.0)).astype(BF16)


def _nsa_decode_cmp_body(n_pages, past, pt_ref, q_ref, slope_ref, abk_ref, abv_ref, wjk_ref, wjv_ref, pek_ref,
                         pev_ref, w2k_ref, w2v_ref, cover_ref, oc_ref, pen_ref, bufk, bufv, sem):
    s = pl.program_id(0)
    c = n_pages * _CHUNKS_PER_PAGE
    rows = q_ref.shape[1]
    groups = abk_ref.shape[2] // LANES
    rpg = rows // groups
    nq = pen_ref.shape[2] // 2

    def issue(p, _):
        _page_ab_copy(abk_ref, pt_ref, s, p, bufk, sem.at[0]).start()
        _page_ab_copy(abv_ref, pt_ref, s, p, bufv, sem.at[1]).start()
        return 0

    def drain(p, _):
        _page_ab_copy(abk_ref, pt_ref, s, p, bufk, sem.at[0]).wait()
        _page_ab_copy(abv_ref, pt_ref, s, p, bufv, sem.at[1]).wait()
        return 0

    lax.fori_loop(0, n_pages, issue, 0)
    pad = jnp.zeros((SUBLANES, bufk.shape[1]), F32)
    bufk[c:c + SUBLANES, :] = pad
    bufv[c:c + SUBLANES, :] = pad
    lax.fori_loop(0, n_pages, drain, 0)

    def tokens(buf, wj_ref, pe_ref, w2_ref, g, add_pos):
        acc = jnp.zeros((SUBLANES, LANES), F32)
        for j in range(CMP_STRIDE):
            xj = jnp.concatenate([pe_ref[j:j + 1, :], pe_ref[CMP_STRIDE + j:CMP_STRIDE + j + 1, :],
                                  jnp.zeros((SUBLANES - 2, HEAD_DIM), F32)], axis=0).astype(BF16)
            acc = acc + jnp.dot(xj, wj_ref[j], preferred_element_type=F32)
        bias = acc[0:1] + pltpu.roll(acc, HEAD_DIM, 1)[1:2]
        slab = slice(g * LANES, (g + 1) * LANES)
        h = buf[0:c, slab] + pltpu.roll(buf[1:c + 1, slab], HEAD_DIM, 1) + bias
        out = jnp.dot(jax.nn.gelu(h).astype(BF16), w2_ref[...], preferred_element_type=F32)
        if add_pos:
            lane = lax.broadcasted_iota(jnp.int32, out.shape, 1)
            cend = lax.broadcasted_iota(jnp.int32, out.shape, 0) * CMP_STRIDE + (CMP_LEN - 1)
            hi = ((cend // 256) * 256).astype(F32)
            lo = (cend % 256).astype(F32)
            out = jnp.where((lane == _POS_LANE) | (lane == _POS_LANE + 1), hi,
                            jnp.where((lane == _POS_LANE + 2) | (lane == _POS_LANE + 3), lo, out))
        return out.astype(BF16)

    qa = _decode_queries(q_ref, slope_ref)
    for g in range(groups):
        ck = tokens(bufk, wjk_ref, pek_ref, w2k_ref, g, True)
        cv = tokens(bufv, wjv_ref, pev_ref, w2v_ref, g, False)
        s_c = _dot_nt(qa[g * rpg:(g + 1) * rpg], ck)
        cend = lax.broadcasted_iota(jnp.int32, s_c.shape, 1) * CMP_STRIDE + (CMP_LEN - 1)
        t_q = past + lax.broadcasted_iota(jnp.int32, s_c.shape, 0) % nq
        ok = cend <= t_q
        s_c = jnp.where(ok, s_c, NEG_BIG)
        e_c = jnp.where(ok, jnp.exp(s_c - jnp.max(s_c, axis=-1, keepdims=True)), 0.0)
        p_c = e_c / jnp.maximum(jnp.sum(e_c, axis=-1, keepdims=True), 1e-30)
        oc_ref[0, g * rpg:(g + 1) * rpg, :] = jnp.dot(p_c.astype(BF16), cv, preferred_element_type=F32)
        half = p_c[0:2 * nq]
        for u in range(1, rpg // (2 * nq)):
            half = half + p_c[2 * nq * u:2 * nq * (u + 1)]
        psum = half + pltpu.roll(half, nq, 0)
        imp = jnp.dot(psum, cover_ref[...], precision=HIGHEST, preferred_element_type=F32)
        lane = lax.broadcasted_iota(jnp.int32, imp.shape, 1)
        qb = (past + lax.broadcasted_iota(jnp.int32, imp.shape, 0) % nq) // SEL_BLOCK
        pen_ref[0, g] = jnp.where(_select_blocks(imp, qb, lane), 0.0, NEG_BIG)


def nsa_decode_cmp(q_rows, slope_rows, ab_k, ab_v, page_table, wjk, wjv, pek, pev, w2k, w2v, cover, nq):
    b, rows, _ = q_rows.shape
    n_pages = page_table.shape[1]
    past = n_pages * PAGE_SIZE
    groups = ab_k.shape[1] // LANES
    c = n_pages * _CHUNKS_PER_PAGE
    assert past % CMP_STRIDE == 0 and nq < CMP_STRIDE and 2 * nq == SUBLANES
    ab_k3 = ab_k.reshape(-1, _CHUNKS_PER_PAGE, groups * LANES)
    ab_v3 = ab_v.reshape(-1, _CHUNKS_PER_PAGE, groups * LANES)
    full = lambda a: pl.BlockSpec(a.shape, lambda s, pt: (0,) * a.ndim)
    grid_spec = pltpu.PrefetchScalarGridSpec(
        num_scalar_prefetch=1,
        grid=(b,),
        in_specs=[pl.BlockSpec((1, rows, LANES), lambda s, pt: (s, 0, 0)), full(slope_rows),
                  pl.BlockSpec(memory_space=pl.ANY), pl.BlockSpec(memory_space=pl.ANY),
                  full(wjk), full(wjv), full(pek), full(pev), full(w2k), full(w2v), full(cover)],
        out_specs=[pl.BlockSpec((1, rows, LANES), lambda s, pt: (s, 0, 0)),
                   pl.BlockSpec((1, groups, 2 * nq, cover.shape[1]), lambda s, pt: (s, 0, 0, 0))],
        scratch_shapes=[pltpu.VMEM((c + SUBLANES, groups * LANES), F32),
                        pltpu.VMEM((c + SUBLANES, groups * LANES), F32),
                        pltpu.SemaphoreType.DMA((2,))],
    )
    return pl.pallas_call(
        functools.partial(_nsa_decode_cmp_body, n_pages, past),
        grid_spec=grid_spec,
        out_shape=[jax.ShapeDtypeStruct((b, rows, LANES), F32),
                   jax.ShapeDtypeStruct((b, groups, 2 * nq, cover.shape[1]), F32)],
        compiler_params=_cparams(("arbitrary",)),
        name="nsa_decode_cmp",
    )(page_table, q_rows, slope_rows, ab_k3, ab_v3, wjk, wjv, pek, pev, w2k, w2v, cover)


_SEL_PAGES_PER_STEP = 8


def _nsa_decode_sel_body(n_steps, past, wbuf, pt_ref, q_ref, slope_ref, pen_ref, oc_ref, gt_ref, ksn_ref, vsn_ref,
                         kwn_ref, vwn_ref, wk_ref, wv_ref, *refs):
    del pt_ref
    pp = _SEL_PAGES_PER_STEP
    k_refs, v_refs = refs[:pp], refs[pp:2 * pp]
    o_ref, m_s, l_s, acc_s = refs[2 * pp:]
    step = pl.program_id(1)
    rows, width = q_ref.shape[1], q_ref.shape[2]
    groups = pen_ref.shape[1]
    rpg = rows // groups
    nq = pen_ref.shape[2] // 2
    n_sel = pen_ref.shape[3]
    own = _own_head_block(rows, width, rpg)
    qb = jnp.where(own, q_ref[0] * HEAD_DIM ** -0.5, 0.0).astype(BF16)
    slope = slope_ref[...]
    blocks_per_page = PAGE_SIZE // SEL_BLOCK
    pen_rows = jnp.concatenate([pen_ref[0, g] for g in range(groups) for _ in range(rpg // (2 * nq))],
                               axis=0).astype(BF16)

    @pl.when(step == 0)
    def _():
        m_s[...] = jnp.full(m_s.shape, -jnp.inf, F32)
        l_s[...] = jnp.zeros(l_s.shape, F32)
        acc_s[...] = jnp.zeros(acc_s.shape, F32)

    def attend(carry, kt, vt, bias, mask=None):
        m, l, acc = carry
        s = jnp.dot(qb, kt, preferred_element_type=F32) + bias
        if mask is not None:
            s = jnp.where(mask, s, NEG_BIG)
        m_new = jnp.maximum(m, jnp.max(s, axis=-1, keepdims=True))
        p = jnp.exp(s - m_new)
        alpha = jnp.exp(m - m_new)
        return m_new, alpha * l + jnp.sum(p, axis=-1, keepdims=True), alpha * acc + _dot_nt(p.astype(BF16), vt)

    col = lax.broadcasted_iota(jnp.int32, (rows, PAGE_SIZE), 1)
    col_f = col.astype(F32)
    span = pp * PAGE_SIZE
    e_row = lax.broadcasted_iota(jnp.int32, (n_sel, span), 0)
    e_col = lax.broadcasted_iota(jnp.int32, (n_sel, span), 1)
    expand = jnp.where(e_row == step * (pp * blocks_per_page) + e_col // SEL_BLOCK, 1.0, 0.0).astype(BF16)
    pos = lax.broadcasted_iota(jnp.int32, (rows, span), 1).astype(F32) + (step * span - past).astype(F32)
    bias = jnp.dot(pen_rows, expand, preferred_element_type=F32) + slope * pos
    kt = jnp.concatenate([r[0, 0].reshape(width, PAGE_SIZE) for r in k_refs], axis=1).astype(BF16)
    vt = jnp.concatenate([r[0, 0].reshape(width, PAGE_SIZE) for r in v_refs], axis=1).astype(BF16)
    m_s[...], l_s[...], acc_s[...] = attend((m_s[...], l_s[...], acc_s[...]), kt, vt, bias)

    @pl.when(step == n_steps - 1)
    def _():
        gates = jax.nn.sigmoid(gt_ref[0])
        qi = lax.broadcasted_iota(jnp.int32, (rows, PAGE_SIZE), 0) % nq
        ok_n = (col <= qi) & (col < nq)
        pen_new = pen_rows[:, past // SEL_BLOCK:past // SEL_BLOCK + 1].astype(F32)
        _, l_sel, acc_sel = attend((m_s[...], l_s[...], acc_s[...]), ksn_ref[0].astype(BF16),
                                   vsn_ref[0].astype(BF16), slope * col_f + pen_new, ok_n)
        o_sel = _fold_heads(acc_sel, own) / l_sel
        wqi = lax.broadcasted_iota(jnp.int32, (rows, wbuf), 0) % nq
        wc = lax.broadcasted_iota(jnp.int32, (rows, wbuf), 1)
        dist_w = wqi + wbuf - wc
        ok_w = (dist_w >= 0) & (dist_w < WINDOW) & (past - wbuf + wc >= 0)
        init = (jnp.full((rows, 1), -jnp.inf, F32), jnp.zeros((rows, 1), F32), jnp.zeros((rows, width), F32))
        carry_w = attend(init, wk_ref[0, 0].reshape(width, wbuf).astype(BF16),
                         wv_ref[0, 0].reshape(width, wbuf).astype(BF16), slope * (wc - wbuf).astype(F32), ok_w)
        _, l_win, acc_win = attend(carry_w, kwn_ref[0].astype(BF16), vwn_ref[0].astype(BF16), slope * col_f, ok_n)
        o_win = _fold_heads(acc_win, own) / l_win
        o_ref[0] = (gates[:, 0:1] * oc_ref[0, :, 0:HEAD_DIM] + gates[:, 1:2] * o_sel + gates[:, 2:3] * o_win)


def nsa_decode_sel(q_tiled, slope_rows, pen, o_c, gt_rows, ks_new, vs_new, kw_new, vw_new, win_k, win_v,
                   cache_k, cache_v, page_table, layer):
    b, rows, _ = q_tiled.shape
    groups = cache_k.shape[2]
    n_pages = page_table.shape[1]
    past = n_pages * PAGE_SIZE
    pp = _SEL_PAGES_PER_STEP
    wbuf = win_k.shape[4]
    assert n_pages % pp == 0 and past % SEL_BLOCK == 0
    n_steps = n_pages // pp
    per_seq = lambda a: pl.BlockSpec((1,) + a.shape[1:], lambda s, j, pt: (s,) + (0,) * (a.ndim - 1))
    page = lambda u: pl.BlockSpec((1, 1, groups, HEAD_DIM, PAGE_SIZE),
                                  lambda s, j, pt: (layer, pt[s, pp * j + u], 0, 0, 0))
    win = pl.BlockSpec((1, 1, groups, HEAD_DIM, wbuf), lambda s, j, pt: (layer, s, 0, 0, 0))
    grid_spec = pltpu.PrefetchScalarGridSpec(
        num_scalar_prefetch=1,
        grid=(b, n_steps),
        in_specs=[per_seq(q_tiled), pl.BlockSpec(slope_rows.shape, lambda s, j, pt: (0, 0)),
                  per_seq(pen), per_seq(o_c), per_seq(gt_rows),
                  per_seq(ks_new), per_seq(vs_new), per_seq(kw_new), per_seq(vw_new), win, win]
        + [page(u) for u in range(pp)] * 2,
        out_specs=pl.BlockSpec((1, rows, HEAD_DIM), lambda s, j, pt: (s, 0, 0)),
        scratch_shapes=[pltpu.VMEM((rows, 1), F32), pltpu.VMEM((rows, 1), F32),
                        pltpu.VMEM((rows, groups * HEAD_DIM), F32)],
    )
    return pl.pallas_call(
        functools.partial(_nsa_decode_sel_body, n_steps, past, wbuf),
        grid_spec=grid_spec,
        out_shape=jax.ShapeDtypeStruct((b, rows, HEAD_DIM), F32),
        compiler_params=_cparams(("parallel", "arbitrary")),
        name="nsa_decode_sel",
    )(page_table, q_tiled, slope_rows, pen, o_c, gt_rows, ks_new, vs_new, kw_new, vw_new, win_k, win_v,
      *([cache_k] * pp), *([cache_v] * pp))


def _heads_to_rows(x, n, t, heads):
    return x.reshape(n, t, heads, HEAD_DIM).transpose(0, 2, 1, 3).reshape(n, heads * t, HEAD_DIM)


def _rows_to_heads(x, n, t, heads):
    return x.reshape(n, heads, t, HEAD_DIM).transpose(0, 2, 1, 3).reshape(n * t, heads * HEAD_DIM)


def _new_kv_cols(x, n, t):
    return jnp.pad(x.reshape(n, t, x.shape[-1]).transpose(0, 2, 1), ((0, 0), (0, 0), (0, PAGE_SIZE - t)))


def _heads_last(a, heads):
    n, _, t = a.shape
    return a.reshape(n, heads, HEAD_DIM, t).transpose(0, 3, 1, 2)


def _tokens_minor(a):
    return a.transpose(0, 1, 3, 4, 2)


def _layer_ab(layer, l, yp, ys, dims, p):
    bp, tp, bs, ts = dims
    d_rnn = p["conv_w"].shape[2]
    heads = (p["w_in_ab"].shape[2] - 2 * d_rnn) // (3 * HEAD_DIM)
    hw = heads * HEAD_DIM
    w_in = p["w_in_ab"][l].astype(BF16)
    cuts = [0, d_rnn, 2 * d_rnn, 2 * d_rnn + hw, 2 * d_rnn + 2 * hw, 2 * d_rnn + 3 * hw]
    ws = [w_in[:, a:b] for a, b in zip(cuts[:-1], cuts[1:])]
    w_out = p["w_out_ab"][l].astype(BF16)
    w_out = [w_out[:d_rnn], w_out[d_rnn:]]
    rg = (p["conv_w"][l], p["conv_b"][l], _block_diag(p["w_rg_a"][l]).astype(BF16), p["b_rg_a"][l],
          _block_diag(p["w_rg_i"][l]).astype(BF16), p["b_rg_i"][l], p["rg_lambda"][l])
    slopes = _alibi_slopes(heads)
    g = p["norm_mix"][layer]
    f32x5 = [(i, F32) for i in range(5)]

    xr, gr, q, k, k_fm, v_fm, kb, vt = norm_proj(
        yp, g, ws, f32x5[:4] + [(3, F32, True), (4, F32, True), (3, BF16), (4, BF16, True)], seq_len=tp)
    seq = lambda a: a.reshape(bp, tp, a.shape[-1])
    r_p, conv_p, h_p = rglru(seq(xr), seq(gr), jnp.zeros((bp, CONV_W - 1, d_rnn), F32),
                             jnp.zeros((bp, d_rnn), F32), *rg)
    att = moba_prompt(seq(q), seq(kb), vt, block_mean(seq(k), MOBA_BLOCK), slopes)
    yp = proj_residual([r_p.reshape(bp * tp, d_rnn), att.reshape(bp * tp, hw)], w_out, yp)
    out_p = (_heads_last(k_fm, heads), _heads_last(v_fm, heads), conv_p, h_p)

    xr, gr, q, k, v = norm_proj(ys, g, ws, f32x5)
    seq = lambda a: a.reshape(bs, ts, a.shape[-1])
    r_s, conv_s, h_s = rglru(seq(xr), seq(gr), p["state_conv"][l], p["state_rglru"][l], *rg)
    att = moba_decode(jnp.tile(_heads_to_rows(q, bs, ts, heads), (1, 1, heads)), _new_kv_cols(k, bs, ts),
                      _new_kv_cols(v, bs, ts), _tokens_minor(p["cache_moba_k"]), _tokens_minor(p["cache_moba_v"]),
                      p["page_table"], l, slopes)
    ys = proj_residual([r_s.reshape(bs * ts, d_rnn), _rows_to_heads(att, bs, ts, heads)], w_out, ys)
    out_s = (k.reshape(bs, ts, heads, HEAD_DIM), v.reshape(bs, ts, heads, HEAD_DIM), conv_s, h_s)
    return yp, ys, out_p, out_s


def _layer_c(layer, l, yp, ys, dims, p):
    bp, tp, bs, ts = dims
    groups = p["cache_cmp_k"].shape[3]
    gw = groups * HEAD_DIM
    w_in = p["w_in_c"][l]
    heads = (w_in.shape[1] - 6 * gw) // (HEAD_DIM + 3)
    qw = heads * HEAD_DIM
    cuts = [0, qw] + [qw + (i + 1) * gw for i in range(6)]
    w_f32 = [w_in[:, a:b] for a, b in zip(cuts[:-1], cuts[1:])]
    w_gt = w_in[:, cuts[-1]:]
    slopes = _alibi_slopes(heads)
    g = p["norm_mix"][layer]
    w_out = [p["w_out_c"][l].astype(BF16)]
    cmp_k = (p["w_cmp_k1"][l], p["w_cmp_k2"][l], p["pe_cmp_k"][l])
    cmp_v = (p["w_cmp_v1"][l], p["w_cmp_v2"][l], p["pe_cmp_v"][l])
    kv4 = lambda a, n, t: a.reshape(n, t, groups, HEAD_DIM)

    ws = ([w.astype(BF16) for w in w_f32] + [_pad_group_cols(w_gt, 3 * heads // groups).astype(BF16)]
          + [_pad_group_cols(w, HEAD_DIM).astype(BF16) for w in w_f32[3:7]])
    outs = ([(0, F32), (1, F32), (2, F32), (7, F32)] + [(i, F32, True) for i in range(1, 7)]
            + [(8, BF16), (9, BF16, True), (10, BF16), (11, BF16, True)])
    q, kc, vc, gt, *kv_fm, ksp, vst, kwp, vwt = norm_proj(yp, g, ws, outs, seq_len=tp)
    seq = lambda a: a.reshape(bp, tp, a.shape[-1])
    ck = nsa_compress(seq(kc), *_compress_weights(*cmp_k, groups), True)
    cvt = nsa_compress(seq(vc), *_compress_weights(*cmp_v, groups), False)
    cover_t = _nsa_cover(tp // CMP_STRIDE, max(-(-tp // SEL_BLOCK), SEL_TOPK)).T
    att = nsa_prompt(seq(q), seq(gt), ck, cvt, cover_t, seq(ksp), vst, seq(kwp), vwt, slopes)
    yp = proj_residual([att.reshape(bp * tp, qw)], w_out, yp)
    w_keep = min(WINDOW, tp)
    out_p = (tuple(_heads_last(a, groups) for a in kv_fm[:4])
             + tuple(_heads_last(a[:, :, tp - w_keep:], groups) for a in kv_fm[4:]))

    ws = [w.astype(BF16) for w in w_f32] + [jnp.pad(w_gt, ((0, 0), (0, LANES - w_gt.shape[1]))).astype(BF16)]
    q, kc, vc, ks, vs, kw, vw, gt = norm_proj(ys, g, ws, [(i, F32) for i in range(8)])
    q_rows = _heads_to_rows(q, bs, ts, heads)
    q_tiled = jnp.tile(q_rows, (1, 1, groups))
    q_rows = jnp.pad(q_rows, ((0, 0), (0, 0), (0, LANES - HEAD_DIM)))
    slope_rows = jnp.asarray(np.repeat(slopes, ts).reshape(heads * ts, 1))
    gt_rows = gt[:, :3 * heads].reshape(bs, ts, heads, 3).transpose(0, 2, 1, 3).reshape(bs, heads * ts, 3)
    n_pages = p["page_table"].shape[1]
    past = n_pages * PAGE_SIZE
    cover = _nsa_cover(n_pages * _CHUNKS_PER_PAGE, max(-(-(past + ts) // SEL_BLOCK), SEL_TOPK))
    dec_w = []
    for w1, w2, pe in (cmp_k, cmp_v):
        w1j = w1.reshape(CMP_LEN, HEAD_DIM, HEAD_DIM)
        dec_w.append((jnp.concatenate([w1j[:CMP_STRIDE], w1j[CMP_STRIDE:]], axis=2).astype(BF16), pe,
                      jnp.pad(w2, ((0, LANES - HEAD_DIM), (0, LANES - HEAD_DIM))).astype(BF16)))
    (wjk, pek, w2k), (wjv, pev, w2v) = dec_w
    ab_k = nsa_page_ab(_tokens_minor(p["cache_cmp_k"]), l, wjk)
    ab_v = nsa_page_ab(_tokens_minor(p["cache_cmp_v"]), l, wjv)
    o_c, pen = nsa_decode_cmp(q_rows, slope_rows, ab_k, ab_v, p["page_table"], wjk, wjv, pek, pev, w2k, w2v,
                              cover, ts)
    att = nsa_decode_sel(q_tiled, slope_rows, pen, o_c, gt_rows, _new_kv_cols(ks, bs, ts),
                         _new_kv_cols(vs, bs, ts), _new_kv_cols(kw, bs, ts), _new_kv_cols(vw, bs, ts),
                         _tokens_minor(p["state_win_k"]), _tokens_minor(p["state_win_v"]),
                         _tokens_minor(p["cache_sel_k"]), _tokens_minor(p["cache_sel_v"]), p["page_table"], l)
    ys = proj_residual([_rows_to_heads(att, bs, ts, heads)], w_out, ys)
    w_buf = p["state_win_k"].shape[2]
    win = lambda state, new: jnp.concatenate([state[l], kv4(new, bs, ts)], axis=1)[:, -w_buf:]
    out_s = tuple(kv4(a, bs, ts) for a in (kc, vc, ks, vs)) + (win(p["state_win_k"], kw), win(p["state_win_v"], vw))
    return yp, ys, out_p, out_s


def kernel(x_prompt, x_sample, cache_moba_k, cache_moba_v, cache_cmp_k, cache_cmp_v, cache_sel_k, cache_sel_v,
           state_win_k, state_win_v, state_conv, state_rglru, page_table, norm_mix, norm_ffn, norm_final,
           w_in_ab, w_out_ab, conv_w, conv_b, w_rg_a, b_rg_a, w_rg_i, b_rg_i, rg_lambda, w_in_c, w_out_c,
           w_cmp_k1, w_cmp_k2, pe_cmp_k, w_cmp_v1, w_cmp_v2, pe_cmp_v, w_route_group, b_route_group,
           w_route_expert, b_route_expert, w_exp_gate, w_exp_up, w_exp_down):
    p = dict(cache_moba_k=cache_moba_k, cache_moba_v=cache_moba_v, cache_cmp_k=cache_cmp_k,
             cache_cmp_v=cache_cmp_v, cache_sel_k=cache_sel_k, cache_sel_v=cache_sel_v,
             state_win_k=state_win_k, state_win_v=state_win_v, state_conv=state_conv, state_rglru=state_rglru,
             page_table=page_table, norm_mix=norm_mix, w_in_ab=w_in_ab, w_out_ab=w_out_ab, conv_w=conv_w,
             conv_b=conv_b, w_rg_a=w_rg_a, b_rg_a=b_rg_a, w_rg_i=w_rg_i, b_rg_i=b_rg_i, rg_lambda=rg_lambda,
             w_in_c=w_in_c, w_out_c=w_out_c, w_cmp_k1=w_cmp_k1, w_cmp_k2=w_cmp_k2, pe_cmp_k=pe_cmp_k,
             w_cmp_v1=w_cmp_v1, w_cmp_v2=w_cmp_v2, pe_cmp_v=pe_cmp_v)
    bp, tp, d = x_prompt.shape
    bs, ts, _ = x_sample.shape
    dims = (bp, tp, bs, ts)
    yp = x_prompt.reshape(bp * tp, d)
    ys = x_sample.reshape(bs * ts, d)
    ab_p, ab_s, c_p, c_s = [], [], [], []
    for layer in range(norm_mix.shape[0]):
        l = layer // 2
        if layer % 2 == 0:
            yp, ys, o_p, o_s = _layer_ab(layer, l, yp, ys, dims, p)
            ab_p.append(o_p)
            ab_s.append(o_s)
        else:
            yp, ys, o_p, o_s = _layer_c(layer, l, yp, ys, dims, p)
            c_p.append(o_p)
            c_s.append(o_s)
        wr, br = _router_weights(w_route_group[layer], b_route_group[layer], w_route_expert[layer],
                                 b_route_expert[layer])
        moe_w = (w_exp_gate[layer].astype(BF16), w_exp_up[layer].astype(BF16), w_exp_down[layer].astype(BF16))
        yp = hier_moe_residual(yp, norm_ffn[layer], wr, br, *moe_w)
        ys = hier_moe_residual(ys, norm_ffn[layer], wr, br, *moe_w)
    y_prompt = rmsnorm(yp, norm_final).reshape(bp, tp, d)
    y_sample = rmsnorm(ys, norm_final).reshape(bs, ts, d)
    stack = lambda outs, i: jnp.stack([o[i] for o in outs])
    res = [y_prompt, y_sample]
    for i in range(4):
        res += [stack(ab_p, i), stack(ab_s, i)]
    for i in range(6):
        res += [stack(c_p, i), stack(c_s, i)]
    return tuple(res)
```

```python
import functools

import numpy as np
import jax
import jax.numpy as jnp
from jax import lax
from jax.experimental import pallas as pl
from jax.experimental.pallas import tpu as pltpu

F32 = jnp.float32
BF16 = jnp.bfloat16
HIGHEST = lax.Precision.HIGHEST

HEAD_DIM = 64
LANES = 128
SUBLANES = 8
CONV_W = 4
RG_C = 8.0
MOBA_BLOCK = 256
MOBA_TOPK = 3
CMP_STRIDE = 16
CMP_LEN = 32
SEL_BLOCK = 64
SEL_TOPK = 16
WINDOW = 512
N_GROUPS = 4
EXPERTS_PER_GROUP = 8
N_EXPERTS = N_GROUPS * EXPERTS_PER_GROUP
PAGE_SIZE = 128
RMS_EPS = 1e-6
NEG_BIG = -1e30
VMEM_LIMIT = 56 * 1024 * 1024
MOE_BLOCK_ROWS = 256


def _cparams(sem):
    return pltpu.CompilerParams(dimension_semantics=sem, vmem_limit_bytes=VMEM_LIMIT)


def _alibi_slopes(n):
    return np.asarray(2.0 ** (-8.0 * np.arange(1, n + 1) / n), dtype=np.float32)


def _row_tile(m, pref=512):
    return pref if m % pref == 0 else m


def _dot_nt(a, b, precision=None):
    return lax.dot_general(a, b, (((1,), (1,)), ((), ())), precision=precision,
                           preferred_element_type=F32)


def _norm_proj_body(out_map, x_ref, g_ref, *refs):
    n_w = max(wi for wi, _ in out_map) + 1
    w_refs, o_refs = refs[:n_w], refs[n_w:]
    x = x_ref[...]
    y = x * lax.rsqrt(jnp.mean(x * x, axis=-1, keepdims=True) + RMS_EPS) * g_ref[...]
    yb = y.astype(BF16)
    prods = [jnp.dot(yb, w_ref[...], preferred_element_type=F32) for w_ref in w_refs]
    for (wi, feature_major), o_ref in zip(out_map, o_refs):
        if feature_major:
            o_ref[0] = prods[wi].T.astype(o_ref.dtype)
        else:
            o_ref[...] = prods[wi].astype(o_ref.dtype)


def norm_proj(x, g, ws, outs, seq_len=None):
    m, d = x.shape
    tm = _row_tile(m)
    outs = [o if len(o) == 3 else (o[0], o[1], False) for o in outs]
    out_map = tuple((wi, fm) for wi, _, fm in outs)
    tiles = (seq_len or tm) // tm
    out_specs, out_shape = [], []
    for wi, dt, fm in outs:
        cols = ws[wi].shape[1]
        if fm:
            assert seq_len % tm == 0
            out_specs.append(pl.BlockSpec((1, cols, tm), lambda i: (i // tiles, 0, i % tiles)))
            out_shape.append(jax.ShapeDtypeStruct((m // seq_len, cols, seq_len), dt))
        else:
            out_specs.append(pl.BlockSpec((tm, cols), lambda i: (i, 0)))
            out_shape.append(jax.ShapeDtypeStruct((m, cols), dt))
    return pl.pallas_call(
        functools.partial(_norm_proj_body, out_map),
        grid=(m // tm,),
        in_specs=[pl.BlockSpec((tm, d), lambda i: (i, 0)), pl.BlockSpec((1, d), lambda i: (0, 0))]
        + [pl.BlockSpec(w.shape, lambda i: (0, 0)) for w in ws],
        out_specs=out_specs,
        out_shape=out_shape,
        compiler_params=_cparams(("parallel",)),
        name="norm_proj",
    )(x, g.reshape(1, d), *ws)


def _proj_res_body(n_in, *refs):
    x_refs, w_refs = refs[:n_in], refs[n_in:2 * n_in]
    res_ref, o_ref = refs[2 * n_in], refs[2 * n_in + 1]
    acc = jnp.dot(x_refs[0][...].astype(BF16), w_refs[0][...], preferred_element_type=F32)
    for x_ref, w_ref in zip(x_refs[1:], w_refs[1:]):
        acc = acc + jnp.dot(x_ref[...].astype(BF16), w_ref[...], preferred_element_type=F32)
    o_ref[...] = res_ref[...] + acc


def proj_residual(xs, ws, res):
    m, d = res.shape
    tm = _row_tile(m)
    return pl.pallas_call(
        functools.partial(_proj_res_body, len(xs)),
        grid=(m // tm,),
        in_specs=[pl.BlockSpec((tm, x.shape[1]), lambda i: (i, 0)) for x in xs]
        + [pl.BlockSpec(w.shape, lambda i: (0, 0)) for w in ws]
        + [pl.BlockSpec((tm, d), lambda i: (i, 0))],
        out_specs=pl.BlockSpec((tm, d), lambda i: (i, 0)),
        out_shape=jax.ShapeDtypeStruct((m, d), F32),
        compiler_params=_cparams(("parallel",)),
        name="proj_residual",
    )(*xs, *ws, res)


def _rmsnorm_body(x_ref, g_ref, o_ref):
    x = x_ref[...]
    o_ref[...] = x * lax.rsqrt(jnp.mean(x * x, axis=-1, keepdims=True) + RMS_EPS) * g_ref[...]


def rmsnorm(x, g):
    m, d = x.shape
    tm = _row_tile(m)
    return pl.pallas_call(
        _rmsnorm_body,
        grid=(m // tm,),
        in_specs=[pl.BlockSpec((tm, d), lambda i: (i, 0)), pl.BlockSpec((1, d), lambda i: (0, 0))],
        out_specs=pl.BlockSpec((tm, d), lambda i: (i, 0)),
        out_shape=jax.ShapeDtypeStruct((m, d), F32),
        compiler_params=_cparams(("parallel",)),
        name="rmsnorm",
    )(x, g.reshape(1, d))


_XIN_PAD = 8


def _rglru_body(xr_ref, gr_ref, cb_ref, h0_ref, cw_ref, cbias_ref, wa_ref, ba_ref, wi_ref, bi_ref,
                lam_ref, y_ref, tail_ref, hl_ref, xin_s, a_s, b_s, h_s):
    tt = xr_ref.shape[1]
    keep = CONV_W - 1

    @pl.when(pl.program_id(1) == 0)
    def _():
        xin_s[_XIN_PAD - keep:_XIN_PAD, :] = cb_ref[0]
        h_s[...] = h0_ref[0]

    xin_s[_XIN_PAD:_XIN_PAD + tt, :] = xr_ref[0]
    xc = cbias_ref[...]
    for k in range(CONV_W):
        lo = _XIN_PAD - keep + k
        xc = xc + xin_s[lo:lo + tt, :] * cw_ref[k:k + 1, :]
    new_tail = xin_s[_XIN_PAD + tt - keep:_XIN_PAD + tt, :]
    xin_s[_XIN_PAD - keep:_XIN_PAD, :] = new_tail
    tail_ref[0] = new_tail

    xcb = xc.astype(BF16)
    r = jax.nn.sigmoid(jnp.dot(xcb, wa_ref[...], preferred_element_type=F32) + ba_ref[...])
    gi = jax.nn.sigmoid(jnp.dot(xcb, wi_ref[...], preferred_element_type=F32) + bi_ref[...])
    nl = -lam_ref[...]
    softplus = jnp.maximum(nl, 0.0) + jnp.log1p(jnp.exp(-jnp.abs(nl)))
    log_a = (-RG_C) * r * softplus
    a = jnp.exp(log_a)
    a_s[...] = a
    b_s[...] = jnp.sqrt(1.0 - a * a) * (gi * xc)

    def step(t, h):
        h = a_s[pl.ds(t, 1), :] * h + b_s[pl.ds(t, 1), :]
        b_s[pl.ds(t, 1), :] = h
        return h

    h = lax.fori_loop(0, tt, step, h_s[...], unroll=min(tt, 8))
    h_s[...] = h
    hl_ref[0] = h
    y_ref[0] = b_s[...] * jax.nn.gelu(gr_ref[0])


def rglru(xr, gr, conv_buf, h0, conv_w, conv_b, wa_bd, b_a, wi_bd, b_i, lam):
    n, t, w = xr.shape
    tt = min(t, 512)
    assert t % tt == 0 and tt >= CONV_W - 1
    row = lambda v: v.reshape(1, w)
    full = lambda shp: pl.BlockSpec(shp, lambda b, s: (0,) * len(shp))
    y, tail, hl = pl.pallas_call(
        _rglru_body,
        grid=(n, t // tt),
        in_specs=[pl.BlockSpec((1, tt, w), lambda b, s: (b, s, 0)),
                  pl.BlockSpec((1, tt, w), lambda b, s: (b, s, 0)),
                  pl.BlockSpec((1, CONV_W - 1, w), lambda b, s: (b, 0, 0)),
                  pl.BlockSpec((1, 1, w), lambda b, s: (b, 0, 0)),
                  full((CONV_W, w)), full((1, w)), full((w, w)), full((1, w)), full((w, w)),
                  full((1, w)), full((1, w))],
        out_specs=[pl.BlockSpec((1, tt, w), lambda b, s: (b, s, 0)),
                   pl.BlockSpec((1, CONV_W - 1, w), lambda b, s: (b, 0, 0)),
                   pl.BlockSpec((1, 1, w), lambda b, s: (b, 0, 0))],
        out_shape=[jax.ShapeDtypeStruct((n, t, w), F32),
                   jax.ShapeDtypeStruct((n, CONV_W - 1, w), F32),
                   jax.ShapeDtypeStruct((n, 1, w), F32)],
        scratch_shapes=[pltpu.VMEM((_XIN_PAD + tt, w), F32), pltpu.VMEM((tt, w), F32),
                        pltpu.VMEM((tt, w), F32), pltpu.VMEM((1, w), F32)],
        compiler_params=_cparams(("parallel", "arbitrary")),
        name="rglru",
    )(xr, gr, conv_buf, h0.reshape(n, 1, w), conv_w, row(conv_b), wa_bd, row(b_a), wi_bd, row(b_i), row(lam))
    return y, tail, hl.reshape(n, w)


def _block_diag(w):
    h, d, _ = w.shape
    eye = jnp.eye(h, dtype=w.dtype)
    return (eye[:, None, :, None] * w[:, :, None, :]).reshape(h * d, h * d)


def _block_mean_body(blk, k_ref, o_ref):
    x = k_ref[0]
    nb = x.shape[0] // blk
    o_ref[0] = jnp.mean(x.reshape(nb, blk, x.shape[1]), axis=1)


def block_mean(k, blk):
    n, t, w = k.shape
    nb = t // blk
    g = SUBLANES if nb % SUBLANES == 0 else nb
    return pl.pallas_call(
        functools.partial(_block_mean_body, blk),
        grid=(n, nb // g),
        in_specs=[pl.BlockSpec((1, g * blk, w), lambda b, s: (b, s, 0))],
        out_specs=pl.BlockSpec((1, g, w), lambda b, s: (b, s, 0)),
        out_shape=jax.ShapeDtypeStruct((n, nb, w), F32),
        compiler_params=_cparams(("parallel", "parallel")),
        name="block_mean",
    )(k)


def _first_argmax(g, idx_f):
    m = jnp.max(g, axis=-1, keepdims=True)
    return jnp.min(jnp.where(g == m, idx_f, jnp.inf), axis=-1, keepdims=True)


def _split_hi_lo(x):
    hi = x.astype(BF16)
    lo = (x - hi.astype(F32)).astype(BF16)
    return hi, lo


LOG2E = 1.4426950408889634
_QSCALE = HEAD_DIM ** -0.5 * LOG2E
_MOBA_PAIRS_PER_STEP = 4


def _moba_prompt_body(nbk, slopes_ref, q_ref, km_ref, kb_ref, vt_ref, o_ref):
    pg = pl.program_id(1)
    i = pl.program_id(2)
    tq = q_ref.shape[1]
    pairs = q_ref.shape[2] // LANES
    qlane = lax.broadcasted_iota(jnp.int32, (tq, LANES), 1)
    blk = lax.broadcasted_iota(jnp.int32, (LANES, tq), 0)
    blk_f = blk.astype(F32)
    qts = []
    for pp in range(pairs):
        q = q_ref[0, :, pp * LANES:(pp + 1) * LANES]
        km = km_ref[0, :, pp * LANES:(pp + 1) * LANES]
        q_cols = []
        for hh in range(2):
            in_head = (qlane >= hh * HEAD_DIM) & (qlane < (hh + 1) * HEAD_DIM)
            qh = jnp.where(in_head, q, 0.0)
            gate = _dot_nt(km, qh, HIGHEST)
            g = jnp.where(blk < i, gate, -jnp.inf)
            sel = blk == i
            for _ in range(MOBA_TOPK):
                m = jnp.max(g, axis=0, keepdims=True)
                first = jnp.min(jnp.where(g == m, blk_f, jnp.inf), axis=0, keepdims=True)
                pick = blk_f == first
                sel = sel | (pick & (blk < i))
                g = jnp.where(pick, -jnp.inf, g)
            slope = slopes_ref[2 * (pg * pairs + pp) + hh]
            pb = jnp.where(sel, 0.0, NEG_BIG) + slope * (blk_f * MOBA_BLOCK)
            pb_hi, pb_lo = _split_hi_lo(pb)
            pb_lo = pltpu.roll(pb_lo.astype(F32), nbk, 0)
            s_hi, s_lo = _split_hi_lo(jnp.full((LANES, tq), slope, F32))
            aug = jnp.where(blk < nbk, pb_hi.astype(F32),
                            jnp.where(blk < 2 * nbk, pb_lo,
                                      jnp.where(blk == 2 * nbk, s_hi.astype(F32),
                                                jnp.where(blk == 2 * nbk + 1, s_lo.astype(F32), 0.0))))
            q_cols.append(jnp.concatenate([(qh * _QSCALE).T.astype(BF16), aug.astype(BF16)], axis=0))
        qts.append(jnp.concatenate(q_cols, axis=1))

    klane = lax.broadcasted_iota(jnp.int32, (MOBA_BLOCK, LANES), 1)
    krow = lax.broadcasted_iota(jnp.int32, (MOBA_BLOCK, LANES), 0).astype(F32)
    vrow = lax.broadcasted_iota(jnp.int32, (LANES, MOBA_BLOCK), 0)
    ones_row = (HEAD_DIM, 0)
    causal = (lax.broadcasted_iota(jnp.int32, (MOBA_BLOCK, tq), 0)
              <= lax.broadcasted_iota(jnp.int32, (MOBA_BLOCK, tq), 1))

    def scores(j):
        off = pl.multiple_of(j * MOBA_BLOCK, MOBA_BLOCK)
        kaug = jnp.where((klane == j) | (klane == nbk + j), 1.0,
                         jnp.where((klane == 2 * nbk) | (klane == 2 * nbk + 1), krow, 0.0)).astype(BF16)
        return [jnp.dot(jnp.concatenate([kb_ref[0, pl.ds(off, MOBA_BLOCK), pp * LANES:(pp + 1) * LANES], kaug],
                                        axis=1), qts[pp], preferred_element_type=F32)
                for pp in range(pairs)]

    def update(j, s, carry, diag):
        off = pl.multiple_of(j * MOBA_BLOCK, MOBA_BLOCK)
        new = []
        for pp in range(pairs):
            vt = vt_ref[0, pp * LANES:(pp + 1) * LANES, pl.ds(off, MOBA_BLOCK)]
            for hh in range(2):
                m, acc = carry[2 * pp + hh]
                sh = s[pp][:, hh * tq:(hh + 1) * tq]
                if diag:
                    sh = jnp.where(causal, sh, NEG_BIG)
                m_new = jnp.maximum(m, jnp.max(sh, axis=0, keepdims=True))
                p = jnp.exp2(sh - m_new).astype(BF16)
                in_head = (vrow >= hh * HEAD_DIM) & (vrow < (hh + 1) * HEAD_DIM)
                v_h = jnp.where(in_head, vt, jnp.where(vrow == ones_row[hh], 1.0, 0.0).astype(BF16))
                new.append((m_new, jnp.exp2(m - m_new) * acc + jnp.dot(v_h, p, preferred_element_type=F32)))
        return new

    def body(j, c):
        s, carry = c
        return scores(j + 1), update(j, s, carry, False)

    init = [(jnp.full((1, tq), -jnp.inf, F32), jnp.zeros((LANES, tq), F32)) for _ in range(2 * pairs)]
    s, carry = lax.fori_loop(0, i, body, (scores(0), init))
    carry = update(i, s, carry, True)
    orow = lax.broadcasted_iota(jnp.int32, (LANES, tq), 0)
    for pp in range(pairs):
        outs = [acc / acc[ones_row[hh]:ones_row[hh] + 1, :] for hh, (_, acc) in enumerate(carry[2 * pp:2 * pp + 2])]
        o_ref[0, :, pp * LANES:(pp + 1) * LANES] = jnp.where(orow < HEAD_DIM, outs[0], outs[1]).T


def moba_prompt(q, kb, vt, kmean, slopes):
    n, t, w = q.shape
    nbk = t // MOBA_BLOCK
    pw = _MOBA_PAIRS_PER_STEP * LANES
    assert t % MOBA_BLOCK == 0 and nbk >= MOBA_TOPK and 2 * nbk + 2 <= LANES and nbk % SUBLANES == 0 and w % pw == 0
    km = jnp.pad(kmean, ((0, 0), (0, LANES - nbk), (0, 0)))
    grid_spec = pltpu.PrefetchScalarGridSpec(
        num_scalar_prefetch=0,
        grid=(n, w // pw, nbk),
        in_specs=[pl.BlockSpec(memory_space=pltpu.SMEM),
                  pl.BlockSpec((1, MOBA_BLOCK, pw), lambda b, p, i: (b, i, p)),
                  pl.BlockSpec((1, LANES, pw), lambda b, p, i: (b, 0, p)),
                  pl.BlockSpec((1, t, pw), lambda b, p, i: (b, 0, p)),
                  pl.BlockSpec((1, pw, t), lambda b, p, i: (b, p, 0))],
        out_specs=pl.BlockSpec((1, MOBA_BLOCK, pw), lambda b, p, i: (b, i, p)),
    )
    return pl.pallas_call(
        functools.partial(_moba_prompt_body, nbk),
        grid_spec=grid_spec,
        out_shape=jax.ShapeDtypeStruct((n, t, w), F32),
        compiler_params=_cparams(("parallel", "parallel", "arbitrary")),
        name="moba_prompt",
    )(jnp.asarray(slopes * np.float32(LOG2E)), q, km, kb, vt)


_ROUTE_E, _ROUTE_W, _ROUTE_RANK = 0, 2, 4


def _router_body(x_ref, g_ref, wr_ref, br_ref, xn_ref, route_ref, cnt_ref, carry_s):
    tm = x_ref.shape[0]

    @pl.when(pl.program_id(0) == 0)
    def _():
        carry_s[...] = jnp.zeros_like(carry_s)

    x = x_ref[...]
    xn = x * lax.rsqrt(jnp.mean(x * x, axis=-1, keepdims=True) + RMS_EPS) * g_ref[...]
    xn_ref[...] = xn
    logits = jnp.dot(xn, wr_ref[...], precision=HIGHEST, preferred_element_type=F32) + br_ref[...]
    lane = lax.broadcasted_iota(jnp.int32, (tm, LANES), 1)
    is_g = lane < N_GROUPS
    gl = jnp.where(is_g, logits, -jnp.inf)
    gmax = jnp.max(gl, axis=-1, keepdims=True)
    g_idx = jnp.min(jnp.where(gl == gmax, lane, LANES), axis=-1, keepdims=True)
    g_w = 1.0 / jnp.sum(jnp.where(is_g, jnp.exp(gl - gmax), 0.0), axis=-1, keepdims=True)
    e_lo = N_GROUPS + g_idx * EXPERTS_PER_GROUP
    el = jnp.where((lane >= e_lo) & (lane < e_lo + EXPERTS_PER_GROUP), logits, -jnp.inf)
    v1 = jnp.max(el, axis=-1, keepdims=True)
    i1 = jnp.min(jnp.where(el == v1, lane, LANES), axis=-1, keepdims=True)
    el2 = jnp.where(lane == i1, -jnp.inf, el)
    v2 = jnp.max(el2, axis=-1, keepdims=True)
    i2 = jnp.min(jnp.where(el2 == v2, lane, LANES), axis=-1, keepdims=True)
    e21 = jnp.exp(v2 - v1)
    p1 = 1.0 / (1.0 + e21)
    w1 = g_w * p1
    w2 = g_w * (e21 * p1)
    hot1 = lane == i1
    hot2 = lane == i2
    cnt = jnp.where(hot1 | hot2, 1.0, 0.0)
    r = lax.broadcasted_iota(jnp.int32, (tm, tm), 0)
    c = lax.broadcasted_iota(jnp.int32, (tm, tm), 1)
    tri = jnp.where(c < r, 1.0, 0.0).astype(BF16)
    before = jnp.dot(tri, cnt.astype(BF16), preferred_element_type=F32) + carry_s[...]
    r1 = jnp.sum(jnp.where(hot1, before, 0.0), axis=-1, keepdims=True)
    r2 = jnp.sum(jnp.where(hot2, before, 0.0), axis=-1, keepdims=True)
    carry_s[...] = carry_s[...] + jnp.sum(cnt, axis=0, keepdims=True)
    cnt_ref[...] = carry_s[...]
    route = jnp.zeros((tm, LANES), F32)
    for pos, val in ((_ROUTE_E, (i1 - N_GROUPS).astype(F32)), (_ROUTE_E + 1, (i2 - N_GROUPS).astype(F32)),
                     (_ROUTE_W, w1), (_ROUTE_W + 1, w2), (_ROUTE_RANK, r1), (_ROUTE_RANK + 1, r2)):
        route = jnp.where(lane == pos, val, route)
    route_ref[...] = route


def moe_router(y, g, wr, br):
    m, d = y.shape
    tm = _row_tile(m, 256)
    return pl.pallas_call(
        _router_body,
        grid=(m // tm,),
        in_specs=[pl.BlockSpec((tm, d), lambda i: (i, 0)), pl.BlockSpec((1, d), lambda i: (0, 0)),
                  pl.BlockSpec((d, LANES), lambda i: (0, 0)), pl.BlockSpec((1, LANES), lambda i: (0, 0))],
        out_specs=[pl.BlockSpec((tm, d), lambda i: (i, 0)), pl.BlockSpec((tm, LANES), lambda i: (i, 0)),
                   pl.BlockSpec((1, LANES), lambda i: (0, 0))],
        out_shape=[jax.ShapeDtypeStruct((m, d), F32), jax.ShapeDtypeStruct((m, LANES), F32),
                   jax.ShapeDtypeStruct((1, LANES), F32)],
        scratch_shapes=[pltpu.VMEM((1, LANES), F32)],
        compiler_params=_cparams(("arbitrary",)),
        name="moe_router",
    )(y, g.reshape(1, d), wr, br)


def _row_copy(src, s_row, dst, d_row, sem):
    return pltpu.make_async_copy(src.at[pl.ds(s_row, 1), :], dst.at[pl.ds(d_row, 1), :], sem)


def _scatter_body(dest_ref, xn_ref, xs_in_ref, xs_ref, sem):
    del xs_in_ref
    tm = xn_ref.shape[0]
    base = pl.program_id(0) * tm

    def issue(r, _):
        for s in range(2):
            _row_copy(xn_ref, r, xs_ref, dest_ref[2 * (base + r) + s], sem).start()
        return 0

    lax.fori_loop(0, tm, issue, 0, unroll=4)
    for _ in range(2):
        pltpu.make_async_copy(xn_ref, xs_ref.at[pl.ds(0, tm), :], sem).wait()


def moe_scatter(dest, xn, rows):
    m, d = xn.shape
    tm = _row_tile(m, 256)
    grid_spec = pltpu.PrefetchScalarGridSpec(
        num_scalar_prefetch=1,
        grid=(m // tm,),
        in_specs=[pl.BlockSpec((tm, d), lambda i, dest: (i, 0)), pl.BlockSpec(memory_space=pl.ANY)],
        out_specs=pl.BlockSpec(memory_space=pl.ANY),
        scratch_shapes=[pltpu.SemaphoreType.DMA(())],
    )
    return pl.pallas_call(
        _scatter_body,
        grid_spec=grid_spec,
        out_shape=jax.ShapeDtypeStruct((rows, d), F32),
        input_output_aliases={2: 0},
        compiler_params=_cparams(("arbitrary",)),
        name="moe_scatter",
    )(dest, xn, jnp.zeros((rows, d), F32))


def _expert_body(blk_e_ref, nused_ref, xs_ref, w1_ref, w3_ref, w2_ref, ys_ref):
    del blk_e_ref
    used = pl.program_id(0) < nused_ref[0]

    @pl.when(used)
    def _():
        x = xs_ref[...].astype(BF16)
        h = jax.nn.silu(jnp.dot(x, w1_ref[0], preferred_element_type=F32))
        h = h * jnp.dot(x, w3_ref[0], preferred_element_type=F32)
        ys_ref[...] = jnp.dot(h.astype(BF16), w2_ref[0], preferred_element_type=F32)

    @pl.when(jnp.logical_not(used))
    def _():
        ys_ref[...] = jnp.zeros_like(ys_ref)


def moe_experts(blk_e, nused, xs, w1, w3, w2):
    rows, d = xs.shape
    te = MOE_BLOCK_ROWS
    de = w1.shape[2]
    grid_spec = pltpu.PrefetchScalarGridSpec(
        num_scalar_prefetch=2,
        grid=(rows // te,),
        in_specs=[pl.BlockSpec((te, d), lambda i, be, nu: (i, 0)),
                  pl.BlockSpec((1, d, de), lambda i, be, nu: (be[i], 0, 0)),
                  pl.BlockSpec((1, d, de), lambda i, be, nu: (be[i], 0, 0)),
                  pl.BlockSpec((1, de, d), lambda i, be, nu: (be[i], 0, 0))],
        out_specs=pl.BlockSpec((te, d), lambda i, be, nu: (i, 0)),
    )
    return pl.pallas_call(
        _expert_body,
        grid_spec=grid_spec,
        out_shape=jax.ShapeDtypeStruct((rows, d), F32),
        compiler_params=_cparams(("arbitrary",)),
        name="moe_experts",
    )(blk_e, nused, xs, w1, w3, w2)


def _combine_body(dest_ref, route_ref, res_ref, ys_ref, o_ref, buf, sem):
    tm = res_ref.shape[0]
    base = pl.program_id(0) * tm

    def issue(r, _):
        for s in range(2):
            _row_copy(ys_ref, dest_ref[2 * (base + r) + s], buf.at[s], r, sem).start()
        return 0

    lax.fori_loop(0, tm, issue, 0, unroll=4)
    for s in range(2):
        pltpu.make_async_copy(ys_ref.at[pl.ds(0, tm), :], buf.at[s], sem).wait()
    route = route_ref[...]
    w1 = route[:, _ROUTE_W:_ROUTE_W + 1]
    w2 = route[:, _ROUTE_W + 1:_ROUTE_W + 2]
    o_ref[...] = res_ref[...] + (w1 * buf[0] + w2 * buf[1])


def moe_combine(dest, route, res, ys):
    m, d = res.shape
    tm = _row_tile(m, 256)
    grid_spec = pltpu.PrefetchScalarGridSpec(
        num_scalar_prefetch=1,
        grid=(m // tm,),
        in_specs=[pl.BlockSpec((tm, LANES), lambda i, dest: (i, 0)),
                  pl.BlockSpec((tm, d), lambda i, dest: (i, 0)),
                  pl.BlockSpec(memory_space=pl.ANY)],
        out_specs=pl.BlockSpec((tm, d), lambda i, dest: (i, 0)),
        scratch_shapes=[pltpu.VMEM((2, tm, d), F32), pltpu.SemaphoreType.DMA(())],
    )
    return pl.pallas_call(
        _combine_body,
        grid_spec=grid_spec,
        out_shape=jax.ShapeDtypeStruct((m, d), F32),
        compiler_params=_cparams(("arbitrary",)),
        name="moe_combine",
    )(dest, route, res, ys)


def hier_moe_residual(y, g, wr, br, w1, w3, w2):
    m, d = y.shape
    te = MOE_BLOCK_ROWS
    xn, route, cnt = moe_router(y, g, wr, br)
    counts = cnt[0, N_GROUPS:N_GROUPS + N_EXPERTS].astype(jnp.int32)
    padded = (counts + te - 1) // te * te
    pend = jnp.cumsum(padded)
    pstart = pend - padded
    e = route[:, _ROUTE_E:_ROUTE_E + 2].astype(jnp.int32)
    rank = route[:, _ROUTE_RANK:_ROUTE_RANK + 2].astype(jnp.int32)
    dest = (pstart[e] + rank).reshape(-1)
    n_blk = -(-(2 * m + N_EXPERTS * (te - 1)) // te)
    blk_start = jnp.arange(n_blk, dtype=jnp.int32) * te
    blk_e = jnp.minimum(jnp.sum((pend[None, :] <= blk_start[:, None]).astype(jnp.int32), axis=1), N_EXPERTS - 1)
    nused = (pend[-1:] // te).astype(jnp.int32)
    xs = moe_scatter(dest, xn, n_blk * te)
    ys = moe_experts(blk_e, nused, xs, w1, w3, w2)
    return moe_combine(dest, route, y, ys)


def _router_weights(w_rg, b_rg, w_re, b_re):
    d = w_rg.shape[0]
    pad = LANES - N_GROUPS - N_EXPERTS
    wr = jnp.concatenate([w_rg, w_re, jnp.zeros((d, pad), F32)], axis=1)
    br = jnp.concatenate([b_rg, b_re, jnp.zeros((pad,), F32)]).reshape(1, LANES)
    return wr, br


_POS_LANE = HEAD_DIM
_TILE_BIAS_LANE = _POS_LANE + 4


def _compress_tail(ab, bias, w2_ref, add_pos):
    c = ab.shape[0] - SUBLANES
    gw = ab.shape[1] // 2
    h = ab[0:c, 0:gw] + ab[1:c + 1, gw:2 * gw] + bias
    out = jnp.dot(jax.nn.gelu(h).astype(BF16), w2_ref[...], preferred_element_type=F32)
    if add_pos:
        lane = lax.broadcasted_iota(jnp.int32, out.shape, 1) % LANES
        cend = lax.broadcasted_iota(jnp.int32, out.shape, 0) * CMP_STRIDE + (CMP_LEN - 1)
        hi = ((cend // 256) * 256).astype(F32)
        lo = (cend % 256).astype(F32)
        out = jnp.where((lane == _POS_LANE) | (lane == _POS_LANE + 1), hi,
                        jnp.where((lane == _POS_LANE + 2) | (lane == _POS_LANE + 3), lo, out))
    return out


def _compress_body(add_pos, n_k, *refs):
    k_refs = refs[:n_k]
    wab_ref, pe_ref, w2_ref, o_ref, ab_s = refs[n_k:]
    c = k_refs[0].shape[1] // CMP_STRIDE
    gw = n_k * LANES
    acc = jnp.zeros((c + SUBLANES, 2 * gw), F32)
    for j in range(CMP_STRIDE):
        xk = jnp.concatenate([r[0, pl.ds(j, c, stride=CMP_STRIDE), :] for r in k_refs], axis=1)
        xj = jnp.concatenate([xk, pe_ref[j:j + 1, :], pe_ref[CMP_STRIDE + j:CMP_STRIDE + j + 1, :],
                              jnp.zeros((SUBLANES - 2, gw), F32)], axis=0).astype(BF16)
        acc = acc + jnp.dot(xj, wab_ref[j], preferred_element_type=F32)
    bias = acc[c:c + 1, 0:gw] + acc[c + 1:c + 2, gw:2 * gw]
    ab_s[0:c, :] = acc[0:c]
    ab_s[c:c + SUBLANES, :] = jnp.zeros((SUBLANES, 2 * gw), F32)
    out = _compress_tail(ab_s, bias, w2_ref, add_pos)
    o_ref[0] = (out if add_pos else out.T).astype(o_ref.dtype)


def _compress_weights(w1, w2, pe, groups):
    d = w2.shape[0]
    w1j = w1.reshape(CMP_LEN, d, d)
    eye = jnp.eye(groups, dtype=F32)
    bd = (eye[None, :, None, :, None] * w1j[:, None, :, None, :]).reshape(CMP_LEN, groups * d, groups * d)
    wab = jnp.concatenate([bd[:CMP_STRIDE], bd[CMP_STRIDE:]], axis=2).astype(BF16)
    pe_t = jnp.tile(pe, (1, groups))
    w2p = jnp.concatenate([w2, jnp.zeros((d, LANES - d), F32)], axis=1)
    w2bd = (eye[:, None, :, None] * w2p[None, :, None, :]).reshape(groups * d, groups * LANES).astype(BF16)
    return wab, pe_t, w2bd


def nsa_compress(k, wab, pe_t, w2bd, add_pos):
    n, t, gw = k.shape
    c = t // CMP_STRIDE
    n_k = gw // LANES
    full = lambda shp: pl.BlockSpec(shp, lambda b: (0,) * len(shp))
    out_dims = (c, w2bd.shape[1]) if add_pos else (w2bd.shape[1], c)
    return pl.pallas_call(
        functools.partial(_compress_body, add_pos, n_k),
        grid=(n,),
        in_specs=[pl.BlockSpec((1, t, LANES), functools.partial(lambda u, b: (b, 0, u), u)) for u in range(n_k)]
        + [full(wab.shape), full(pe_t.shape), full(w2bd.shape)],
        out_specs=pl.BlockSpec((1,) + out_dims, lambda b: (b, 0, 0)),
        out_shape=jax.ShapeDtypeStruct((n,) + out_dims, BF16),
        scratch_shapes=[pltpu.VMEM((c + SUBLANES, 2 * gw), F32)],
        compiler_params=_cparams(("parallel",)),
        name="nsa_compress",
    )(*([k] * n_k), wab, pe_t, w2bd)


def _select_blocks_t(imp, qb, blk):
    vis = blk <= qb
    forced = (blk == 0) | (blk == qb) | (blk == qb - 1)
    score = jnp.where(vis & jnp.logical_not(forced), imp, -jnp.inf)
    blk_f = blk.astype(F32)
    sel = forced
    for _ in range(SEL_TOPK - 3):
        m = jnp.max(score, axis=0, keepdims=True)
        first = jnp.min(jnp.where(score == m, blk_f, jnp.inf), axis=0, keepdims=True)
        pick = blk_f == first
        sel = sel | pick
        score = jnp.where(pick, -jnp.inf, score)
    return sel & vis


def _select_blocks(imp, qb, lane):
    vis = lane <= qb
    forced = (lane == 0) | (lane == qb) | (lane == qb - 1)
    score = jnp.where(vis & jnp.logical_not(forced), imp, -jnp.inf)
    lane_f = lane.astype(F32)
    sel = forced
    for _ in range(SEL_TOPK - 3):
        pick = lane_f == _first_argmax(score, lane_f)
        sel = sel | pick
        score = jnp.where(pick, -jnp.inf, score)
    return sel & vis


def _nsa_prompt_body(slopes_ref, q_ref, gt_ref, ck_ref, cvt_ref, covert_ref, ks_ref, vst_ref, kw_ref, vwt_ref,
                     o_ref):
    g = pl.program_id(1)
    i = pl.program_id(2)
    tq = q_ref.shape[1]
    r_heads = q_ref.shape[2] // HEAD_DIM
    cols = r_heads * tq
    t0 = i * tq
    frow = lax.broadcasted_iota(jnp.int32, (LANES, tq), 0)
    qt_all = q_ref[0].T
    q_cols = []
    for r in range(r_heads):
        slope = slopes_ref[r_heads * g + r]
        s_hi, s_lo = _split_hi_lo(jnp.full((LANES, tq), slope, F32))
        feat = jnp.where(((frow - _POS_LANE) % 2) == 1, s_lo.astype(F32), s_hi.astype(F32))
        qr = jnp.concatenate([qt_all[r * HEAD_DIM:(r + 1) * HEAD_DIM] * _QSCALE,
                              jnp.zeros((LANES - HEAD_DIM, tq), F32)], axis=0)
        q_cols.append(jnp.where(frow < HEAD_DIM, qr,
                                jnp.where(frow < _POS_LANE + 4, feat,
                                          jnp.where(frow == _TILE_BIAS_LANE, 1.0, 0.0))).astype(BF16))
    qt = jnp.concatenate(q_cols, axis=1)
    qoff = lax.broadcasted_iota(jnp.int32, (1, cols), 1) % tq

    ck = ck_ref[0]
    n_cmp = ck.shape[0]
    s_c = jnp.dot(ck, qt, preferred_element_type=F32)
    cend = lax.broadcasted_iota(jnp.int32, (n_cmp, cols), 0) * CMP_STRIDE + (CMP_LEN - 1)
    ok_c = cend <= t0 + qoff
    s_c = jnp.where(ok_c, s_c, NEG_BIG)
    e_c = jnp.where(ok_c, jnp.exp2(s_c - jnp.max(s_c, axis=0, keepdims=True)), 0.0)
    p_c = e_c / jnp.maximum(jnp.sum(e_c, axis=0, keepdims=True), 1e-30)
    o_c = jnp.dot(cvt_ref[0], p_c.astype(BF16), preferred_element_type=F32)

    psum = p_c[:, 0:tq]
    for r in range(1, r_heads):
        psum = psum + p_c[:, r * tq:(r + 1) * tq]
    imp = jnp.dot(covert_ref[...], psum, precision=HIGHEST, preferred_element_type=F32)
    qb = (t0 + lax.broadcasted_iota(jnp.int32, (LANES, tq), 1)) // SEL_BLOCK
    pen = jnp.where(_select_blocks_t(imp, qb, frow), 0.0, NEG_BIG).astype(BF16)
    qs = jnp.concatenate([qt, jnp.concatenate([pen] * r_heads, axis=1)], axis=0)

    tk = tq
    klane = lax.broadcasted_iota(jnp.int32, (tk, LANES), 1)
    krow = lax.broadcasted_iota(jnp.int32, (tk, LANES), 0)
    krow_b = krow.astype(F32).astype(BF16)
    is_pos = (klane == _POS_LANE) | (klane == _POS_LANE + 1)
    is_off = (klane == _POS_LANE + 2) | (klane == _POS_LANE + 3)
    blocks_per_tile = tk // SEL_BLOCK
    vrow = lax.broadcasted_iota(jnp.int32, (LANES, tk), 0)
    one_b = jnp.ones((LANES, tk), BF16)
    key_i = lax.broadcasted_iota(jnp.int32, (tk, cols), 0)

    def keys_with_pos(k_ref, j):
        kt = k_ref[0, pl.ds(pl.multiple_of(j * tk, tk), tk), :]
        off_b = jnp.full((tk, LANES), ((j - i) * tk).astype(F32), F32).astype(BF16)
        return jnp.where(is_pos, krow_b, jnp.where(is_off, off_b, kt))

    def values_with_ones(vt_ref, j):
        return jnp.where(vrow == _POS_LANE, one_b, vt_ref[0, :, pl.ds(pl.multiple_of(j * tk, tk), tk)])

    def sel_scores(j):
        onehot = jnp.where(klane == j * blocks_per_tile + krow // SEL_BLOCK, 1.0, 0.0).astype(BF16)
        return jnp.dot(jnp.concatenate([keys_with_pos(ks_ref, j), onehot], axis=1), qs,
                       preferred_element_type=F32)

    def update(s, v, carry, mask=None):
        m, acc = carry
        if mask is not None:
            s = jnp.where(mask, s, NEG_BIG)
        m_new = jnp.maximum(m, jnp.max(s, axis=0, keepdims=True))
        p = jnp.exp2(s - m_new).astype(BF16)
        return m_new, jnp.exp2(m - m_new) * acc + jnp.dot(v, p, preferred_element_type=F32)

    def sel_body(j, c):
        s, carry = c
        return sel_scores(j + 1), update(s, values_with_ones(vst_ref, j), carry)

    def finish(acc):
        return acc / acc[_POS_LANE:_POS_LANE + 1, :]

    init = (jnp.full((1, cols), -jnp.inf, F32), jnp.zeros((LANES, cols), F32))
    s, carry = lax.fori_loop(0, i, sel_body, (sel_scores(0), init))
    o_s = finish(update(s, values_with_ones(vst_ref, i), carry, key_i <= qoff)[1])

    n_win = WINDOW // tk + 1
    kws, vws = [], []
    for w in range(n_win):
        j_true = i - (n_win - 1) + w
        j = jnp.maximum(j_true, 0)
        tile_bias = jnp.full((tk, LANES), jnp.where(j_true >= 0, 0.0, NEG_BIG), F32).astype(BF16)
        kws.append(jnp.where(klane == _TILE_BIAS_LANE, tile_bias, keys_with_pos(kw_ref, j)))
        vws.append(values_with_ones(vwt_ref, j))
    s_w = jnp.dot(jnp.concatenate(kws, axis=0), qt, preferred_element_type=F32)
    newest = (n_win - 1) * tk
    s_w = jnp.concatenate([jnp.where(key_i > qoff, s_w[0:tk], NEG_BIG), s_w[tk:newest],
                           jnp.where(key_i <= qoff, s_w[newest:], NEG_BIG)], axis=0)
    p_w = jnp.exp2(s_w - jnp.max(s_w, axis=0, keepdims=True)).astype(BF16)
    o_w = finish(jnp.dot(jnp.concatenate(vws, axis=1), p_w, preferred_element_type=F32))

    gates = jax.nn.sigmoid(gt_ref[0]).T
    outs = []
    for r in range(r_heads):
        cs = slice(r * tq, (r + 1) * tq)
        gr = [gates[3 * r + b:3 * r + b + 1, :] for b in range(3)]
        outs.append((gr[0] * o_c[:, cs] + gr[1] * o_s[:, cs] + gr[2] * o_w[:, cs])[0:HEAD_DIM])
    o_ref[0] = jnp.concatenate(outs, axis=0).T


def nsa_prompt(q, gt, ck, cvt, cover_t, ks, vst, kw, vwt, slopes):
    n, t, w = q.shape
    groups = ks.shape[2] // LANES
    qw = w // groups
    tq = 256
    c = ck.shape[1]
    assert t % tq == 0 and -(-t // SEL_BLOCK) <= LANES and t // SEL_BLOCK >= SEL_TOPK and WINDOW % tq == 0
    rows_spec = lambda rows: pl.BlockSpec((1, rows, LANES), lambda b, g, i: (b, 0, g))
    cols_spec = lambda cols: pl.BlockSpec((1, LANES, cols), lambda b, g, i: (b, g, 0))
    grid_spec = pltpu.PrefetchScalarGridSpec(
        num_scalar_prefetch=0,
        grid=(n, groups, t // tq),
        in_specs=[pl.BlockSpec(memory_space=pltpu.SMEM),
                  pl.BlockSpec((1, tq, qw), lambda b, g, i: (b, i, g)),
                  pl.BlockSpec((1, tq, LANES), lambda b, g, i: (b, i, g)),
                  rows_spec(c), cols_spec(c),
                  pl.BlockSpec(cover_t.shape, lambda b, g, i: (0, 0)),
                  rows_spec(t), cols_spec(t), rows_spec(t), cols_spec(t)],
        out_specs=pl.BlockSpec((1, tq, qw), lambda b, g, i: (b, i, g)),
    )
    return pl.pallas_call(
        _nsa_prompt_body,
        grid_spec=grid_spec,
        out_shape=jax.ShapeDtypeStruct((n, t, w), F32),
        compiler_params=_cparams(("parallel", "parallel", "arbitrary")),
        name="nsa_prompt",
    )(jnp.asarray(slopes * np.float32(LOG2E)), q, gt, ck, cvt, cover_t, ks, vst, kw, vwt)


def _nsa_cover(n_cmp_rows, n_sel):
    i = np.arange(n_cmp_rows)[:, None]
    j = np.arange(-(-n_sel // LANES) * LANES)[None, :]
    cov = (CMP_STRIDE * i < SEL_BLOCK * (j + 1)) & (CMP_STRIDE * i + CMP_LEN > SEL_BLOCK * j) & (j < n_sel)
    return jnp.asarray(cov.astype(np.float32))


def _pad_group_cols(w, width):
    d, gw = w.shape
    g = gw // width
    return jnp.pad(w.reshape(d, g, width), ((0, 0), (0, 0), (0, LANES - width))).reshape(d, g * LANES)


def _own_head_block(rows, width, nq):
    r = lax.broadcasted_iota(jnp.int32, (rows, width), 0)
    c = lax.broadcasted_iota(jnp.int32, (rows, width), 1)
    return (c // HEAD_DIM) == (r // nq)


def _fold_heads(x, own):
    width = x.shape[1]
    fold = jnp.where(lax.broadcasted_iota(jnp.int32, (width, HEAD_DIM), 0) % HEAD_DIM
                     == lax.broadcasted_iota(jnp.int32, (width, HEAD_DIM), 1), 1.0, 0.0)
    return jnp.dot(jnp.where(own, x, 0.0), fold, precision=HIGHEST, preferred_element_type=F32)


_MOBA_BLOCKS_PER_STEP = 8


def _moba_decode_body(nblk, past, pt_ref, q_ref, slope_ref, kn_ref, vn_ref, *refs):
    del pt_ref
    n_page_refs = _MOBA_BLOCKS_PER_STEP * (MOBA_BLOCK // PAGE_SIZE)
    k_refs, v_refs = refs[:n_page_refs], refs[n_page_refs:2 * n_page_refs]
    o_ref, m_s, l_s, g_s, acc_s = refs[2 * n_page_refs:]
    j = pl.program_id(1)
    rows, width = q_ref.shape[1], q_ref.shape[2]
    nq = rows // (width // HEAD_DIM)
    own = _own_head_block(rows, width, nq)
    q_hi, q_lo = _split_hi_lo(jnp.where(own, q_ref[0] * HEAD_DIM ** -0.5, 0.0))
    slope = slope_ref[...]
    lane = lax.broadcasted_iota(jnp.int32, (rows, LANES), 1)
    lane_f = lane.astype(F32)

    @pl.when(j == 0)
    def _():
        m_s[...] = jnp.full(m_s.shape, NEG_BIG, F32)
        l_s[...] = jnp.zeros(l_s.shape, F32)
        g_s[...] = jnp.zeros(g_s.shape, F32)

    lhs = jnp.concatenate([q_hi, q_lo], axis=0)
    pages_per_blk = MOBA_BLOCK // PAGE_SIZE
    span = jnp.concatenate([lane_f + pg * PAGE_SIZE for pg in range(pages_per_blk)], axis=1)
    for bk in range(_MOBA_BLOCKS_PER_STEP):
        jb = j * _MOBA_BLOCKS_PER_STEP + bk
        pages = slice(bk * pages_per_blk, (bk + 1) * pages_per_blk)
        kt = jnp.concatenate([r[0, 0].reshape(width, PAGE_SIZE) for r in k_refs[pages]], axis=1).astype(BF16)
        vt = jnp.concatenate([r[0, 0].reshape(width, PAGE_SIZE) for r in v_refs[pages]], axis=1).astype(BF16)
        s2 = jnp.dot(lhs, kt, preferred_element_type=F32)
        gate = jnp.sum(s2[:rows] + s2[rows:], axis=-1, keepdims=True)
        s = s2[:rows] + slope * (span + (jb * MOBA_BLOCK - past).astype(F32))
        m_j = jnp.max(s, axis=-1, keepdims=True)
        p = jnp.exp(s - m_j)
        m_s[...] = jnp.where(lane == jb, m_j, m_s[...])
        l_s[...] = jnp.where(lane == jb, jnp.sum(p, axis=-1, keepdims=True), l_s[...])
        g_s[...] = jnp.where(lane == jb, gate, g_s[...])
        acc_s[jb] = _dot_nt(p.astype(BF16), vt)

    @pl.when(j == nblk // _MOBA_BLOCKS_PER_STEP - 1)
    def _():
        g = jnp.where(lane < nblk, g_s[...], -jnp.inf)
        sel = jnp.zeros(g.shape, jnp.bool_)
        for _ in range(MOBA_TOPK):
            pick = lane_f == _first_argmax(g, lane_f)
            sel = sel | pick
            g = jnp.where(pick, -jnp.inf, g)
        sel = sel & (lane < nblk)
        ok_o = (lane < nq) & (lane <= lax.broadcasted_iota(jnp.int32, (rows, LANES), 0) % nq)
        s_o = jnp.dot(q_hi, kn_ref[0].astype(BF16), preferred_element_type=F32) + slope * lane_f
        s_o = jnp.where(ok_o, s_o, NEG_BIG)
        m_o = jnp.max(s_o, axis=-1, keepdims=True)
        p_o = jnp.exp(s_o - m_o)
        l_o = jnp.sum(p_o, axis=-1, keepdims=True)
        acc_o = _dot_nt(p_o.astype(BF16), vn_ref[0].astype(BF16))
        m_all = jnp.maximum(jnp.max(jnp.where(sel, m_s[...], -jnp.inf), axis=-1, keepdims=True), m_o)
        w = jnp.where(sel, jnp.exp(m_s[...] - m_all), 0.0)
        w_o = jnp.exp(m_o - m_all)
        den = jnp.sum(w * l_s[...], axis=-1, keepdims=True) + w_o * l_o
        num = w_o * acc_o
        for jj in range(nblk):
            num = num + w[:, jj:jj + 1] * acc_s[jj]
        o_ref[0] = _fold_heads(num, own) / den


def moba_decode(q_tiled, kn_t, vn_t, cache_k, cache_v, page_table, layer, slopes):
    b, rows, width = q_tiled.shape
    heads = cache_k.shape[2]
    nq = rows // heads
    n_pages = page_table.shape[1]
    past = n_pages * PAGE_SIZE
    pages_per_blk = MOBA_BLOCK // PAGE_SIZE
    nblk = past // MOBA_BLOCK
    pages_per_step = _MOBA_BLOCKS_PER_STEP * pages_per_blk
    assert past % MOBA_BLOCK == 0 and MOBA_TOPK <= nblk <= LANES and nblk % _MOBA_BLOCKS_PER_STEP == 0
    slope_rows = jnp.asarray(np.repeat(slopes, nq).reshape(rows, 1))
    page = lambda pg: pl.BlockSpec((1, 1, heads, HEAD_DIM, PAGE_SIZE),
                                   lambda s, j, pt: (layer, pt[s, pages_per_step * j + pg], 0, 0, 0))
    pages = [page(pg) for pg in range(pages_per_step)]
    per_seq = lambda a: pl.BlockSpec((1,) + a.shape[1:], lambda s, j, pt: (s, 0, 0))
    grid_spec = pltpu.PrefetchScalarGridSpec(
        num_scalar_prefetch=1,
        grid=(b, nblk // _MOBA_BLOCKS_PER_STEP),
        in_specs=[per_seq(q_tiled), pl.BlockSpec((rows, 1), lambda s, j, pt: (0, 0)), per_seq(kn_t), per_seq(vn_t)]
        + pages + pages,
        out_specs=pl.BlockSpec((1, rows, HEAD_DIM), lambda s, j, pt: (s, 0, 0)),
        scratch_shapes=[pltpu.VMEM((rows, LANES), F32), pltpu.VMEM((rows, LANES), F32),
                        pltpu.VMEM((rows, LANES), F32), pltpu.VMEM((nblk, rows, width), F32)],
    )
    return pl.pallas_call(
        functools.partial(_moba_decode_body, nblk, past),
        grid_spec=grid_spec,
        out_shape=jax.ShapeDtypeStruct((b, rows, HEAD_DIM), F32),
        compiler_params=_cparams(("parallel", "arbitrary")),
        name="moba_decode",
    )(page_table, q_tiled, slope_rows, kn_t, vn_t, *([cache_k] * pages_per_step), *([cache_v] * pages_per_step))


_CHUNKS_PER_PAGE = PAGE_SIZE // CMP_STRIDE


def _page_ab_body(k_ref, wj_ref, o_ref, xs):
    pages, groups = k_ref.shape[1], k_ref.shape[2]
    for gp in range(groups // 2):
        both = jnp.concatenate([k_ref[0, :, 2 * gp], k_ref[0, :, 2 * gp + 1]], axis=1)
        xs[...] = jnp.swapaxes(both, 1, 2)
        acc = jnp.zeros((pages * _CHUNKS_PER_PAGE, 2 * LANES), F32)
        for j in range(CMP_STRIDE):
            x = xs[:, pl.ds(j, _CHUNKS_PER_PAGE, stride=CMP_STRIDE), :]
            x = x.reshape(pages * _CHUNKS_PER_PAGE, 2 * HEAD_DIM).astype(BF16)
            acc = acc + jnp.dot(x, wj_ref[j], preferred_element_type=F32)
        o_ref[:, 2 * gp * LANES:2 * (gp + 1) * LANES] = acc


def nsa_page_ab(cache, layer, wj):
    _, n_phys, groups, _, _ = cache.shape
    assert groups % 2 == 0
    pages = next(p for p in (32, 16, 8, 4, 2, 1) if n_phys % p == 0)
    zero = jnp.zeros_like(wj)
    wj = jnp.concatenate([jnp.concatenate([wj, zero], axis=2), jnp.concatenate([zero, wj], axis=2)], axis=1)
    return pl.pallas_call(
        _page_ab_body,
        grid=(n_phys // pages,),
        in_specs=[pl.BlockSpec((1, pages, groups, HEAD_DIM, PAGE_SIZE), lambda i: (layer, i, 0, 0, 0)),
                  pl.BlockSpec(wj.shape, lambda i: (0, 0, 0))],
        out_specs=pl.BlockSpec((pages * _CHUNKS_PER_PAGE, groups * LANES), lambda i: (i, 0)),
        out_shape=jax.ShapeDtypeStruct((n_phys * _CHUNKS_PER_PAGE, groups * LANES), F32),
        scratch_shapes=[pltpu.VMEM((pages, PAGE_SIZE, 2 * HEAD_DIM), F32)],
        compiler_params=_cparams(("parallel",)),
        name="nsa_page_ab",
    )(cache, wj)


def _page_ab_copy(ab_hbm, pt_ref, s, p, buf, sem):
    return pltpu.make_async_copy(ab_hbm.at[pt_ref[s, p]], buf.at[pl.ds(p * _CHUNKS_PER_PAGE, _CHUNKS_PER_PAGE), :], sem)


def _decode_queries(q_ref, slope_ref):
    q = q_ref[0]
    lane = lax.broadcasted_iota(jnp.int32, q.shape, 1)
    s_hi, s_lo = _split_hi_lo(jnp.broadcast_to(slope_ref[...], q.shape))
    feat = jnp.where(((lane - _POS_LANE) % 2) == 1, s_lo.astype(F32), s_hi.astype(F32))
    return jnp.where(lane < HEAD_DIM, q * HEAD_DIM ** -0.5,
                     jnp.where(lane < _POS_LANE + 4, feat, 0.0)).astype(BF16)


def _nsa_decode_cmp_body(n_pages, past, pt_ref, q_ref, slope_ref, abk_ref, abv_ref, wjk_ref, wjv_ref, pek_ref,
                         pev_ref, w2k_ref, w2v_ref, cover_ref, oc_ref, pen_ref, bufk, bufv, sem):
    s = pl.program_id(0)
    c = n_pages * _CHUNKS_PER_PAGE
    rows = q_ref.shape[1]
    groups = abk_ref.shape[2] // LANES
    rpg = rows // groups
    nq = pen_ref.shape[2] // 2

    def issue(p, _):
        _page_ab_copy(abk_ref, pt_ref, s, p, bufk, sem.at[0]).start()
        _page_ab_copy(abv_ref, pt_ref, s, p, bufv, sem.at[1]).start()
        return 0

    def drain(p, _):
        _page_ab_copy(abk_ref, pt_ref, s, p, bufk, sem.at[0]).wait()
        _page_ab_copy(abv_ref, pt_ref, s, p, bufv, sem.at[1]).wait()
        return 0

    lax.fori_loop(0, n_pages, issue, 0)
    pad = jnp.zeros((SUBLANES, bufk.shape[1]), F32)
    bufk[c:c + SUBLANES, :] = pad
    bufv[c:c + SUBLANES, :] = pad
    lax.fori_loop(0, n_pages, drain, 0)

    def tokens(buf, wj_ref, pe_ref, w2_ref, g, add_pos):
        acc = jnp.zeros((SUBLANES, LANES), F32)
        for j in range(CMP_STRIDE):
            xj = jnp.concatenate([pe_ref[j:j + 1, :], pe_ref[CMP_STRIDE + j:CMP_STRIDE + j + 1, :],
                                  jnp.zeros((SUBLANES - 2, HEAD_DIM), F32)], axis=0).astype(BF16)
            acc = acc + jnp.dot(xj, wj_ref[j], preferred_element_type=F32)
        bias = acc[0:1] + pltpu.roll(acc, HEAD_DIM, 1)[1:2]
        slab = slice(g * LANES, (g + 1) * LANES)
        h = buf[0:c, slab] + pltpu.roll(buf[1:c + 1, slab], HEAD_DIM, 1) + bias
        out = jnp.dot(jax.nn.gelu(h).astype(BF16), w2_ref[...], preferred_element_type=F32)
        if add_pos:
            lane = lax.broadcasted_iota(jnp.int32, out.shape, 1)
            cend = lax.broadcasted_iota(jnp.int32, out.shape, 0) * CMP_STRIDE + (CMP_LEN - 1)
            hi = ((cend // 256) * 256).astype(F32)
            lo = (cend % 256).astype(F32)
            out = jnp.where((lane == _POS_LANE) | (lane == _POS_LANE + 1), hi,
                            jnp.where((lane == _POS_LANE + 2) | (lane == _POS_LANE + 3), lo, out))
        return out.astype(BF16)

    qa = _decode_queries(q_ref, slope_ref)
    for g in range(groups):
        ck = tokens(bufk, wjk_ref, pek_ref, w2k_ref, g, True)
        cv = tokens(bufv, wjv_ref, pev_ref, w2v_ref, g, False)
        s_c = _dot_nt(qa[g * rpg:(g + 1) * rpg], ck)
        cend = lax.broadcasted_iota(jnp.int32, s_c.shape, 1) * CMP_STRIDE + (CMP_LEN - 1)
        t_q = past + lax.broadcasted_iota(jnp.int32, s_c.shape, 0) % nq
        ok = cend <= t_q
        s_c = jnp.where(ok, s_c, NEG_BIG)
        e_c = jnp.where(ok, jnp.exp(s_c - jnp.max(s_c, axis=-1, keepdims=True)), 0.0)
        p_c = e_c / jnp.maximum(jnp.sum(e_c, axis=-1, keepdims=True), 1e-30)
        oc_ref[0, g * rpg:(g + 1) * rpg, :] = jnp.dot(p_c.astype(BF16), cv, preferred_element_type=F32)
        half = p_c[0:2 * nq]
        for u in range(1, rpg // (2 * nq)):
            half = half + p_c[2 * nq * u:2 * nq * (u + 1)]
        psum = half + pltpu.roll(half, nq, 0)
        imp = jnp.dot(psum, cover_ref[...], precision=HIGHEST, preferred_element_type=F32)
        lane = lax.broadcasted_iota(jnp.int32, imp.shape, 1)
        qb = (past + lax.broadcasted_iota(jnp.int32, imp.shape, 0) % nq) // SEL_BLOCK
        pen_ref[0, g] = jnp.where(_select_blocks(imp, qb, lane), 0.0, NEG_BIG)


def nsa_decode_cmp(q_rows, slope_rows, ab_k, ab_v, page_table, wjk, wjv, pek, pev, w2k, w2v, cover, nq):
    b, rows, _ = q_rows.shape
    n_pages = page_table.shape[1]
    past = n_pages * PAGE_SIZE
    groups = ab_k.shape[1] // LANES
    c = n_pages * _CHUNKS_PER_PAGE
    assert past % CMP_STRIDE == 0 and nq < CMP_STRIDE and 2 * nq == SUBLANES
    ab_k3 = ab_k.reshape(-1, _CHUNKS_PER_PAGE, groups * LANES)
    ab_v3 = ab_v.reshape(-1, _CHUNKS_PER_PAGE, groups * LANES)
    full = lambda a: pl.BlockSpec(a.shape, lambda s, pt: (0,) * a.ndim)
    grid_spec = pltpu.PrefetchScalarGridSpec(
        num_scalar_prefetch=1,
        grid=(b,),
        in_specs=[pl.BlockSpec((1, rows, LANES), lambda s, pt: (s, 0, 0)), full(slope_rows),
                  pl.BlockSpec(memory_space=pl.ANY), pl.BlockSpec(memory_space=pl.ANY),
                  full(wjk), full(wjv), full(pek), full(pev), full(w2k), full(w2v), full(cover)],
        out_specs=[pl.BlockSpec((1, rows, LANES), lambda s, pt: (s, 0, 0)),
                   pl.BlockSpec((1, groups, 2 * nq, cover.shape[1]), lambda s, pt: (s, 0, 0, 0))],
        scratch_shapes=[pltpu.VMEM((c + SUBLANES, groups * LANES), F32),
                        pltpu.VMEM((c + SUBLANES, groups * LANES), F32),
                        pltpu.SemaphoreType.DMA((2,))],
    )
    return pl.pallas_call(
        functools.partial(_nsa_decode_cmp_body, n_pages, past),
        grid_spec=grid_spec,
        out_shape=[jax.ShapeDtypeStruct((b, rows, LANES), F32),
                   jax.ShapeDtypeStruct((b, groups, 2 * nq, cover.shape[1]), F32)],
        compiler_params=_cparams(("arbitrary",)),
        name="nsa_decode_cmp",
    )(page_table, q_rows, slope_rows, ab_k3, ab_v3, wjk, wjv, pek, pev, w2k, w2v, cover)


_SEL_PAGES_PER_STEP = 16


def _nsa_decode_sel_body(n_steps, past, wbuf, pt_ref, q_ref, slope_ref, pen_ref, oc_ref, gt_ref, ksn_ref, vsn_ref,
                         kwn_ref, vwn_ref, wk_ref, wv_ref, *refs):
    del pt_ref
    pp = _SEL_PAGES_PER_STEP
    k_refs, v_refs = refs[:pp], refs[pp:2 * pp]
    o_ref, m_s, l_s, acc_s = refs[2 * pp:]
    step = pl.program_id(1)
    rows, width = q_ref.shape[1], q_ref.shape[2]
    groups = pen_ref.shape[1]
    rpg = rows // groups
    nq = pen_ref.shape[2] // 2
    n_sel = pen_ref.shape[3]
    own = _own_head_block(rows, width, rpg)
    qb = jnp.where(own, q_ref[0] * HEAD_DIM ** -0.5, 0.0).astype(BF16)
    slope = slope_ref[...]
    blocks_per_page = PAGE_SIZE // SEL_BLOCK
    pen_rows = jnp.concatenate([pen_ref[0, g] for g in range(groups) for _ in range(rpg // (2 * nq))],
                               axis=0).astype(BF16)

    @pl.when(step == 0)
    def _():
        m_s[...] = jnp.full(m_s.shape, -jnp.inf, F32)
        l_s[...] = jnp.zeros(l_s.shape, F32)
        acc_s[...] = jnp.zeros(acc_s.shape, F32)

    def attend(carry, kt, vt, bias, mask=None):
        m, l, acc = carry
        s = jnp.dot(qb, kt, preferred_element_type=F32) + bias
        if mask is not None:
            s = jnp.where(mask, s, NEG_BIG)
        m_new = jnp.maximum(m, jnp.max(s, axis=-1, keepdims=True))
        p = jnp.exp(s - m_new)
        alpha = jnp.exp(m - m_new)
        return m_new, alpha * l + jnp.sum(p, axis=-1, keepdims=True), alpha * acc + _dot_nt(p.astype(BF16), vt)

    col = lax.broadcasted_iota(jnp.int32, (rows, PAGE_SIZE), 1)
    col_f = col.astype(F32)
    span = pp * PAGE_SIZE
    e_row = lax.broadcasted_iota(jnp.int32, (n_sel, span), 0)
    e_col = lax.broadcasted_iota(jnp.int32, (n_sel, span), 1)
    expand = jnp.where(e_row == step * (pp * blocks_per_page) + e_col // SEL_BLOCK, 1.0, 0.0).astype(BF16)
    pos = lax.broadcasted_iota(jnp.int32, (rows, span), 1).astype(F32) + (step * span - past).astype(F32)
    bias = jnp.dot(pen_rows, expand, preferred_element_type=F32) + slope * pos
    kt = jnp.concatenate([r[0, 0].reshape(width, PAGE_SIZE) for r in k_refs], axis=1).astype(BF16)
    vt = jnp.concatenate([r[0, 0].reshape(width, PAGE_SIZE) for r in v_refs], axis=1).astype(BF16)
    m_s[...], l_s[...], acc_s[...] = attend((m_s[...], l_s[...], acc_s[...]), kt, vt, bias)

    @pl.when(step == n_steps - 1)
    def _():
        gates = jax.nn.sigmoid(gt_ref[0])
        qi = lax.broadcasted_iota(jnp.int32, (rows, PAGE_SIZE), 0) % nq
        ok_n = (col <= qi) & (col < nq)
        pen_new = pen_rows[:, past // SEL_BLOCK:past // SEL_BLOCK + 1].astype(F32)
        _, l_sel, acc_sel = attend((m_s[...], l_s[...], acc_s[...]), ksn_ref[0].astype(BF16),
                                   vsn_ref[0].astype(BF16), slope * col_f + pen_new, ok_n)
        o_sel = _fold_heads(acc_sel, own) / l_sel
        wqi = lax.broadcasted_iota(jnp.int32, (rows, wbuf), 0) % nq
        wc = lax.broadcasted_iota(jnp.int32, (rows, wbuf), 1)
        dist_w = wqi + wbuf - wc
        ok_w = (dist_w >= 0) & (dist_w < WINDOW) & (past - wbuf + wc >= 0)
        init = (jnp.full((rows, 1), -jnp.inf, F32), jnp.zeros((rows, 1), F32), jnp.zeros((rows, width), F32))
        carry_w = attend(init, wk_ref[0, 0].reshape(width, wbuf).astype(BF16),
                         wv_ref[0, 0].reshape(width, wbuf).astype(BF16), slope * (wc - wbuf).astype(F32), ok_w)
        _, l_win, acc_win = attend(carry_w, kwn_ref[0].astype(BF16), vwn_ref[0].astype(BF16), slope * col_f, ok_n)
        o_win = _fold_heads(acc_win, own) / l_win
        o_ref[0] = (gates[:, 0:1] * oc_ref[0, :, 0:HEAD_DIM] + gates[:, 1:2] * o_sel + gates[:, 2:3] * o_win)


def nsa_decode_sel(q_tiled, slope_rows, pen, o_c, gt_rows, ks_new, vs_new, kw_new, vw_new, win_k, win_v,
                   cache_k, cache_v, page_table, layer):
    b, rows, _ = q_tiled.shape
    groups = cache_k.shape[2]
    n_pages = page_table.shape[1]
    past = n_pages * PAGE_SIZE
    pp = _SEL_PAGES_PER_STEP
    wbuf = win_k.shape[4]
    assert n_pages % pp == 0 and past % SEL_BLOCK == 0
    n_steps = n_pages // pp
    per_seq = lambda a: pl.BlockSpec((1,) + a.shape[1:], lambda s, j, pt: (s,) + (0,) * (a.ndim - 1))
    page = lambda u: pl.BlockSpec((1, 1, groups, HEAD_DIM, PAGE_SIZE),
                                  lambda s, j, pt: (layer, pt[s, pp * j + u], 0, 0, 0))
    win = pl.BlockSpec((1, 1, groups, HEAD_DIM, wbuf), lambda s, j, pt: (layer, s, 0, 0, 0))
    grid_spec = pltpu.PrefetchScalarGridSpec(
        num_scalar_prefetch=1,
        grid=(b, n_steps),
        in_specs=[per_seq(q_tiled), pl.BlockSpec(slope_rows.shape, lambda s, j, pt: (0, 0)),
                  per_seq(pen), per_seq(o_c), per_seq(gt_rows),
                  per_seq(ks_new), per_seq(vs_new), per_seq(kw_new), per_seq(vw_new), win, win]
        + [page(u) for u in range(pp)] * 2,
        out_specs=pl.BlockSpec((1, rows, HEAD_DIM), lambda s, j, pt: (s, 0, 0)),
        scratch_shapes=[pltpu.VMEM((rows, 1), F32), pltpu.VMEM((rows, 1), F32),
                        pltpu.VMEM((rows, groups * HEAD_DIM), F32)],
    )
    return pl.pallas_call(
        functools.partial(_nsa_decode_sel_body, n_steps, past, wbuf),
        grid_spec=grid_spec,
        out_shape=jax.ShapeDtypeStruct((b, rows, HEAD_DIM), F32),
        compiler_params=_cparams(("parallel", "arbitrary")),
        name="nsa_decode_sel",
    )(page_table, q_tiled, slope_rows, pen, o_c, gt_rows, ks_new, vs_new, kw_new, vw_new, win_k, win_v,
      *([cache_k] * pp), *([cache_v] * pp))


def _heads_to_rows(x, n, t, heads):
    return x.reshape(n, t, heads, HEAD_DIM).transpose(0, 2, 1, 3).reshape(n, heads * t, HEAD_DIM)


def _rows_to_heads(x, n, t, heads):
    return x.reshape(n, heads, t, HEAD_DIM).transpose(0, 2, 1, 3).reshape(n * t, heads * HEAD_DIM)


def _new_kv_cols(x, n, t):
    return jnp.pad(x.reshape(n, t, x.shape[-1]).transpose(0, 2, 1), ((0, 0), (0, 0), (0, PAGE_SIZE - t)))


def _heads_last(a, heads):
    n, _, t = a.shape
    return a.reshape(n, heads, HEAD_DIM, t).transpose(0, 3, 1, 2)


def _tokens_minor(a):
    return a.transpose(0, 1, 3, 4, 2)


def _layer_ab(layer, l, yp, ys, dims, p):
    bp, tp, bs, ts = dims
    d_rnn = p["conv_w"].shape[2]
    heads = (p["w_in_ab"].shape[2] - 2 * d_rnn) // (3 * HEAD_DIM)
    hw = heads * HEAD_DIM
    w_in = p["w_in_ab"][l].astype(BF16)
    cuts = [0, d_rnn, 2 * d_rnn, 2 * d_rnn + hw, 2 * d_rnn + 2 * hw, 2 * d_rnn + 3 * hw]
    ws = [w_in[:, a:b] for a, b in zip(cuts[:-1], cuts[1:])]
    w_out = p["w_out_ab"][l].astype(BF16)
    w_out = [w_out[:d_rnn], w_out[d_rnn:]]
    rg = (p["conv_w"][l], p["conv_b"][l], _block_diag(p["w_rg_a"][l]).astype(BF16), p["b_rg_a"][l],
          _block_diag(p["w_rg_i"][l]).astype(BF16), p["b_rg_i"][l], p["rg_lambda"][l])
    slopes = _alibi_slopes(heads)
    g = p["norm_mix"][layer]
    f32x5 = [(i, F32) for i in range(5)]

    xr, gr, q, k, k_fm, v_fm, kb, vt = norm_proj(
        yp, g, ws, f32x5[:4] + [(3, F32, True), (4, F32, True), (3, BF16), (4, BF16, True)], seq_len=tp)
    seq = lambda a: a.reshape(bp, tp, a.shape[-1])
    r_p, conv_p, h_p = rglru(seq(xr), seq(gr), jnp.zeros((bp, CONV_W - 1, d_rnn), F32),
                             jnp.zeros((bp, d_rnn), F32), *rg)
    att = moba_prompt(seq(q), seq(kb), vt, block_mean(seq(k), MOBA_BLOCK), slopes)
    yp = proj_residual([r_p.reshape(bp * tp, d_rnn), att.reshape(bp * tp, hw)], w_out, yp)
    out_p = (_heads_last(k_fm, heads), _heads_last(v_fm, heads), conv_p, h_p)

    xr, gr, q, k, v = norm_proj(ys, g, ws, f32x5)
    seq = lambda a: a.reshape(bs, ts, a.shape[-1])
    r_s, conv_s, h_s = rglru(seq(xr), seq(gr), p["state_conv"][l], p["state_rglru"][l], *rg)
    att = moba_decode(jnp.tile(_heads_to_rows(q, bs, ts, heads), (1, 1, heads)), _new_kv_cols(k, bs, ts),
                      _new_kv_cols(v, bs, ts), _tokens_minor(p["cache_moba_k"]), _tokens_minor(p["cache_moba_v"]),
                      p["page_table"], l, slopes)
    ys = proj_residual([r_s.reshape(bs * ts, d_rnn), _rows_to_heads(att, bs, ts, heads)], w_out, ys)
    out_s = (k.reshape(bs, ts, heads, HEAD_DIM), v.reshape(bs, ts, heads, HEAD_DIM), conv_s, h_s)
    return yp, ys, out_p, out_s


def _layer_c(layer, l, yp, ys, dims, p):
    bp, tp, bs, ts = dims
    groups = p["cache_cmp_k"].shape[3]
    gw = groups * HEAD_DIM
    w_in = p["w_in_c"][l]
    heads = (w_in.shape[1] - 6 * gw) // (HEAD_DIM + 3)
    qw = heads * HEAD_DIM
    cuts = [0, qw] + [qw + (i + 1) * gw for i in range(6)]
    w_f32 = [w_in[:, a:b] for a, b in zip(cuts[:-1], cuts[1:])]
    w_gt = w_in[:, cuts[-1]:]
    slopes = _alibi_slopes(heads)
    g = p["norm_mix"][layer]
    w_out = [p["w_out_c"][l].astype(BF16)]
    cmp_k = (p["w_cmp_k1"][l], p["w_cmp_k2"][l], p["pe_cmp_k"][l])
    cmp_v = (p["w_cmp_v1"][l], p["w_cmp_v2"][l], p["pe_cmp_v"][l])
    kv4 = lambda a, n, t: a.reshape(n, t, groups, HEAD_DIM)

    ws = ([w.astype(BF16) for w in w_f32] + [_pad_group_cols(w_gt, 3 * heads // groups).astype(BF16)]
          + [_pad_group_cols(w, HEAD_DIM).astype(BF16) for w in w_f32[3:7]])
    outs = ([(0, F32), (1, F32), (2, F32), (7, F32)] + [(i, F32, True) for i in range(1, 7)]
            + [(8, BF16), (9, BF16, True), (10, BF16), (11, BF16, True)])
    q, kc, vc, gt, *kv_fm, ksp, vst, kwp, vwt = norm_proj(yp, g, ws, outs, seq_len=tp)
    seq = lambda a: a.reshape(bp, tp, a.shape[-1])
    ck = nsa_compress(seq(kc), *_compress_weights(*cmp_k, groups), True)
    cvt = nsa_compress(seq(vc), *_compress_weights(*cmp_v, groups), False)
    cover_t = _nsa_cover(tp // CMP_STRIDE, max(-(-tp // SEL_BLOCK), SEL_TOPK)).T
    att = nsa_prompt(seq(q), seq(gt), ck, cvt, cover_t, seq(ksp), vst, seq(kwp), vwt, slopes)
    yp = proj_residual([att.reshape(bp * tp, qw)], w_out, yp)
    w_keep = min(WINDOW, tp)
    out_p = (tuple(_heads_last(a, groups) for a in kv_fm[:4])
             + tuple(_heads_last(a[:, :, tp - w_keep:], groups) for a in kv_fm[4:]))

    ws = [w.astype(BF16) for w in w_f32] + [jnp.pad(w_gt, ((0, 0), (0, LANES - w_gt.shape[1]))).astype(BF16)]
    q, kc, vc, ks, vs, kw, vw, gt = norm_proj(ys, g, ws, [(i, F32) for i in range(8)])
    q_rows = _heads_to_rows(q, bs, ts, heads)
    q_tiled = jnp.tile(q_rows, (1, 1, groups))
    q_rows = jnp.pad(q_rows, ((0, 0), (0, 0), (0, LANES - HEAD_DIM)))
    slope_rows = jnp.asarray(np.repeat(slopes, ts).reshape(heads * ts, 1))
    gt_rows = gt[:, :3 * heads].reshape(bs, ts, heads, 3).transpose(0, 2, 1, 3).reshape(bs, heads * ts, 3)
    n_pages = p["page_table"].shape[1]
    past = n_pages * PAGE_SIZE
    cover = _nsa_cover(n_pages * _CHUNKS_PER_PAGE, max(-(-(past + ts) // SEL_BLOCK), SEL_TOPK))
    dec_w = []
    for w1, w2, pe in (cmp_k, cmp_v):
        w1j = w1.reshape(CMP_LEN, HEAD_DIM, HEAD_DIM)
        dec_w.append((jnp.concatenate([w1j[:CMP_STRIDE], w1j[CMP_STRIDE:]], axis=2).astype(BF16), pe,
                      jnp.pad(w2, ((0, LANES - HEAD_DIM), (0, LANES - HEAD_DIM))).astype(BF16)))
    (wjk, pek, w2k), (wjv, pev, w2v) = dec_w
    ab_k = nsa_page_ab(_tokens_minor(p["cache_cmp_k"]), l, wjk)
    ab_v = nsa_page_ab(_tokens_minor(p["cache_cmp_v"]), l, wjv)
    o_c, pen = nsa_decode_cmp(q_rows, slope_rows, ab_k, ab_v, p["page_table"], wjk, wjv, pek, pev, w2k, w2v,
                              cover, ts)
    att = nsa_decode_sel(q_tiled, slope_rows, pen, o_c, gt_rows, _new_kv_cols(ks, bs, ts),
                         _new_kv_cols(vs, bs, ts), _new_kv_cols(kw, bs, ts), _new_kv_cols(vw, bs, ts),
                         _tokens_minor(p["state_win_k"]), _tokens_minor(p["state_win_v"]),
                         _tokens_minor(p["cache_sel_k"]), _tokens_minor(p["cache_sel_v"]), p["page_table"], l)
    ys = proj_residual([_rows_to_heads(att, bs, ts, heads)], w_out, ys)
    w_buf = p["state_win_k"].shape[2]
    win = lambda state, new: jnp.concatenate([state[l], kv4(new, bs, ts)], axis=1)[:, -w_buf:]
    out_s = tuple(kv4(a, bs, ts) for a in (kc, vc, ks, vs)) + (win(p["state_win_k"], kw), win(p["state_win_v"], vw))
    return yp, ys, out_p, out_s


def kernel(x_prompt, x_sample, cache_moba_k, cache_moba_v, cache_cmp_k, cache_cmp_v, cache_sel_k, cache_sel_v,
           state_win_k, state_win_v, state_conv, state_rglru, page_table, norm_mix, norm_ffn, norm_final,
           w_in_ab, w_out_ab, conv_w, conv_b, w_rg_a, b_rg_a, w_rg_i, b_rg_i, rg_lambda, w_in_c, w_out_c,
           w_cmp_k1, w_cmp_k2, pe_cmp_k, w_cmp_v1, w_cmp_v2, pe_cmp_v, w_route_group, b_route_group,
           w_route_expert, b_route_expert, w_exp_gate, w_exp_up, w_exp_down):
    p = dict(cache_moba_k=cache_moba_k, cache_moba_v=cache_moba_v, cache_cmp_k=cache_cmp_k,
             cache_cmp_v=cache_cmp_v, cache_sel_k=cache_sel_k, cache_sel_v=cache_sel_v,
             state_win_k=state_win_k, state_win_v=state_win_v, state_conv=state_conv, state_rglru=state_rglru,
             page_table=page_table, norm_mix=norm_mix, w_in_ab=w_in_ab, w_out_ab=w_out_ab, conv_w=conv_w,
             conv_b=conv_b, w_rg_a=w_rg_a, b_rg_a=b_rg_a, w_rg_i=w_rg_i, b_rg_i=b_rg_i, rg_lambda=rg_lambda,
             w_in_c=w_in_c, w_out_c=w_out_c, w_cmp_k1=w_cmp_k1, w_cmp_k2=w_cmp_k2, pe_cmp_k=pe_cmp_k,
             w_cmp_v1=w_cmp_v1, w_cmp_v2=w_cmp_v2, pe_cmp_v=pe_cmp_v)
    bp, tp, d = x_prompt.shape
    bs, ts, _ = x_sample.shape
    dims = (bp, tp, bs, ts)
    yp = x_prompt.reshape(bp * tp, d)
    ys = x_sample.reshape(bs * ts, d)
    ab_p, ab_s, c_p, c_s = [], [], [], []
    for layer in range(norm_mix.shape[0]):
        l = layer // 2
        if layer % 2 == 0:
            yp, ys, o_p, o_s = _layer_ab(layer, l, yp, ys, dims, p)
            ab_p.append(o_p)
            ab_s.append(o_s)
        else:
            yp, ys, o_p, o_s = _layer_c(layer, l, yp, ys, dims, p)
            c_p.append(o_p)
            c_s.append(o_s)
        wr, br = _router_weights(w_route_group[layer], b_route_group[layer], w_route_expert[layer],
                                 b_route_expert[layer])
        moe_w = (w_exp_gate[layer].astype(BF16), w_exp_up[layer].astype(BF16), w_exp_down[layer].astype(BF16))
        yp = hier_moe_residual(yp, norm_ffn[layer], wr, br, *moe_w)
        ys = hier_moe_residual(ys, norm_ffn[layer], wr, br, *moe_w)
    y_prompt = rmsnorm(yp, norm_final).reshape(bp, tp, d)
    y_sample = rmsnorm(ys, norm_final).reshape(bs, ts, d)
    stack = lambda outs, i: jnp.stack([o[i] for o in outs])
    res = [y_prompt, y_sample]
    for i in range(4):
        res += [stack(ab_p, i), stack(ab_s, i)]
    for i in range(6):
        res += [stack(c_p, i), stack(c_s, i)]
    return tuple(res)
```
